```python
import math
import jax, jax.numpy as jnp
from jax import lax
import numpy as np

D_MODEL = 1024
BATCH = 4
SEQ = 4096
DEPTH = 2

POOL_WINDOWS = (2, 4, 8, 16)
N_POOL_GROUPS = len(POOL_WINDOWS)
POOL_GROUP_DIM = D_MODEL // 8
POOL_DIM = N_POOL_GROUPS * POOL_GROUP_DIM

N_HEADS = D_MODEL // 128
QK_NOPE = 64
QK_ROPE = 32
V_DIM = 64
Q_LORA = 384
KV_LORA = 256
ROPE_THETA = 10000.0
ATTN_DIM = N_HEADS * V_DIM
Q_BLOCK = 128

D_FF = 4 * D_MODEL

N_MOD = 6
EPS = 1e-6

IN_SPLITS = (POOL_DIM, Q_LORA, KV_LORA, QK_ROPE, D_MODEL, D_MODEL)
D_IN = sum(IN_SPLITS)

kernel_name = "hybrid_pool_mla_gated_adaln"


def rms_norm(x, g):
    xf = x.astype(jnp.float32)
    y = xf * lax.rsqrt(jnp.mean(xf * xf, axis=-1, keepdims=True) + EPS)
    return y.astype(x.dtype) * g


def apply_rope(x, cos, sin):
    half = x.shape[-1] // 2
    x1, x2 = x[..., :half], x[..., half:]
    return jnp.concatenate([x1 * cos - x2 * sin, x2 * cos + x1 * sin], axis=-1)


def pool_mixer(u, w_pool, pool_scale):
    B, S, _ = u.shape
    t = jnp.arange(S)
    outs = []
    for g, w in enumerate(POOL_WINDOWS):
        ug = u[..., g * POOL_GROUP_DIM:(g + 1) * POOL_GROUP_DIM].astype(jnp.float32)
        cs = jnp.cumsum(ug, axis=1)
        lag = jnp.pad(cs, ((0, 0), (w, 0), (0, 0)))[:, :S]
        cnt = jnp.minimum(t + 1, w).astype(jnp.float32)[None, :, None]
        outs.append(((cs - lag) / cnt - ug).astype(u.dtype))
    p = jnp.stack(outs, axis=2)
    y = jnp.einsum('bsgc,gcd->bsgd', p, w_pool).reshape(B, S, POOL_DIM)
    return y * pool_scale


def mla(c_q_raw, c_kv_raw, k_rope_raw, q_norm_g, w_uq, kv_norm_g, w_uk, w_uv, cos, sin):
    B, S, _ = c_q_raw.shape
    c_q = rms_norm(c_q_raw, q_norm_g)
    q = jnp.einsum('bsr,rhd->bshd', c_q, w_uq)
    q_nope = q[..., :QK_NOPE]
    q_rope = apply_rope(q[..., QK_NOPE:], cos[:, :, None, :], sin[:, :, None, :])
    c_kv = rms_norm(c_kv_raw, kv_norm_g)
    k_nope = jnp.einsum('bsr,rhd->bhsd', c_kv, w_uk)
    v = jnp.einsum('bsr,rhd->bhsd', c_kv, w_uv)
    k_rope = apply_rope(k_rope_raw, cos, sin)

    nb = S // Q_BLOCK
    def to_blocks(a):
        d = a.shape[-1]
        return a.reshape(B, nb, Q_BLOCK, N_HEADS, d).transpose(1, 0, 3, 2, 4)
    qn_b = to_blocks(q_nope)
    qr_b = to_blocks(q_rope)
    starts = jnp.arange(nb, dtype=jnp.int32) * Q_BLOCK
    key_pos = jnp.arange(S, dtype=jnp.int32)
    scale = 1.0 / math.sqrt(QK_NOPE + QK_ROPE)
    neg = jnp.finfo(jnp.float32).min

    def attend(args):
        qn, qr, start = args
        s = (jnp.einsum('bhqd,bhkd->bhqk', qn, k_nope)
             + jnp.einsum('bhqd,bkd->bhqk', qr, k_rope)).astype(jnp.float32) * scale
        q_pos = start + jnp.arange(Q_BLOCK, dtype=jnp.int32)
        mask = q_pos[:, None] >= key_pos[None, :]
        p = jax.nn.softmax(jnp.where(mask, s, neg), axis=-1)
        return jnp.einsum('bhqk,bhkd->bhqd', p.astype(v.dtype), v)

    o = lax.map(attend, (qn_b, qr_b, starts))
    return o.transpose(1, 0, 3, 2, 4).reshape(B, S, ATTN_DIM)


def setup_inputs(seed: int = 0) -> dict:
    key = jax.random.key(seed)
    ks = jax.random.split(key, 24)
    f32 = jnp.float32

    def dense(k, shape, fan_in, mult=1.0):
        return jax.random.normal(k, shape, f32) * (mult * fan_in ** -0.5)

    def gain(k, shape):
        return 1.0 + 0.02 * jax.random.normal(k, shape, f32)

    x = jax.random.normal(ks[0], (BATCH, SEQ, D_MODEL), f32)
    c = jax.random.normal(ks[1], (BATCH, D_MODEL), f32)
    offsets = jax.random.randint(ks[2], (BATCH, 1), 0, 1024, dtype=jnp.int32)
    positions = offsets + jnp.arange(SEQ, dtype=jnp.int32)[None, :]
    return {
        "x": x,
        "c": c,
        "positions": positions,
        "ln1_g": gain(ks[3], (DEPTH, D_MODEL)),
        "ln2_g": gain(ks[4], (DEPTH, D_MODEL)),
        "w_ada": dense(ks[5], (DEPTH, D_MODEL, N_MOD * D_MODEL), D_MODEL, 0.5),
        "b_ada": 0.01 * jax.random.normal(ks[6], (DEPTH, N_MOD * D_MODEL), f32),
        "w_in": dense(ks[7], (DEPTH, D_MODEL, D_IN), D_MODEL),
        "q_norm_g": gain(ks[8], (DEPTH, Q_LORA)),
        "w_uq": dense(ks[9], (DEPTH, Q_LORA, N_HEADS, QK_NOPE + QK_ROPE), Q_LORA),
        "kv_norm_g": gain(ks[10], (DEPTH, KV_LORA)),
        "w_uk": dense(ks[11], (DEPTH, KV_LORA, N_HEADS, QK_NOPE), KV_LORA),
        "w_uv": dense(ks[12], (DEPTH, KV_LORA, N_HEADS, V_DIM), KV_LORA),
        "w_pool": dense(ks[13], (DEPTH, N_POOL_GROUPS, POOL_GROUP_DIM, POOL_GROUP_DIM), POOL_GROUP_DIM),
        "pool_scale": gain(ks[14], (DEPTH, POOL_DIM)),
        "p_pool": dense(ks[15], (DEPTH, POOL_DIM, D_MODEL), POOL_DIM),
        "p_attn": dense(ks[16], (DEPTH, ATTN_DIM, D_MODEL), ATTN_DIM),
        "w_out": dense(ks[17], (DEPTH, D_MODEL, D_MODEL), D_MODEL),
        "w_ff1": dense(ks[18], (DEPTH, D_MODEL, D_FF), D_MODEL),
        "w_ff2": dense(ks[19], (DEPTH, D_FF, D_MODEL), D_FF),
        "final_g": gain(ks[20], (D_MODEL,)),
    }


def reference(x, c, positions, ln1_g, ln2_g, w_ada, b_ada, w_in, q_norm_g, w_uq,
              kv_norm_g, w_uk, w_uv, w_pool, pool_scale, p_pool, p_attn, w_out,
              w_ff1, w_ff2, final_g):
    inv_freq = ROPE_THETA ** (-jnp.arange(0, QK_ROPE, 2, dtype=jnp.float32) / QK_ROPE)
    ang = positions.astype(jnp.float32)[..., None] * inv_freq
    cos = jnp.cos(ang).astype(x.dtype)
    sin = jnp.sin(ang).astype(x.dtype)
    c_act = jax.nn.silu(c)
    cuts = np.cumsum(IN_SPLITS)[:-1].tolist()

    for l in range(DEPTH):
        mod = c_act @ w_ada[l] + b_ada[l]
        sh1, sc1, g1, sh2, sc2, g2 = [m[:, None, :] for m in jnp.split(mod, N_MOD, axis=-1)]

        h = rms_norm(x, ln1_g[l]) * (1.0 + sc1) + sh1
        z = h @ w_in[l]
        u_pool, c_q_raw, c_kv_raw, k_rope_raw, gz_a, gz_b = jnp.split(z, cuts, axis=-1)
        y_a = pool_mixer(u_pool, w_pool[l], pool_scale[l]) @ p_pool[l]
        y_b = mla(c_q_raw, c_kv_raw, k_rope_raw, q_norm_g[l], w_uq[l], kv_norm_g[l],
                  w_uk[l], w_uv[l], cos, sin) @ p_attn[l]
        merged = jax.nn.sigmoid(gz_a) * y_a + jax.nn.sigmoid(gz_b) * y_b
        x = x + g1 * (merged @ w_out[l])

        h2 = rms_norm(x, ln2_g[l]) * (1.0 + sc2) + sh2
        x = x + g2 * (jnp.square(jax.nn.relu(h2 @ w_ff1[l])) @ w_ff2[l])

    return rms_norm(x, final_g)
```

```python
import functools
import math

import jax
import jax.numpy as jnp
from jax import lax
from jax.experimental import pallas as pl
from jax.experimental.pallas import tpu as pltpu

D_MODEL = 1024
N_HEADS = 8
QK_NOPE = 64
QK_ROPE = 32
V_DIM = 64
Q_LORA = 384
KV_LORA = 256
POOL_WINDOWS = (2, 4, 8, 16)
POOL_GROUP_DIM = 128
POOL_DIM = len(POOL_WINDOWS) * POOL_GROUP_DIM
ATTN_DIM = N_HEADS * V_DIM
D_FF = 4 * D_MODEL
N_MOD = 6
EPS = 1e-6
ROPE_THETA = 10000.0

HEAD_PAD = 128
ROPE_LO = QK_NOPE
ROPE_HI = QK_NOPE + QK_ROPE
POOL_HALO = 16

VMEM_LIMIT_BYTES = 56 * 1024 * 1024

F32 = jnp.float32
BF16 = jnp.bfloat16

TM_QKV = 512
TM_MIX = 512
TQ = 256
TK = 256
FF_CHUNK = 1024
MOD_TN = 1536


def _nt_dot(a, b):
    return lax.dot_general(a, b, (((1,), (1,)), ((), ())), preferred_element_type=F32)


def _tn_dot(a, b):
    return lax.dot_general(a, b, (((0,), (0,)), ((), ())), preferred_element_type=F32)


def _rms(x):
    return x * lax.rsqrt(jnp.mean(x * x, axis=-1, keepdims=True) + EPS)


def _sigmoid(x):
    return 1.0 / (1.0 + jnp.exp(-x))


def _rope_tables_kernel(pos_col_ref, pos_row_ref, invf_row_ref, invf_col_ref,
                        cos_ref, sin_ref, cost_ref, sint_ref):
    ang = pos_col_ref[0].astype(F32) * invf_row_ref[...]
    cos_ref[0] = jnp.cos(ang)
    sin_ref[0] = jnp.sin(ang)
    angt = invf_col_ref[...] * pos_row_ref[0].astype(F32)
    cost_ref[0] = jnp.cos(angt)
    sint_ref[0] = jnp.sin(angt)


def _rope_tables(positions):
    b, s = positions.shape
    inv_freq = ROPE_THETA ** (-jnp.arange(0, QK_ROPE, 2, dtype=F32) / QK_ROPE)
    two = jnp.concatenate([inv_freq, inv_freq])
    invf_row = jnp.zeros((1, HEAD_PAD), F32).at[0, ROPE_LO:ROPE_HI].set(two)
    invf_col = two.reshape(QK_ROPE, 1)
    return pl.pallas_call(
        _rope_tables_kernel,
        grid=(b,),
        in_specs=[
            pl.BlockSpec((1, s, 1), lambda i: (i, 0, 0)),
            pl.BlockSpec((1, 1, s), lambda i: (i, 0, 0)),
            pl.BlockSpec((1, HEAD_PAD), lambda i: (0, 0)),
            pl.BlockSpec((QK_ROPE, 1), lambda i: (0, 0)),
        ],
        out_specs=[
            pl.BlockSpec((1, s, HEAD_PAD), lambda i: (i, 0, 0)),
            pl.BlockSpec((1, s, HEAD_PAD), lambda i: (i, 0, 0)),
            pl.BlockSpec((1, QK_ROPE, s), lambda i: (i, 0, 0)),
            pl.BlockSpec((1, QK_ROPE, s), lambda i: (i, 0, 0)),
        ],
        out_shape=[
            jax.ShapeDtypeStruct((b, s, HEAD_PAD), F32),
            jax.ShapeDtypeStruct((b, s, HEAD_PAD), F32),
            jax.ShapeDtypeStruct((b, QK_ROPE, s), F32),
            jax.ShapeDtypeStruct((b, QK_ROPE, s), F32),
        ],
        compiler_params=pltpu.CompilerParams(vmem_limit_bytes=VMEM_LIMIT_BYTES),
        name="rope_tables",
    )(positions.reshape(b, s, 1), positions.reshape(b, 1, s), invf_row, invf_col)


def _mod_kernel(c_ref, w_ref, b_ref, o_ref):
    c = c_ref[...]
    c_act = c * _sigmoid(c)
    o_ref[0] = jnp.dot(c_act.astype(BF16), w_ref[0].astype(BF16),
                       preferred_element_type=F32) + b_ref[0]


def _modulation(c, w_ada, b_ada):
    depth, d, n = w_ada.shape
    b = c.shape[0]
    rows = 8
    c_pad = jnp.zeros((rows, d), F32).at[:b].set(c)
    out = pl.pallas_call(
        _mod_kernel,
        grid=(depth, n // MOD_TN),
        in_specs=[
            pl.BlockSpec((rows, d), lambda l, j: (0, 0)),
            pl.BlockSpec((1, d, MOD_TN), lambda l, j: (l, 0, j)),
            pl.BlockSpec((1, 1, MOD_TN), lambda l, j: (l, 0, j)),
        ],
        out_specs=pl.BlockSpec((1, rows, MOD_TN), lambda l, j: (l, 0, j)),
        out_shape=jax.ShapeDtypeStruct((depth, rows, n), F32),
        compiler_params=pltpu.CompilerParams(vmem_limit_bytes=VMEM_LIMIT_BYTES),
        name="adaln_mod",
    )(c_pad, w_ada, b_ada.reshape(depth, 1, n))
    return out[:, :b].reshape(depth, b, N_MOD, d)


def _qkv_kernel(x_ref, mod_ref, g_ref, w_in_ref, gq_ref, gkv_ref, wuqt_ref, wuqrt_ref,
                wuk_ref, wuvt_ref, cos_ref, sin_ref, cost_ref, sint_ref,
                qt_ref, k_ref, vt_ref, *, scale):
    x = x_ref[0]
    shift = mod_ref[0, 0:1, :]
    scl = mod_ref[0, 1:2, :]
    h = (_rms(x) * g_ref[...]) * (1.0 + scl) + shift
    z = jnp.dot(h.astype(BF16), w_in_ref[...], preferred_element_type=F32)
    c_q = z[:, 0:Q_LORA]
    c_kv = z[:, Q_LORA:Q_LORA + KV_LORA]
    kr = z[:, Q_LORA + KV_LORA:Q_LORA + KV_LORA + HEAD_PAD]
    kr_rot = z[:, Q_LORA + KV_LORA + HEAD_PAD:]
    cqn = (_rms(c_q) * gq_ref[...]).astype(BF16)
    ckvn = (_rms(c_kv) * gkv_ref[...]).astype(BF16)

    qt = _nt_dot(wuqt_ref[...], cqn)
    qt_rot = _nt_dot(wuqrt_ref[...], cqn)
    cost = cost_ref[0]
    sint = sint_ref[0]
    for hd in range(N_HEADS):
        base = hd * HEAD_PAD
        qt_ref[0, base:base + ROPE_LO, :] = (qt[base:base + ROPE_LO] * scale).astype(BF16)
        roped = (qt[base + ROPE_LO:base + ROPE_HI] * cost
                 + qt_rot[hd * QK_ROPE:(hd + 1) * QK_ROPE] * sint)
        qt_ref[0, base + ROPE_LO:base + ROPE_HI, :] = (roped * scale).astype(BF16)
        qt_ref[0, base + ROPE_HI:base + HEAD_PAD, :] = (
            qt[base + ROPE_HI:base + HEAD_PAD] * scale).astype(BF16)

    k = jnp.dot(ckvn, wuk_ref[...], preferred_element_type=F32)
    kr_full = kr * cos_ref[0] + kr_rot * sin_ref[0]
    for hd in range(N_HEADS):
        base = hd * HEAD_PAD
        k_ref[0, :, base:base + HEAD_PAD] = (k[:, base:base + HEAD_PAD] + kr_full).astype(BF16)

    vt_ref[0] = _nt_dot(wuvt_ref[...], ckvn).astype(BF16)


def _qkv(x, mod_l, ln_g, w_in_a, gq, gkv, wuqt, wuqrt, wuk, wuvt, tables):
    b, s, d = x.shape
    tm = TM_QKV
    cos128, sin128, cost, sint = tables
    const = lambda shape: pl.BlockSpec(shape, lambda i, j: (0,) * len(shape))
    scale = 1.0 / math.sqrt(QK_NOPE + QK_ROPE)
    return pl.pallas_call(
        functools.partial(_qkv_kernel, scale=scale),
        grid=(b, s // tm),
        in_specs=[
            pl.BlockSpec((1, tm, d), lambda i, j: (i, j, 0)),
            pl.BlockSpec((1, N_MOD, d), lambda i, j: (i, 0, 0)),
            const((1, d)),
            const(w_in_a.shape),
            const((1, Q_LORA)),
            const((1, KV_LORA)),
            const(wuqt.shape),
            const(wuqrt.shape),
            const(wuk.shape),
            const(wuvt.shape),
            pl.BlockSpec((1, tm, HEAD_PAD), lambda i, j: (i, j, 0)),
            pl.BlockSpec((1, tm, HEAD_PAD), lambda i, j: (i, j, 0)),
            pl.BlockSpec((1, QK_ROPE, tm), lambda i, j: (i, 0, j)),
            pl.BlockSpec((1, QK_ROPE, tm), lambda i, j: (i, 0, j)),
        ],
        out_specs=[
            pl.BlockSpec((1, N_HEADS * HEAD_PAD, tm), lambda i, j: (i, 0, j)),
            pl.BlockSpec((1, tm, N_HEADS * HEAD_PAD), lambda i, j: (i, j, 0)),
            pl.BlockSpec((1, ATTN_DIM, tm), lambda i, j: (i, 0, j)),
        ],
        out_shape=[
            jax.ShapeDtypeStruct((b, N_HEADS * HEAD_PAD, s), BF16),
            jax.ShapeDtypeStruct((b, s, N_HEADS * HEAD_PAD), BF16),
            jax.ShapeDtypeStruct((b, ATTN_DIM, s), BF16),
        ],
        compiler_params=pltpu.CompilerParams(
            dimension_semantics=("arbitrary", "arbitrary"),
            vmem_limit_bytes=VMEM_LIMIT_BYTES),
        name="qkv_proj",
    )(x, mod_l, ln_g, w_in_a, gq, gkv, wuqt, wuqrt, wuk, wuvt, cos128, sin128, cost, sint)


def _attn_kernel(qt_ref, k_ref, vt_ref, o_ref):
    qi = pl.program_id(2)
    qt = qt_ref[0]

    def scores(off):
        kj = k_ref[0, pl.ds(off, TK), :]
        return jnp.dot(kj, qt, preferred_element_type=F32)

    def values(off):
        return vt_ref[0, :, pl.ds(off, TK)]

    off_d = pl.multiple_of(qi * TQ, TQ)
    s = scores(off_d)
    key_pos = lax.broadcasted_iota(jnp.int32, (TK, TQ), 0)
    qry_pos = lax.broadcasted_iota(jnp.int32, (TK, TQ), 1)
    s = jnp.where(qry_pos >= key_pos, s, jnp.finfo(F32).min)
    m0 = jnp.max(s, axis=0, keepdims=True)
    p = jnp.exp(s - m0)
    l0 = jnp.sum(p, axis=0, keepdims=True)
    acc0 = jnp.dot(values(off_d), p.astype(BF16), preferred_element_type=F32)

    def body(j, carry):
        m, l, acc = carry
        off = pl.multiple_of(j * TK, TK)
        s = scores(off)
        m_new = jnp.maximum(m, jnp.max(s, axis=0, keepdims=True))
        alpha = jnp.exp(m - m_new)
        p = jnp.exp(s - m_new)
        l = alpha * l + jnp.sum(p, axis=0, keepdims=True)
        acc = alpha * acc + jnp.dot(values(off), p.astype(BF16), preferred_element_type=F32)
        return m_new, l, acc

    _, l, acc = lax.fori_loop(0, qi, body, (m0, l0, acc0))
    o_ref[0] = (acc * (1.0 / l)).astype(o_ref.dtype)


def _attention(qt, k, vt):
    b, _, s = qt.shape
    return pl.pallas_call(
        _attn_kernel,
        grid=(b, N_HEADS, s // TQ),
        in_specs=[
            pl.BlockSpec((1, HEAD_PAD, TQ), lambda i, h, q: (i, h, q)),
            pl.BlockSpec((1, s, HEAD_PAD), lambda i, h, q: (i, 0, h)),
            pl.BlockSpec((1, V_DIM, s), lambda i, h, q: (i, h, 0)),
        ],
        out_specs=pl.BlockSpec((1, V_DIM, TQ), lambda i, h, q: (i, h, q)),
        out_shape=jax.ShapeDtypeStruct((b, ATTN_DIM, s), BF16),
        compiler_params=pltpu.CompilerParams(
            dimension_semantics=("arbitrary", "arbitrary", "arbitrary"),
            vmem_limit_bytes=VMEM_LIMIT_BYTES),
        name="mla_attention",
    )(qt, k, vt)


def _mix_kernel(x_ref, ot_ref, mod_ref, g1_ref, g2_ref, gf_ref, w_in_ref, w_pool_ref, pscale_ref,
                p_pool_ref, p_attn_ref, w_out_ref, w_ff1_ref, w_ff2_ref,
                o_ref, uext_ref, *, final):
    tm = x_ref.shape[1]
    si = pl.program_id(1)
    x = x_ref[0]
    shift1, scale1, gate1 = mod_ref[0, 0:1, :], mod_ref[0, 1:2, :], mod_ref[0, 2:3, :]
    shift2, scale2, gate2 = mod_ref[0, 3:4, :], mod_ref[0, 4:5, :], mod_ref[0, 5:6, :]

    h = (_rms(x) * g1_ref[...]) * (1.0 + scale1) + shift1
    z = jnp.dot(h.astype(BF16), w_in_ref[...], preferred_element_type=F32)
    u = z[:, 0:POOL_DIM]
    gz_a = z[:, POOL_DIM:POOL_DIM + D_MODEL]
    gz_b = z[:, POOL_DIM + D_MODEL:]

    @pl.when(si == 0)
    def _():
        uext_ref[0:POOL_HALO, :] = jnp.zeros((POOL_HALO, POOL_DIM), F32)

    @pl.when(si > 0)
    def _():
        uext_ref[0:POOL_HALO, :] = uext_ref[tm:tm + POOL_HALO, :]

    uext_ref[POOL_HALO:POOL_HALO + tm, :] = u
    t_pos = si * tm + lax.broadcasted_iota(jnp.int32, (tm, 1), 0)
    pooled = []
    for g, w in enumerate(POOL_WINDOWS):
        cols = pl.ds(g * POOL_GROUP_DIM, POOL_GROUP_DIM)
        ug = u[:, g * POOL_GROUP_DIM:(g + 1) * POOL_GROUP_DIM]
        acc = ug
        for kk in range(1, w):
            acc = acc + uext_ref[pl.ds(POOL_HALO - kk, tm), cols]
        inv_cnt = 1.0 / jnp.minimum(t_pos + 1, w).astype(F32)
        pg = acc * inv_cnt - ug
        yg = jnp.dot(pg.astype(BF16), w_pool_ref[g], preferred_element_type=F32)
        pooled.append(yg)
    y_pool = jnp.concatenate(pooled, axis=-1) * pscale_ref[...]
    y_a = jnp.dot(y_pool.astype(BF16), p_pool_ref[...], preferred_element_type=F32)

    y_b = _tn_dot(ot_ref[0], p_attn_ref[...])
    merged = _sigmoid(gz_a) * y_a + _sigmoid(gz_b) * y_b
    x1 = x + gate1 * jnp.dot(merged.astype(BF16), w_out_ref[...], preferred_element_type=F32)

    h2 = ((_rms(x1) * g2_ref[...]) * (1.0 + scale2) + shift2).astype(BF16)
    ff = jnp.zeros((tm, D_MODEL), F32)
    for c0 in range(0, D_FF, FF_CHUNK):
        t = jnp.dot(h2, w_ff1_ref[:, c0:c0 + FF_CHUNK], preferred_element_type=F32)
        t = jnp.square(jnp.maximum(t, 0.0)).astype(BF16)
        ff = ff + jnp.dot(t, w_ff2_ref[c0:c0 + FF_CHUNK, :], preferred_element_type=F32)
    x2 = x1 + gate2 * ff
    if final:
        x2 = _rms(x2) * gf_ref[...]
    o_ref[0] = x2


def _mix(x, ot, mod_l, g1, g2, gf, w_in_b, w_pool, pscale, p_pool, p_attn, w_out, w_ff1, w_ff2,
         *, final):
    b, s, d = x.shape
    tm = TM_MIX

    def const(shape):
        return pl.BlockSpec(shape, lambda i, j: (0,) * len(shape), pipeline_mode=pl.Buffered(1))

    return pl.pallas_call(
        functools.partial(_mix_kernel, final=final),
        grid=(b, s // tm),
        in_specs=[
            pl.BlockSpec((1, tm, d), lambda i, j: (i, j, 0)),
            pl.BlockSpec((1, ATTN_DIM, tm), lambda i, j: (i, 0, j)),
            pl.BlockSpec((1, N_MOD, d), lambda i, j: (i, 0, 0)),
            const((1, d)),
            const((1, d)),
            const((1, d)),
            const(w_in_b.shape),
            const(w_pool.shape),
            const((1, POOL_DIM)),
            const(p_pool.shape),
            const(p_attn.shape),
            const(w_out.shape),
            const(w_ff1.shape),
            const(w_ff2.shape),
        ],
        out_specs=pl.BlockSpec((1, tm, d), lambda i, j: (i, j, 0)),
        out_shape=jax.ShapeDtypeStruct((b, s, d), F32),
        scratch_shapes=[pltpu.VMEM((tm + POOL_HALO, POOL_DIM), F32)],
        compiler_params=pltpu.CompilerParams(
            dimension_semantics=("arbitrary", "arbitrary"),
            vmem_limit_bytes=VMEM_LIMIT_BYTES),
        name="mix_mlp",
    )(x, ot, mod_l, g1, g2, gf, w_in_b, w_pool, pscale, p_pool, p_attn, w_out, w_ff1, w_ff2)


def _layer_weights(w_in, w_uq, w_uk, w_uv):
    d = w_in.shape[0]
    c0 = POOL_DIM
    c1 = c0 + Q_LORA
    c2 = c1 + KV_LORA
    c3 = c2 + QK_ROPE
    half = QK_ROPE // 2
    w_kr = w_in[:, c2:c3]
    zl = jnp.zeros((d, ROPE_LO), F32)
    zr = jnp.zeros((d, HEAD_PAD - ROPE_HI), F32)
    kr_pad = jnp.concatenate([zl, w_kr, zr], axis=1)
    kr_rot_pad = jnp.concatenate([zl, -w_kr[:, half:], w_kr[:, :half], zr], axis=1)
    w_in_a = jnp.concatenate([w_in[:, c0:c2], kr_pad, kr_rot_pad], axis=1).astype(BF16)
    w_in_b = jnp.concatenate([w_in[:, :c0], w_in[:, c3:]], axis=1).astype(BF16)

    pad_q = HEAD_PAD - (QK_NOPE + QK_ROPE)
    wuq_pad = jnp.pad(w_uq, ((0, 0), (0, 0), (0, pad_q)))
    wuqt = wuq_pad.reshape(Q_LORA, N_HEADS * HEAD_PAD).T.astype(BF16)
    r = w_uq[:, :, QK_NOPE:]
    rot = jnp.concatenate([-r[..., half:], r[..., :half]], axis=-1)
    wuqrt = rot.reshape(Q_LORA, N_HEADS * QK_ROPE).T.astype(BF16)
    wuk = jnp.pad(w_uk, ((0, 0), (0, 0), (0, HEAD_PAD - QK_NOPE)))
    wuk = wuk.reshape(KV_LORA, N_HEADS * HEAD_PAD).astype(BF16)
    wuvt = w_uv.reshape(KV_LORA, ATTN_DIM).T.astype(BF16)
    return w_in_a, w_in_b, wuqt, wuqrt, wuk, wuvt


def kernel(x, c, positions, ln1_g, ln2_g, w_ada, b_ada, w_in, q_norm_g, w_uq, kv_norm_g, w_uk,
           w_uv, w_pool, pool_scale, p_pool, p_attn, w_out, w_ff1, w_ff2, final_g):
    depth = w_in.shape[0]
    tables = _rope_tables(positions)
    mod = _modulation(c, w_ada, b_ada)
    gf = final_g.reshape(1, D_MODEL)
    for l in range(depth):
        w_in_a, w_in_b, wuqt, wuqrt, wuk, wuvt = _layer_weights(w_in[l], w_uq[l], w_uk[l], w_uv[l])
        g1 = ln1_g[l].reshape(1, D_MODEL)
        g2 = ln2_g[l].reshape(1, D_MODEL)
        qt, k, vt = _qkv(x, mod[l], g1, w_in_a, q_norm_g[l].reshape(1, Q_LORA),
                         kv_norm_g[l].reshape(1, KV_LORA), wuqt, wuqrt, wuk, wuvt, tables)
        ot = _attention(qt, k, vt)
        x = _mix(x, ot, mod[l], g1, g2, gf, w_in_b, w_pool[l].astype(BF16),
                 pool_scale[l].reshape(1, POOL_DIM), p_pool[l].astype(BF16), p_attn[l].astype(BF16),
                 w_out[l].astype(BF16), w_ff1[l].astype(BF16), w_ff2[l].astype(BF16),
                 final=(l == depth - 1))
    return x
```

```python
import functools
import math

import jax
import jax.numpy as jnp
from jax import lax
from jax.experimental import pallas as pl
from jax.experimental.pallas import tpu as pltpu

D_MODEL = 1024
N_HEADS = 8
QK_NOPE = 64
QK_ROPE = 32
V_DIM = 64
Q_LORA = 384
KV_LORA = 256
POOL_WINDOWS = (2, 4, 8, 16)
POOL_GROUP_DIM = 128
POOL_DIM = len(POOL_WINDOWS) * POOL_GROUP_DIM
ATTN_DIM = N_HEADS * V_DIM
D_FF = 4 * D_MODEL
N_MOD = 6
EPS = 1e-6
ROPE_THETA = 10000.0

HEAD_PAD = 128
ROPE_LO = QK_NOPE
ROPE_HI = QK_NOPE + QK_ROPE
POOL_HALO = 16

VMEM_LIMIT_BYTES = 56 * 1024 * 1024

F32 = jnp.float32
BF16 = jnp.bfloat16

TM_QKV = 512
TM_MIX = 512
TQ = 512
TK = 512
ATTN_HEADS = 4
FF_CHUNK = 1024
MOD_TN = 1536


def _nt_dot(a, b):
    return lax.dot_general(a, b, (((1,), (1,)), ((), ())), preferred_element_type=F32)


def _tn_dot(a, b):
    return lax.dot_general(a, b, (((0,), (0,)), ((), ())), preferred_element_type=F32)


def _rms(x):
    return x * lax.rsqrt(jnp.mean(x * x, axis=-1, keepdims=True) + EPS)


def _sigmoid(x):
    return 1.0 / (1.0 + jnp.exp(-x))


def _rope_tables_kernel(pos_col_ref, pos_row_ref, invf_row_ref, invf_col_ref,
                        cos_ref, sin_ref, cost_ref, sint_ref):
    ang = pos_col_ref[0].astype(F32) * invf_row_ref[...]
    cos_ref[0] = jnp.cos(ang)
    sin_ref[0] = jnp.sin(ang)
    angt = invf_col_ref[...] * pos_row_ref[0].astype(F32)
    cost_ref[0] = jnp.cos(angt)
    sint_ref[0] = jnp.sin(angt)


def _rope_tables(positions):
    b, s = positions.shape
    inv_freq = ROPE_THETA ** (-jnp.arange(0, QK_ROPE, 2, dtype=F32) / QK_ROPE)
    two = jnp.concatenate([inv_freq, inv_freq])
    invf_row = jnp.zeros((1, HEAD_PAD), F32).at[0, ROPE_LO:ROPE_HI].set(two)
    invf_col = two.reshape(QK_ROPE, 1)
    return pl.pallas_call(
        _rope_tables_kernel,
        grid=(b,),
        in_specs=[
            pl.BlockSpec((1, s, 1), lambda i: (i, 0, 0)),
            pl.BlockSpec((1, 1, s), lambda i: (i, 0, 0)),
            pl.BlockSpec((1, HEAD_PAD), lambda i: (0, 0)),
            pl.BlockSpec((QK_ROPE, 1), lambda i: (0, 0)),
        ],
        out_specs=[
            pl.BlockSpec((1, s, HEAD_PAD), lambda i: (i, 0, 0)),
            pl.BlockSpec((1, s, HEAD_PAD), lambda i: (i, 0, 0)),
            pl.BlockSpec((1, QK_ROPE, s), lambda i: (i, 0, 0)),
            pl.BlockSpec((1, QK_ROPE, s), lambda i: (i, 0, 0)),
        ],
        out_shape=[
            jax.ShapeDtypeStruct((b, s, HEAD_PAD), F32),
            jax.ShapeDtypeStruct((b, s, HEAD_PAD), F32),
            jax.ShapeDtypeStruct((b, QK_ROPE, s), F32),
            jax.ShapeDtypeStruct((b, QK_ROPE, s), F32),
        ],
        compiler_params=pltpu.CompilerParams(vmem_limit_bytes=VMEM_LIMIT_BYTES),
        name="rope_tables",
    )(positions.reshape(b, s, 1), positions.reshape(b, 1, s), invf_row, invf_col)


def _mod_kernel(c_ref, w_ref, b_ref, o_ref):
    c = c_ref[...]
    c_act = c * _sigmoid(c)
    o_ref[0] = jnp.dot(c_act.astype(BF16), w_ref[0].astype(BF16),
                       preferred_element_type=F32) + b_ref[0]


def _modulation(c, w_ada, b_ada):
    depth, d, n = w_ada.shape
    b = c.shape[0]
    rows = 8
    c_pad = jnp.zeros((rows, d), F32).at[:b].set(c)
    out = pl.pallas_call(
        _mod_kernel,
        grid=(depth, n // MOD_TN),
        in_specs=[
            pl.BlockSpec((rows, d), lambda l, j: (0, 0)),
            pl.BlockSpec((1, d, MOD_TN), lambda l, j: (l, 0, j)),
            pl.BlockSpec((1, 1, MOD_TN), lambda l, j: (l, 0, j)),
        ],
        out_specs=pl.BlockSpec((1, rows, MOD_TN), lambda l, j: (l, 0, j)),
        out_shape=jax.ShapeDtypeStruct((depth, rows, n), F32),
        compiler_params=pltpu.CompilerParams(vmem_limit_bytes=VMEM_LIMIT_BYTES),
        name="adaln_mod",
    )(c_pad, w_ada, b_ada.reshape(depth, 1, n))
    return out[:, :b].reshape(depth, b, N_MOD, d)


def _qkv_kernel(x_ref, mod_ref, g_ref, w_in_ref, gq_ref, gkv_ref, wuqt_ref, wuqrt_ref,
                wuk_ref, wuvt_ref, cos_ref, sin_ref, cost_ref, sint_ref,
                qt_ref, k_ref, vt_ref, *, scale):
    x = x_ref[0]
    shift = mod_ref[0, 0:1, :]
    scl = mod_ref[0, 1:2, :]
    h = (_rms(x) * g_ref[...]) * (1.0 + scl) + shift
    z = jnp.dot(h.astype(BF16), w_in_ref[...], preferred_element_type=F32)
    c_q = z[:, 0:Q_LORA]
    c_kv = z[:, Q_LORA:Q_LORA + KV_LORA]
    kr = z[:, Q_LORA + KV_LORA:Q_LORA + KV_LORA + HEAD_PAD]
    kr_rot = z[:, Q_LORA + KV_LORA + HEAD_PAD:]
    cqn = (_rms(c_q) * gq_ref[...]).astype(BF16)
    ckvn = (_rms(c_kv) * gkv_ref[...]).astype(BF16)

    qt = _nt_dot(wuqt_ref[...], cqn)
    qt_rot = _nt_dot(wuqrt_ref[...], cqn)
    cost = cost_ref[0]
    sint = sint_ref[0]
    for hd in range(N_HEADS):
        base = hd * HEAD_PAD
        qt_ref[0, base:base + ROPE_LO, :] = (qt[base:base + ROPE_LO] * scale).astype(BF16)
        roped = (qt[base + ROPE_LO:base + ROPE_HI] * cost
                 + qt_rot[hd * QK_ROPE:(hd + 1) * QK_ROPE] * sint)
        qt_ref[0, base + ROPE_LO:base + ROPE_HI, :] = (roped * scale).astype(BF16)
        qt_ref[0, base + ROPE_HI:base + HEAD_PAD, :] = (
            qt[base + ROPE_HI:base + HEAD_PAD] * scale).astype(BF16)

    k = jnp.dot(ckvn, wuk_ref[...], preferred_element_type=F32)
    kr_full = kr * cos_ref[0] + kr_rot * sin_ref[0]
    for hd in range(N_HEADS):
        base = hd * HEAD_PAD
        k_ref[0, :, base:base + HEAD_PAD] = (k[:, base:base + HEAD_PAD] + kr_full).astype(BF16)

    vt_ref[0] = _nt_dot(wuvt_ref[...], ckvn).astype(BF16)


def _qkv(x, mod_l, ln_g, w_in_a, gq, gkv, wuqt, wuqrt, wuk, wuvt, tables):
    b, s, d = x.shape
    tm = TM_QKV
    cos128, sin128, cost, sint = tables
    const = lambda shape: pl.BlockSpec(shape, lambda i, j: (0,) * len(shape))
    scale = 1.0 / math.sqrt(QK_NOPE + QK_ROPE)
    return pl.pallas_call(
        functools.partial(_qkv_kernel, scale=scale),
        grid=(b, s // tm),
        in_specs=[
            pl.BlockSpec((1, tm, d), lambda i, j: (i, j, 0)),
            pl.BlockSpec((1, N_MOD, d), lambda i, j: (i, 0, 0)),
            const((1, d)),
            const(w_in_a.shape),
            const((1, Q_LORA)),
            const((1, KV_LORA)),
            const(wuqt.shape),
            const(wuqrt.shape),
            const(wuk.shape),
            const(wuvt.shape),
            pl.BlockSpec((1, tm, HEAD_PAD), lambda i, j: (i, j, 0)),
            pl.BlockSpec((1, tm, HEAD_PAD), lambda i, j: (i, j, 0)),
            pl.BlockSpec((1, QK_ROPE, tm), lambda i, j: (i, 0, j)),
            pl.BlockSpec((1, QK_ROPE, tm), lambda i, j: (i, 0, j)),
        ],
        out_specs=[
            pl.BlockSpec((1, N_HEADS * HEAD_PAD, tm), lambda i, j: (i, 0, j)),
            pl.BlockSpec((1, tm, N_HEADS * HEAD_PAD), lambda i, j: (i, j, 0)),
            pl.BlockSpec((1, ATTN_DIM, tm), lambda i, j: (i, 0, j)),
        ],
        out_shape=[
            jax.ShapeDtypeStruct((b, N_HEADS * HEAD_PAD, s), BF16),
            jax.ShapeDtypeStruct((b, s, N_HEADS * HEAD_PAD), BF16),
            jax.ShapeDtypeStruct((b, ATTN_DIM, s), BF16),
        ],
        compiler_params=pltpu.CompilerParams(
            dimension_semantics=("arbitrary", "arbitrary"),
            vmem_limit_bytes=VMEM_LIMIT_BYTES),
        name="qkv_proj",
    )(x, mod_l, ln_g, w_in_a, gq, gkv, wuqt, wuqrt, wuk, wuvt, cos128, sin128, cost, sint)


def _attn_kernel(qt_ref, k_ref, vt_ref, o_ref):
    qi = pl.program_id(2)
    key_pos = lax.broadcasted_iota(jnp.int32, (TK, TQ), 0)
    qry_pos = lax.broadcasted_iota(jnp.int32, (TK, TQ), 1)
    causal = qry_pos >= key_pos

    def scores(hd, off):
        kj = k_ref[0, pl.ds(off, TK), hd * HEAD_PAD:(hd + 1) * HEAD_PAD]
        qt = qt_ref[0, hd * HEAD_PAD:(hd + 1) * HEAD_PAD, :]
        return jnp.dot(kj, qt, preferred_element_type=F32)

    def values(hd, off):
        return vt_ref[0, hd * V_DIM:(hd + 1) * V_DIM, pl.ds(off, TK)]

    off_d = pl.multiple_of(qi * TQ, TQ)
    init = []
    for hd in range(ATTN_HEADS):
        s = jnp.where(causal, scores(hd, off_d), jnp.finfo(F32).min)
        m0 = jnp.max(s, axis=0, keepdims=True)
        p = jnp.exp(s - m0)
        l0 = jnp.sum(p, axis=0, keepdims=True)
        acc0 = jnp.dot(values(hd, off_d), p.astype(BF16), preferred_element_type=F32)
        init.append((m0, l0, acc0))

    def body(j, carry):
        off = pl.multiple_of(j * TK, TK)
        ss = [scores(hd, off) for hd in range(ATTN_HEADS)]
        stats = []
        for hd in range(ATTN_HEADS):
            m, l, acc = carry[hd]
            m_new = jnp.maximum(m, jnp.max(ss[hd], axis=0, keepdims=True))
            alpha = jnp.exp(m - m_new)
            p = jnp.exp(ss[hd] - m_new)
            l = alpha * l + jnp.sum(p, axis=0, keepdims=True)
            stats.append((m_new, l, alpha, p.astype(BF16)))
        out = []
        for hd in range(ATTN_HEADS):
            m_new, l, alpha, p = stats[hd]
            acc = alpha * carry[hd][2] + jnp.dot(values(hd, off), p, preferred_element_type=F32)
            out.append((m_new, l, acc))
        return tuple(out)

    final = lax.fori_loop(0, qi, body, tuple(init))
    for hd in range(ATTN_HEADS):
        _, l, acc = final[hd]
        o_ref[0, hd * V_DIM:(hd + 1) * V_DIM, :] = (acc * (1.0 / l)).astype(o_ref.dtype)


def _attention(qt, k, vt):
    b, _, s = qt.shape
    hb = ATTN_HEADS
    return pl.pallas_call(
        _attn_kernel,
        grid=(b, N_HEADS // hb, s // TQ),
        in_specs=[
            pl.BlockSpec((1, hb * HEAD_PAD, TQ), lambda i, h, q: (i, h, q)),
            pl.BlockSpec((1, s, hb * HEAD_PAD), lambda i, h, q: (i, 0, h)),
            pl.BlockSpec((1, hb * V_DIM, s), lambda i, h, q: (i, h, 0)),
        ],
        out_specs=pl.BlockSpec((1, hb * V_DIM, TQ), lambda i, h, q: (i, h, q)),
        out_shape=jax.ShapeDtypeStruct((b, ATTN_DIM, s), BF16),
        compiler_params=pltpu.CompilerParams(
            dimension_semantics=("arbitrary", "arbitrary", "arbitrary"),
            vmem_limit_bytes=VMEM_LIMIT_BYTES),
        name="mla_attention",
    )(qt, k, vt)


def _mix_kernel(x_ref, ot_ref, mod_ref, g1_ref, g2_ref, gf_ref, w_in_ref, w_pool_ref, pscale_ref,
                p_pool_ref, p_attn_ref, w_out_ref, w_ff1_ref, w_ff2_ref,
                o_ref, uext_ref, *, final):
    tm = x_ref.shape[1]
    si = pl.program_id(1)
    x = x_ref[0]
    shift1, scale1, gate1 = mod_ref[0, 0:1, :], mod_ref[0, 1:2, :], mod_ref[0, 2:3, :]
    shift2, scale2, gate2 = mod_ref[0, 3:4, :], mod_ref[0, 4:5, :], mod_ref[0, 5:6, :]

    h = (_rms(x) * g1_ref[...]) * (1.0 + scale1) + shift1
    z = jnp.dot(h.astype(BF16), w_in_ref[...], preferred_element_type=F32)
    u = z[:, 0:POOL_DIM]
    gz_a = z[:, POOL_DIM:POOL_DIM + D_MODEL]
    gz_b = z[:, POOL_DIM + D_MODEL:]

    @pl.when(si == 0)
    def _():
        uext_ref[0:POOL_HALO, :] = jnp.zeros((POOL_HALO, POOL_DIM), F32)

    @pl.when(si > 0)
    def _():
        uext_ref[0:POOL_HALO, :] = uext_ref[tm:tm + POOL_HALO, :]

    uext_ref[POOL_HALO:POOL_HALO + tm, :] = u
    t_pos = si * tm + lax.broadcasted_iota(jnp.int32, (tm, 1), 0)
    pooled = []
    for g, w in enumerate(POOL_WINDOWS):
        cols = pl.ds(g * POOL_GROUP_DIM, POOL_GROUP_DIM)
        ug = u[:, g * POOL_GROUP_DIM:(g + 1) * POOL_GROUP_DIM]
        acc = ug
        for kk in range(1, w):
            acc = acc + uext_ref[pl.ds(POOL_HALO - kk, tm), cols]
        inv_cnt = 1.0 / jnp.minimum(t_pos + 1, w).astype(F32)
        pg = acc * inv_cnt - ug
        yg = jnp.dot(pg.astype(BF16), w_pool_ref[g], preferred_element_type=F32)
        pooled.append(yg)
    y_pool = jnp.concatenate(pooled, axis=-1) * pscale_ref[...]
    y_a = jnp.dot(y_pool.astype(BF16), p_pool_ref[...], preferred_element_type=F32)

    y_b = _tn_dot(ot_ref[0], p_attn_ref[...])
    merged = _sigmoid(gz_a) * y_a + _sigmoid(gz_b) * y_b
    x1 = x + gate1 * jnp.dot(merged.astype(BF16), w_out_ref[...], preferred_element_type=F32)

    h2 = ((_rms(x1) * g2_ref[...]) * (1.0 + scale2) + shift2).astype(BF16)
    ff = jnp.zeros((tm, D_MODEL), F32)
    for c0 in range(0, D_FF, FF_CHUNK):
        t = jnp.dot(h2, w_ff1_ref[:, c0:c0 + FF_CHUNK], preferred_element_type=F32)
        t = jnp.square(jnp.maximum(t, 0.0)).astype(BF16)
        ff = ff + jnp.dot(t, w_ff2_ref[c0:c0 + FF_CHUNK, :], preferred_element_type=F32)
    x2 = x1 + gate2 * ff
    if final:
        x2 = _rms(x2) * gf_ref[...]
    o_ref[0] = x2


def _mix(x, ot, mod_l, g1, g2, gf, w_in_b, w_pool, pscale, p_pool, p_attn, w_out, w_ff1, w_ff2,
         *, final):
    b, s, d = x.shape
    tm = TM_MIX

    def const(shape):
        return pl.BlockSpec(shape, lambda i, j: (0,) * len(shape), pipeline_mode=pl.Buffered(1))

    return pl.pallas_call(
        functools.partial(_mix_kernel, final=final),
        grid=(b, s // tm),
        in_specs=[
            pl.BlockSpec((1, tm, d), lambda i, j: (i, j, 0)),
            pl.BlockSpec((1, ATTN_DIM, tm), lambda i, j: (i, 0, j)),
            pl.BlockSpec((1, N_MOD, d), lambda i, j: (i, 0, 0)),
            const((1, d)),
            const((1, d)),
            const((1, d)),
            const(w_in_b.shape),
            const(w_pool.shape),
            const((1, POOL_DIM)),
            const(p_pool.shape),
            const(p_attn.shape),
            const(w_out.shape),
            const(w_ff1.shape),
            const(w_ff2.shape),
        ],
        out_specs=pl.BlockSpec((1, tm, d), lambda i, j: (i, j, 0)),
        out_shape=jax.ShapeDtypeStruct((b, s, d), F32),
        scratch_shapes=[pltpu.VMEM((tm + POOL_HALO, POOL_DIM), F32)],
        compiler_params=pltpu.CompilerParams(
            dimension_semantics=("arbitrary", "arbitrary"),
            vmem_limit_bytes=VMEM_LIMIT_BYTES),
        name="mix_mlp",
    )(x, ot, mod_l, g1, g2, gf, w_in_b, w_pool, pscale, p_pool, p_attn, w_out, w_ff1, w_ff2)


def _layer_weights(w_in, w_uq, w_uk, w_uv):
    d = w_in.shape[0]
    c0 = POOL_DIM
    c1 = c0 + Q_LORA
    c2 = c1 + KV_LORA
    c3 = c2 + QK_ROPE
    half = QK_ROPE // 2
    w_kr = w_in[:, c2:c3]
    zl = jnp.zeros((d, ROPE_LO), F32)
    zr = jnp.zeros((d, HEAD_PAD - ROPE_HI), F32)
    kr_pad = jnp.concatenate([zl, w_kr, zr], axis=1)
    kr_rot_pad = jnp.concatenate([zl, -w_kr[:, half:], w_kr[:, :half], zr], axis=1)
    w_in_a = jnp.concatenate([w_in[:, c0:c2], kr_pad, kr_rot_pad], axis=1).astype(BF16)
    w_in_b = jnp.concatenate([w_in[:, :c0], w_in[:, c3:]], axis=1).astype(BF16)

    pad_q = HEAD_PAD - (QK_NOPE + QK_ROPE)
    wuq_pad = jnp.pad(w_uq, ((0, 0), (0, 0), (0, pad_q)))
    wuqt = wuq_pad.reshape(Q_LORA, N_HEADS * HEAD_PAD).T.astype(BF16)
    r = w_uq[:, :, QK_NOPE:]
    rot = jnp.concatenate([-r[..., half:], r[..., :half]], axis=-1)
    wuqrt = rot.reshape(Q_LORA, N_HEADS * QK_ROPE).T.astype(BF16)
    wuk = jnp.pad(w_uk, ((0, 0), (0, 0), (0, HEAD_PAD - QK_NOPE)))
    wuk = wuk.reshape(KV_LORA, N_HEADS * HEAD_PAD).astype(BF16)
    wuvt = w_uv.reshape(KV_LORA, ATTN_DIM).T.astype(BF16)
    return w_in_a, w_in_b, wuqt, wuqrt, wuk, wuvt


def kernel(x, c, positions, ln1_g, ln2_g, w_ada, b_ada, w_in, q_norm_g, w_uq, kv_norm_g, w_uk,
           w_uv, w_pool, pool_scale, p_pool, p_attn, w_out, w_ff1, w_ff2, final_g):
    depth = w_in.shape[0]
    tables = _rope_tables(positions)
    mod = _modulation(c, w_ada, b_ada)
    gf = final_g.reshape(1, D_MODEL)
    for l in range(depth):
        w_in_a, w_in_b, wuqt, wuqrt, wuk, wuvt = _layer_weights(w_in[l], w_uq[l], w_uk[l], w_uv[l])
        g1 = ln1_g[l].reshape(1, D_MODEL)
        g2 = ln2_g[l].reshape(1, D_MODEL)
        qt, k, vt = _qkv(x, mod[l], g1, w_in_a, q_norm_g[l].reshape(1, Q_LORA),
                         kv_norm_g[l].reshape(1, KV_LORA), wuqt, wuqrt, wuk, wuvt, tables)
        ot = _attention(qt, k, vt)
        x = _mix(x, ot, mod[l], g1, g2, gf, w_in_b, w_pool[l].astype(BF16),
                 pool_scale[l].reshape(1, POOL_DIM), p_pool[l].astype(BF16), p_attn[l].astype(BF16),
                 w_out[l].astype(BF16), w_ff1[l].astype(BF16), w_ff2[l].astype(BF16),
                 final=(l == depth - 1))
    return x
```

```python
import functools
import math

import jax
import jax.numpy as jnp
from jax import lax
from jax.experimental import pallas as pl
from jax.experimental.pallas import tpu as pltpu

D_MODEL = 1024
N_HEADS = 8
QK_NOPE = 64
QK_ROPE = 32
V_DIM = 64
Q_LORA = 384
KV_LORA = 256
POOL_WINDOWS = (2, 4, 8, 16)
POOL_GROUP_DIM = 128
POOL_DIM = len(POOL_WINDOWS) * POOL_GROUP_DIM
ATTN_DIM = N_HEADS * V_DIM
D_FF = 4 * D_MODEL
N_MOD = 6
EPS = 1e-6
ROPE_THETA = 10000.0

HEAD_PAD = 128
ROPE_LO = QK_NOPE
ROPE_HI = QK_NOPE + QK_ROPE
POOL_HALO = 16

VMEM_LIMIT_BYTES = 56 * 1024 * 1024

F32 = jnp.float32
BF16 = jnp.bfloat16

TM_QKV = 512
TM_MIX = 512
TQ = 512
TK = 512
ATTN_HEADS = 4
BF16_SUBLANES = 16
ACC_ROWS = V_DIM + BF16_SUBLANES
FF_CHUNK = 1024
MOD_TN = 1536


def _nt_dot(a, b):
    return lax.dot_general(a, b, (((1,), (1,)), ((), ())), preferred_element_type=F32)


def _tn_dot(a, b):
    return lax.dot_general(a, b, (((0,), (0,)), ((), ())), preferred_element_type=F32)


def _rms(x):
    return x * lax.rsqrt(jnp.mean(x * x, axis=-1, keepdims=True) + EPS)


def _sigmoid(x):
    return 1.0 / (1.0 + jnp.exp(-x))


def _rope_tables_kernel(pos_col_ref, pos_row_ref, invf_row_ref, invf_col_ref,
                        cos_ref, sin_ref, cost_ref, sint_ref):
    ang = pos_col_ref[0].astype(F32) * invf_row_ref[...]
    cos_ref[0] = jnp.cos(ang)
    sin_ref[0] = jnp.sin(ang)
    angt = invf_col_ref[...] * pos_row_ref[0].astype(F32)
    cost_ref[0] = jnp.cos(angt)
    sint_ref[0] = jnp.sin(angt)


def _rope_tables(positions):
    b, s = positions.shape
    inv_freq = ROPE_THETA ** (-jnp.arange(0, QK_ROPE, 2, dtype=F32) / QK_ROPE)
    two = jnp.concatenate([inv_freq, inv_freq])
    invf_row = jnp.zeros((1, HEAD_PAD), F32).at[0, ROPE_LO:ROPE_HI].set(two)
    invf_col = two.reshape(QK_ROPE, 1)
    return pl.pallas_call(
        _rope_tables_kernel,
        grid=(b,),
        in_specs=[
            pl.BlockSpec((1, s, 1), lambda i: (i, 0, 0)),
            pl.BlockSpec((1, 1, s), lambda i: (i, 0, 0)),
            pl.BlockSpec((1, HEAD_PAD), lambda i: (0, 0)),
            pl.BlockSpec((QK_ROPE, 1), lambda i: (0, 0)),
        ],
        out_specs=[
            pl.BlockSpec((1, s, HEAD_PAD), lambda i: (i, 0, 0)),
            pl.BlockSpec((1, s, HEAD_PAD), lambda i: (i, 0, 0)),
            pl.BlockSpec((1, QK_ROPE, s), lambda i: (i, 0, 0)),
            pl.BlockSpec((1, QK_ROPE, s), lambda i: (i, 0, 0)),
        ],
        out_shape=[
            jax.ShapeDtypeStruct((b, s, HEAD_PAD), F32),
            jax.ShapeDtypeStruct((b, s, HEAD_PAD), F32),
            jax.ShapeDtypeStruct((b, QK_ROPE, s), F32),
            jax.ShapeDtypeStruct((b, QK_ROPE, s), F32),
        ],
        compiler_params=pltpu.CompilerParams(vmem_limit_bytes=VMEM_LIMIT_BYTES),
        name="rope_tables",
    )(positions.reshape(b, s, 1), positions.reshape(b, 1, s), invf_row, invf_col)


def _mod_kernel(c_ref, w_ref, b_ref, o_ref):
    c = c_ref[...]
    c_act = c * _sigmoid(c)
    o_ref[0] = jnp.dot(c_act.astype(BF16), w_ref[0].astype(BF16),
                       preferred_element_type=F32) + b_ref[0]


def _modulation(c, w_ada, b_ada):
    depth, d, n = w_ada.shape
    b = c.shape[0]
    rows = 8
    c_pad = jnp.zeros((rows, d), F32).at[:b].set(c)
    out = pl.pallas_call(
        _mod_kernel,
        grid=(depth, n // MOD_TN),
        in_specs=[
            pl.BlockSpec((rows, d), lambda l, j: (0, 0)),
            pl.BlockSpec((1, d, MOD_TN), lambda l, j: (l, 0, j)),
            pl.BlockSpec((1, 1, MOD_TN), lambda l, j: (l, 0, j)),
        ],
        out_specs=pl.BlockSpec((1, rows, MOD_TN), lambda l, j: (l, 0, j)),
        out_shape=jax.ShapeDtypeStruct((depth, rows, n), F32),
        compiler_params=pltpu.CompilerParams(vmem_limit_bytes=VMEM_LIMIT_BYTES),
        name="adaln_mod",
    )(c_pad, w_ada, b_ada.reshape(depth, 1, n))
    return out[:, :b].reshape(depth, b, N_MOD, d)


def _qkv_kernel(x_ref, mod_ref, g_ref, w_in_ref, gq_ref, gkv_ref, wuqt_ref, wuqrt_ref,
                wuk_ref, wuvt_ref, cos_ref, sin_ref, cost_ref, sint_ref,
                qt_ref, k_ref, vt_ref, *, scale):
    x = x_ref[0]
    shift = mod_ref[0, 0:1, :]
    scl = mod_ref[0, 1:2, :]
    h = (_rms(x) * g_ref[...]) * (1.0 + scl) + shift
    z = jnp.dot(h.astype(BF16), w_in_ref[...], preferred_element_type=F32)
    c_q = z[:, 0:Q_LORA]
    c_kv = z[:, Q_LORA:Q_LORA + KV_LORA]
    kr = z[:, Q_LORA + KV_LORA:Q_LORA + KV_LORA + HEAD_PAD]
    kr_rot = z[:, Q_LORA + KV_LORA + HEAD_PAD:]
    cqn = (_rms(c_q) * gq_ref[...]).astype(BF16)
    ckvn = (_rms(c_kv) * gkv_ref[...]).astype(BF16)

    qt = _nt_dot(wuqt_ref[...], cqn)
    qt_rot = _nt_dot(wuqrt_ref[...], cqn)
    cost = cost_ref[0]
    sint = sint_ref[0]
    for hd in range(N_HEADS):
        base = hd * HEAD_PAD
        qt_ref[0, base:base + ROPE_LO, :] = (qt[base:base + ROPE_LO] * scale).astype(BF16)
        roped = (qt[base + ROPE_LO:base + ROPE_HI] * cost
                 + qt_rot[hd * QK_ROPE:(hd + 1) * QK_ROPE] * sint)
        qt_ref[0, base + ROPE_LO:base + ROPE_HI, :] = (roped * scale).astype(BF16)
        qt_ref[0, base + ROPE_HI:base + HEAD_PAD, :] = (
            qt[base + ROPE_HI:base + HEAD_PAD] * scale).astype(BF16)

    k = jnp.dot(ckvn, wuk_ref[...], preferred_element_type=F32)
    kr_full = kr * cos_ref[0] + kr_rot * sin_ref[0]
    for hd in range(N_HEADS):
        base = hd * HEAD_PAD
        k_ref[0, :, base:base + HEAD_PAD] = (k[:, base:base + HEAD_PAD] + kr_full).astype(BF16)

    vt_ref[0] = _nt_dot(wuvt_ref[...], ckvn).astype(BF16)


def _qkv(x, mod_l, ln_g, w_in_a, gq, gkv, wuqt, wuqrt, wuk, wuvt, tables):
    b, s, d = x.shape
    tm = TM_QKV
    cos128, sin128, cost, sint = tables
    const = lambda shape: pl.BlockSpec(shape, lambda i, j: (0,) * len(shape))
    scale = math.log2(math.e) / math.sqrt(QK_NOPE + QK_ROPE)
    return pl.pallas_call(
        functools.partial(_qkv_kernel, scale=scale),
        grid=(b, s // tm),
        in_specs=[
            pl.BlockSpec((1, tm, d), lambda i, j: (i, j, 0)),
            pl.BlockSpec((1, N_MOD, d), lambda i, j: (i, 0, 0)),
            const((1, d)),
            const(w_in_a.shape),
            const((1, Q_LORA)),
            const((1, KV_LORA)),
            const(wuqt.shape),
            const(wuqrt.shape),
            const(wuk.shape),
            const(wuvt.shape),
            pl.BlockSpec((1, tm, HEAD_PAD), lambda i, j: (i, j, 0)),
            pl.BlockSpec((1, tm, HEAD_PAD), lambda i, j: (i, j, 0)),
            pl.BlockSpec((1, QK_ROPE, tm), lambda i, j: (i, 0, j)),
            pl.BlockSpec((1, QK_ROPE, tm), lambda i, j: (i, 0, j)),
        ],
        out_specs=[
            pl.BlockSpec((1, N_HEADS * HEAD_PAD, tm), lambda i, j: (i, 0, j)),
            pl.BlockSpec((1, tm, N_HEADS * HEAD_PAD), lambda i, j: (i, j, 0)),
            pl.BlockSpec((1, ATTN_DIM, tm), lambda i, j: (i, 0, j)),
        ],
        out_shape=[
            jax.ShapeDtypeStruct((b, N_HEADS * HEAD_PAD, s), BF16),
            jax.ShapeDtypeStruct((b, s, N_HEADS * HEAD_PAD), BF16),
            jax.ShapeDtypeStruct((b, ATTN_DIM, s), BF16),
        ],
        compiler_params=pltpu.CompilerParams(
            dimension_semantics=("arbitrary", "arbitrary"),
            vmem_limit_bytes=VMEM_LIMIT_BYTES),
        name="qkv_proj",
    )(x, mod_l, ln_g, w_in_a, gq, gkv, wuqt, wuqrt, wuk, wuvt, cos128, sin128, cost, sint)


def _attn_kernel(qt_ref, k_ref, vt_ref, o_ref, s_ref, acc_ref):
    qi = pl.program_id(2)
    ones_rows = jnp.ones((ACC_ROWS - V_DIM, TK), BF16)
    neg = jnp.finfo(F32).min

    def produce_scores(hd, off):
        kj = k_ref[0, pl.ds(off, TK), hd * HEAD_PAD:(hd + 1) * HEAD_PAD]
        qt = qt_ref[0, hd * HEAD_PAD:(hd + 1) * HEAD_PAD, :]
        s = jnp.dot(kj, qt, preferred_element_type=F32)
        s_ref[hd] = s
        return jnp.max(s, axis=0, keepdims=True)

    def consume(hd, off, m, tile_max, s):
        m_new = jnp.maximum(m, tile_max)
        alpha = jnp.exp2(m - m_new)
        p = jnp.exp2(s - m_new).astype(BF16)
        v_ext = jnp.concatenate([vt_ref[0, hd * V_DIM:(hd + 1) * V_DIM, pl.ds(off, TK)], ones_rows],
                                axis=0)
        acc_ref[hd] = alpha * acc_ref[hd] + jnp.dot(v_ext, p, preferred_element_type=F32)
        return m_new

    acc_ref[...] = jnp.zeros(acc_ref.shape, F32)
    tile_max0 = tuple(produce_scores(hd, 0) for hd in range(ATTN_HEADS))
    m0 = tuple(jnp.full((1, TQ), neg, F32) for _ in range(ATTN_HEADS))

    def body(j, carry):
        ms, tile_maxes = carry
        off = pl.multiple_of(j * TK, TK)
        new_ms, new_maxes = [], []
        for hd in range(ATTN_HEADS):
            new_ms.append(consume(hd, off, ms[hd], tile_maxes[hd], s_ref[hd]))
            new_maxes.append(produce_scores(hd, off + TK))
        return tuple(new_ms), tuple(new_maxes)

    ms, _ = lax.fori_loop(0, qi, body, (m0, tile_max0))

    off_d = pl.multiple_of(qi * TQ, TQ)
    key_pos = lax.broadcasted_iota(jnp.int32, (TK, TQ), 0)
    qry_pos = lax.broadcasted_iota(jnp.int32, (TK, TQ), 1)
    causal = qry_pos >= key_pos
    for hd in range(ATTN_HEADS):
        s = jnp.where(causal, s_ref[hd], neg)
        consume(hd, off_d, ms[hd], jnp.max(s, axis=0, keepdims=True), s)
        acc = acc_ref[hd]
        o_ref[0, hd * V_DIM:(hd + 1) * V_DIM, :] = (
            acc[0:V_DIM] * (1.0 / acc[V_DIM:V_DIM + 1])).astype(o_ref.dtype)


def _attention(qt, k, vt):
    b, _, s = qt.shape
    hb = ATTN_HEADS
    return pl.pallas_call(
        _attn_kernel,
        grid=(b, N_HEADS // hb, s // TQ),
        in_specs=[
            pl.BlockSpec((1, hb * HEAD_PAD, TQ), lambda i, h, q: (i, h, q)),
            pl.BlockSpec((1, s, hb * HEAD_PAD), lambda i, h, q: (i, 0, h)),
            pl.BlockSpec((1, hb * V_DIM, s), lambda i, h, q: (i, h, 0)),
        ],
        out_specs=pl.BlockSpec((1, hb * V_DIM, TQ), lambda i, h, q: (i, h, q)),
        out_shape=jax.ShapeDtypeStruct((b, ATTN_DIM, s), BF16),
        scratch_shapes=[pltpu.VMEM((hb, TK, TQ), F32), pltpu.VMEM((hb, ACC_ROWS, TQ), F32)],
        compiler_params=pltpu.CompilerParams(
            dimension_semantics=("arbitrary", "arbitrary", "arbitrary"),
            vmem_limit_bytes=VMEM_LIMIT_BYTES),
        name="mla_attention",
    )(qt, k, vt)


def _mix_kernel(x_ref, ot_ref, mod_ref, g1_ref, g2_ref, gf_ref, w_in_ref, w_pool_ref, pscale_ref,
                p_pool_ref, p_attn_ref, w_out_ref, w_ff1_ref, w_ff2_ref,
                o_ref, uext_ref, *, final):
    tm = x_ref.shape[1]
    si = pl.program_id(1)
    x = x_ref[0]
    shift1, scale1, gate1 = mod_ref[0, 0:1, :], mod_ref[0, 1:2, :], mod_ref[0, 2:3, :]
    shift2, scale2, gate2 = mod_ref[0, 3:4, :], mod_ref[0, 4:5, :], mod_ref[0, 5:6, :]

    h = (_rms(x) * g1_ref[...]) * (1.0 + scale1) + shift1
    z = jnp.dot(h.astype(BF16), w_in_ref[...], preferred_element_type=F32)
    u = z[:, 0:POOL_DIM]
    gz_a = z[:, POOL_DIM:POOL_DIM + D_MODEL]
    gz_b = z[:, POOL_DIM + D_MODEL:]

    @pl.when(si == 0)
    def _():
        uext_ref[0:POOL_HALO, :] = jnp.zeros((POOL_HALO, POOL_DIM), F32)

    @pl.when(si > 0)
    def _():
        uext_ref[0:POOL_HALO, :] = uext_ref[tm:tm + POOL_HALO, :]

    uext_ref[POOL_HALO:POOL_HALO + tm, :] = u
    t_pos = si * tm + lax.broadcasted_iota(jnp.int32, (tm, 1), 0)
    pooled = []
    for g, w in enumerate(POOL_WINDOWS):
        cols = pl.ds(g * POOL_GROUP_DIM, POOL_GROUP_DIM)
        ug = u[:, g * POOL_GROUP_DIM:(g + 1) * POOL_GROUP_DIM]
        acc = ug
        for kk in range(1, w):
            acc = acc + uext_ref[pl.ds(POOL_HALO - kk, tm), cols]
        inv_cnt = 1.0 / jnp.minimum(t_pos + 1, w).astype(F32)
        pg = acc * inv_cnt - ug
        yg = jnp.dot(pg.astype(BF16), w_pool_ref[g], preferred_element_type=F32)
        pooled.append(yg)
    y_pool = jnp.concatenate(pooled, axis=-1) * pscale_ref[...]
    y_a = jnp.dot(y_pool.astype(BF16), p_pool_ref[...], preferred_element_type=F32)

    y_b = _tn_dot(ot_ref[0], p_attn_ref[...])
    merged = _sigmoid(gz_a) * y_a + _sigmoid(gz_b) * y_b
    x1 = x + gate1 * jnp.dot(merged.astype(BF16), w_out_ref[...], preferred_element_type=F32)

    h2 = ((_rms(x1) * g2_ref[...]) * (1.0 + scale2) + shift2).astype(BF16)
    ff = jnp.zeros((tm, D_MODEL), F32)
    for c0 in range(0, D_FF, FF_CHUNK):
        t = jnp.dot(h2, w_ff1_ref[:, c0:c0 + FF_CHUNK], preferred_element_type=F32)
        t = jnp.square(jnp.maximum(t, 0.0)).astype(BF16)
        ff = ff + jnp.dot(t, w_ff2_ref[c0:c0 + FF_CHUNK, :], preferred_element_type=F32)
    x2 = x1 + gate2 * ff
    if final:
        x2 = _rms(x2) * gf_ref[...]
    o_ref[0] = x2


def _mix(x, ot, mod_l, g1, g2, gf, w_in_b, w_pool, pscale, p_pool, p_attn, w_out, w_ff1, w_ff2,
         *, final):
    b, s, d = x.shape
    tm = TM_MIX

    def const(shape):
        return pl.BlockSpec(shape, lambda i, j: (0,) * len(shape), pipeline_mode=pl.Buffered(1))

    return pl.pallas_call(
        functools.partial(_mix_kernel, final=final),
        grid=(b, s // tm),
        in_specs=[
            pl.BlockSpec((1, tm, d), lambda i, j: (i, j, 0)),
            pl.BlockSpec((1, ATTN_DIM, tm), lambda i, j: (i, 0, j)),
            pl.BlockSpec((1, N_MOD, d), lambda i, j: (i, 0, 0)),
            const((1, d)),
            const((1, d)),
            const((1, d)),
            const(w_in_b.shape),
            const(w_pool.shape),
            const((1, POOL_DIM)),
            const(p_pool.shape),
            const(p_attn.shape),
            const(w_out.shape),
            const(w_ff1.shape),
            const(w_ff2.shape),
        ],
        out_specs=pl.BlockSpec((1, tm, d), lambda i, j: (i, j, 0)),
        out_shape=jax.ShapeDtypeStruct((b, s, d), F32),
        scratch_shapes=[pltpu.VMEM((tm + POOL_HALO, POOL_DIM), F32)],
        compiler_params=pltpu.CompilerParams(
            dimension_semantics=("arbitrary", "arbitrary"),
            vmem_limit_bytes=VMEM_LIMIT_BYTES),
        name="mix_mlp",
    )(x, ot, mod_l, g1, g2, gf, w_in_b, w_pool, pscale, p_pool, p_attn, w_out, w_ff1, w_ff2)


def _layer_weights(w_in, w_uq, w_uk, w_uv):
    d = w_in.shape[0]
    c0 = POOL_DIM
    c1 = c0 + Q_LORA
    c2 = c1 + KV_LORA
    c3 = c2 + QK_ROPE
    half = QK_ROPE // 2
    w_kr = w_in[:, c2:c3]
    zl = jnp.zeros((d, ROPE_LO), F32)
    zr = jnp.zeros((d, HEAD_PAD - ROPE_HI), F32)
    kr_pad = jnp.concatenate([zl, w_kr, zr], axis=1)
    kr_rot_pad = jnp.concatenate([zl, -w_kr[:, half:], w_kr[:, :half], zr], axis=1)
    w_in_a = jnp.concatenate([w_in[:, c0:c2], kr_pad, kr_rot_pad], axis=1).astype(BF16)
    w_in_b = jnp.concatenate([w_in[:, :c0], w_in[:, c3:]], axis=1).astype(BF16)

    pad_q = HEAD_PAD - (QK_NOPE + QK_ROPE)
    wuq_pad = jnp.pad(w_uq, ((0, 0), (0, 0), (0, pad_q)))
    wuqt = wuq_pad.reshape(Q_LORA, N_HEADS * HEAD_PAD).T.astype(BF16)
    r = w_uq[:, :, QK_NOPE:]
    rot = jnp.concatenate([-r[..., half:], r[..., :half]], axis=-1)
    wuqrt = rot.reshape(Q_LORA, N_HEADS * QK_ROPE).T.astype(BF16)
    wuk = jnp.pad(w_uk, ((0, 0), (0, 0), (0, HEAD_PAD - QK_NOPE)))
    wuk = wuk.reshape(KV_LORA, N_HEADS * HEAD_PAD).astype(BF16)
    wuvt = w_uv.reshape(KV_LORA, ATTN_DIM).T.astype(BF16)
    return w_in_a, w_in_b, wuqt, wuqrt, wuk, wuvt


def kernel(x, c, positions, ln1_g, ln2_g, w_ada, b_ada, w_in, q_norm_g, w_uq, kv_norm_g, w_uk,
           w_uv, w_pool, pool_scale, p_pool, p_attn, w_out, w_ff1, w_ff2, final_g):
    depth = w_in.shape[0]
    tables = _rope_tables(positions)
    mod = _modulation(c, w_ada, b_ada)
    gf = final_g.reshape(1, D_MODEL)
    for l in range(depth):
        w_in_a, w_in_b, wuqt, wuqrt, wuk, wuvt = _layer_weights(w_in[l], w_uq[l], w_uk[l], w_uv[l])
        g1 = ln1_g[l].reshape(1, D_MODEL)
        g2 = ln2_g[l].reshape(1, D_MODEL)
        qt, k, vt = _qkv(x, mod[l], g1, w_in_a, q_norm_g[l].reshape(1, Q_LORA),
                         kv_norm_g[l].reshape(1, KV_LORA), wuqt, wuqrt, wuk, wuvt, tables)
        ot = _attention(qt, k, vt)
        x = _mix(x, ot, mod[l], g1, g2, gf, w_in_b, w_pool[l].astype(BF16),
                 pool_scale[l].reshape(1, POOL_DIM), p_pool[l].astype(BF16), p_attn[l].astype(BF16),
                 w_out[l].astype(BF16), w_ff1[l].astype(BF16), w_ff2[l].astype(BF16),
                 final=(l == depth - 1))
    return x
```

```python
import functools
import math

import jax
import jax.numpy as jnp
from jax import lax
from jax.experimental import pallas as pl
from jax.experimental.pallas import tpu as pltpu

D_MODEL = 1024
N_HEADS = 8
QK_NOPE = 64
QK_ROPE = 32
V_DIM = 64
Q_LORA = 384
KV_LORA = 256
POOL_WINDOWS = (2, 4, 8, 16)
POOL_GROUP_DIM = 128
POOL_DIM = len(POOL_WINDOWS) * POOL_GROUP_DIM
ATTN_DIM = N_HEADS * V_DIM
D_FF = 4 * D_MODEL
N_MOD = 6
EPS = 1e-6
ROPE_THETA = 10000.0

HEAD_PAD = 128
ROPE_LO = QK_NOPE
ROPE_HI = QK_NOPE + QK_ROPE
POOL_HALO = 16

VMEM_LIMIT_BYTES = 56 * 1024 * 1024

F32 = jnp.float32
BF16 = jnp.bfloat16

TM_QKV = 512
TM_MIX = 512
TQ = 512
TK = 512
ATTN_HEADS = 4
BF16_SUBLANES = 16
ACC_ROWS = V_DIM + BF16_SUBLANES
FF_CHUNK = 1024
MOD_TN = 1536


def _nt_dot(a, b):
    return lax.dot_general(a, b, (((1,), (1,)), ((), ())), preferred_element_type=F32)


def _tn_dot(a, b):
    return lax.dot_general(a, b, (((0,), (0,)), ((), ())), preferred_element_type=F32)


def _rms(x):
    return x * lax.rsqrt(jnp.mean(x * x, axis=-1, keepdims=True) + EPS)


def _sigmoid(x):
    return 1.0 / (1.0 + jnp.exp(-x))


def _rope_tables_kernel(pos_ref, invf_ref, cost_ref, sint_ref):
    ang = invf_ref[...] * pos_ref[0].astype(F32)
    cost_ref[0] = jnp.cos(ang)
    sint_ref[0] = jnp.sin(ang)


def _rope_tables(positions):
    b, s = positions.shape
    inv_freq = ROPE_THETA ** (-jnp.arange(0, QK_ROPE, 2, dtype=F32) / QK_ROPE)
    invf_col = jnp.concatenate([inv_freq, inv_freq]).reshape(QK_ROPE, 1)
    return pl.pallas_call(
        _rope_tables_kernel,
        grid=(b,),
        in_specs=[
            pl.BlockSpec((1, 1, s), lambda i: (i, 0, 0)),
            pl.BlockSpec((QK_ROPE, 1), lambda i: (0, 0)),
        ],
        out_specs=[
            pl.BlockSpec((1, QK_ROPE, s), lambda i: (i, 0, 0)),
            pl.BlockSpec((1, QK_ROPE, s), lambda i: (i, 0, 0)),
        ],
        out_shape=[
            jax.ShapeDtypeStruct((b, QK_ROPE, s), F32),
            jax.ShapeDtypeStruct((b, QK_ROPE, s), F32),
        ],
        compiler_params=pltpu.CompilerParams(vmem_limit_bytes=VMEM_LIMIT_BYTES),
        name="rope_tables",
    )(positions.reshape(b, 1, s), invf_col)


def _mod_kernel(c_ref, w_ref, b_ref, o_ref):
    c = c_ref[...]
    c_act = c * _sigmoid(c)
    o_ref[0] = jnp.dot(c_act.astype(BF16), w_ref[0].astype(BF16),
                       preferred_element_type=F32) + b_ref[0]


def _modulation(c, w_ada, b_ada):
    depth, d, n = w_ada.shape
    b = c.shape[0]
    rows = 8
    c_pad = jnp.zeros((rows, d), F32).at[:b].set(c)
    out = pl.pallas_call(
        _mod_kernel,
        grid=(depth, n // MOD_TN),
        in_specs=[
            pl.BlockSpec((rows, d), lambda l, j: (0, 0)),
            pl.BlockSpec((1, d, MOD_TN), lambda l, j: (l, 0, j)),
            pl.BlockSpec((1, 1, MOD_TN), lambda l, j: (l, 0, j)),
        ],
        out_specs=pl.BlockSpec((1, rows, MOD_TN), lambda l, j: (l, 0, j)),
        out_shape=jax.ShapeDtypeStruct((depth, rows, n), F32),
        compiler_params=pltpu.CompilerParams(vmem_limit_bytes=VMEM_LIMIT_BYTES),
        name="adaln_mod",
    )(c_pad, w_ada, b_ada.reshape(depth, 1, n))
    return out[:, :b].reshape(depth, b, N_MOD, d)


def _qkv_kernel(x_ref, mod_ref, g_ref, w_in_ref, gq_ref, gkv_ref, wuqt_ref, wuqrt_ref,
                wuk_ref, wuvt_ref, cost_ref, sint_ref,
                qt_ref, k_ref, vt_ref, *, scale):
    x = x_ref[0]
    shift = mod_ref[0, 0:1, :]
    scl = mod_ref[0, 1:2, :]
    h = (_rms(x) * g_ref[...]) * (1.0 + scl) + shift
    z = jnp.dot(h.astype(BF16), w_in_ref[...], preferred_element_type=F32)
    c_q = z[:, 0:Q_LORA]
    c_kv = z[:, Q_LORA:Q_LORA + KV_LORA]
    kr = z[:, Q_LORA + KV_LORA:Q_LORA + KV_LORA + HEAD_PAD]
    kr_rot = z[:, Q_LORA + KV_LORA + HEAD_PAD:]
    cqn = (_rms(c_q) * gq_ref[...]).astype(BF16)
    ckvn = (_rms(c_kv) * gkv_ref[...]).astype(BF16)

    qt = _nt_dot(wuqt_ref[...], cqn)
    qt_rot = _nt_dot(wuqrt_ref[...], cqn)
    cost = cost_ref[0]
    sint = sint_ref[0]
    for hd in range(N_HEADS):
        base = hd * HEAD_PAD
        qt_ref[0, base:base + ROPE_LO, :] = (qt[base:base + ROPE_LO] * scale).astype(BF16)
        roped = (qt[base + ROPE_LO:base + ROPE_HI] * cost
                 + qt_rot[hd * QK_ROPE:(hd + 1) * QK_ROPE] * sint)
        qt_ref[0, base + ROPE_LO:base + ROPE_HI, :] = (roped * scale).astype(BF16)
        qt_ref[0, base + ROPE_HI:base + HEAD_PAD, :] = (
            qt[base + ROPE_HI:base + HEAD_PAD] * scale).astype(BF16)

    k = jnp.dot(ckvn, wuk_ref[...], preferred_element_type=F32)
    tm = x.shape[0]
    z_lo = jnp.zeros((ROPE_LO, tm), F32)
    z_hi = jnp.zeros((HEAD_PAD - ROPE_HI, tm), F32)
    cos_tok = jnp.concatenate([z_lo, cost, z_hi], axis=0).T
    sin_tok = jnp.concatenate([z_lo, sint, z_hi], axis=0).T
    kr_full = kr * cos_tok + kr_rot * sin_tok
    for hd in range(N_HEADS):
        base = hd * HEAD_PAD
        k_ref[0, :, base:base + HEAD_PAD] = (k[:, base:base + HEAD_PAD] + kr_full).astype(BF16)

    vt_ref[0] = _nt_dot(wuvt_ref[...], ckvn).astype(BF16)


def _qkv(x, mod_l, ln_g, w_in_a, gq, gkv, wuqt, wuqrt, wuk, wuvt, tables):
    b, s, d = x.shape
    tm = TM_QKV
    cost, sint = tables
    const = lambda shape: pl.BlockSpec(shape, lambda i, j: (0,) * len(shape))
    scale = math.log2(math.e) / math.sqrt(QK_NOPE + QK_ROPE)
    return pl.pallas_call(
        functools.partial(_qkv_kernel, scale=scale),
        grid=(b, s // tm),
        in_specs=[
            pl.BlockSpec((1, tm, d), lambda i, j: (i, j, 0)),
            pl.BlockSpec((1, N_MOD, d), lambda i, j: (i, 0, 0)),
            const((1, d)),
            const(w_in_a.shape),
            const((1, Q_LORA)),
            const((1, KV_LORA)),
            const(wuqt.shape),
            const(wuqrt.shape),
            const(wuk.shape),
            const(wuvt.shape),
            pl.BlockSpec((1, QK_ROPE, tm), lambda i, j: (i, 0, j)),
            pl.BlockSpec((1, QK_ROPE, tm), lambda i, j: (i, 0, j)),
        ],
        out_specs=[
            pl.BlockSpec((1, N_HEADS * HEAD_PAD, tm), lambda i, j: (i, 0, j)),
            pl.BlockSpec((1, tm, N_HEADS * HEAD_PAD), lambda i, j: (i, j, 0)),
            pl.BlockSpec((1, ATTN_DIM, tm), lambda i, j: (i, 0, j)),
        ],
        out_shape=[
            jax.ShapeDtypeStruct((b, N_HEADS * HEAD_PAD, s), BF16),
            jax.ShapeDtypeStruct((b, s, N_HEADS * HEAD_PAD), BF16),
            jax.ShapeDtypeStruct((b, ATTN_DIM, s), BF16),
        ],
        compiler_params=pltpu.CompilerParams(
            dimension_semantics=("arbitrary", "arbitrary"),
            vmem_limit_bytes=VMEM_LIMIT_BYTES),
        name="qkv_proj",
    )(x, mod_l, ln_g, w_in_a, gq, gkv, wuqt, wuqrt, wuk, wuvt, cost, sint)


def _attn_kernel(qt_ref, k_ref, vt_ref, o_ref, s_ref, acc_ref):
    qi = pl.program_id(2)
    ones_rows = jnp.ones((ACC_ROWS - V_DIM, TK), BF16)
    neg = jnp.finfo(F32).min

    def produce_scores(hd, off):
        kj = k_ref[0, pl.ds(off, TK), hd * HEAD_PAD:(hd + 1) * HEAD_PAD]
        qt = qt_ref[0, hd * HEAD_PAD:(hd + 1) * HEAD_PAD, :]
        s = jnp.dot(kj, qt, preferred_element_type=F32)
        s_ref[hd] = s
        return jnp.max(s, axis=0, keepdims=True)

    def consume(hd, off, m, tile_max, s):
        m_new = jnp.maximum(m, tile_max)
        alpha = jnp.exp2(m - m_new)
        p = jnp.exp2(s - m_new).astype(BF16)
        v_ext = jnp.concatenate([vt_ref[0, hd * V_DIM:(hd + 1) * V_DIM, pl.ds(off, TK)], ones_rows],
                                axis=0)
        acc_ref[hd] = alpha * acc_ref[hd] + jnp.dot(v_ext, p, preferred_element_type=F32)
        return m_new

    acc_ref[...] = jnp.zeros(acc_ref.shape, F32)
    tile_max0 = tuple(produce_scores(hd, 0) for hd in range(ATTN_HEADS))
    m0 = tuple(jnp.full((1, TQ), neg, F32) for _ in range(ATTN_HEADS))

    def body(j, carry):
        ms, tile_maxes = carry
        off = pl.multiple_of(j * TK, TK)
        new_ms, new_maxes = [], []
        for hd in range(ATTN_HEADS):
            new_ms.append(consume(hd, off, ms[hd], tile_maxes[hd], s_ref[hd]))
            new_maxes.append(produce_scores(hd, off + TK))
        return tuple(new_ms), tuple(new_maxes)

    ms, _ = lax.fori_loop(0, qi, body, (m0, tile_max0))

    off_d = pl.multiple_of(qi * TQ, TQ)
    key_pos = lax.broadcasted_iota(jnp.int32, (TK, TQ), 0)
    qry_pos = lax.broadcasted_iota(jnp.int32, (TK, TQ), 1)
    causal = qry_pos >= key_pos
    for hd in range(ATTN_HEADS):
        s = jnp.where(causal, s_ref[hd], neg)
        consume(hd, off_d, ms[hd], jnp.max(s, axis=0, keepdims=True), s)
        acc = acc_ref[hd]
        o_ref[0, hd * V_DIM:(hd + 1) * V_DIM, :] = (
            acc[0:V_DIM] * (1.0 / acc[V_DIM:V_DIM + 1])).astype(o_ref.dtype)


def _attention(qt, k, vt):
    b, _, s = qt.shape
    hb = ATTN_HEADS
    return pl.pallas_call(
        _attn_kernel,
        grid=(b, N_HEADS // hb, s // TQ),
        in_specs=[
            pl.BlockSpec((1, hb * HEAD_PAD, TQ), lambda i, h, q: (i, h, q)),
            pl.BlockSpec((1, s, hb * HEAD_PAD), lambda i, h, q: (i, 0, h)),
            pl.BlockSpec((1, hb * V_DIM, s), lambda i, h, q: (i, h, 0)),
        ],
        out_specs=pl.BlockSpec((1, hb * V_DIM, TQ), lambda i, h, q: (i, h, q)),
        out_shape=jax.ShapeDtypeStruct((b, ATTN_DIM, s), BF16),
        scratch_shapes=[pltpu.VMEM((hb, TK, TQ), F32), pltpu.VMEM((hb, ACC_ROWS, TQ), F32)],
        compiler_params=pltpu.CompilerParams(
            dimension_semantics=("arbitrary", "arbitrary", "arbitrary"),
            vmem_limit_bytes=VMEM_LIMIT_BYTES),
        name="mla_attention",
    )(qt, k, vt)


def _mix_kernel(x_ref, ot_ref, mod_ref, g1_ref, g2_ref, gf_ref, w_in_ref, w_pool_ref, pscale_ref,
                p_pool_ref, p_attn_ref, w_out_ref, w_ff1_ref, w_ff2_ref,
                o_ref, uext_ref, *, final):
    tm = x_ref.shape[1]
    si = pl.program_id(1)

    @pl.when(si == 0)
    def _():
        uext_ref[0:POOL_HALO, :] = jnp.zeros((POOL_HALO, POOL_DIM), F32)

    @pl.when(si > 0)
    def _():
        uext_ref[0:POOL_HALO, :] = uext_ref[tm:tm + POOL_HALO, :]

    x = x_ref[0]
    shift1, scale1, gate1 = mod_ref[0, 0:1, :], mod_ref[0, 1:2, :], mod_ref[0, 2:3, :]
    shift2, scale2, gate2 = mod_ref[0, 3:4, :], mod_ref[0, 4:5, :], mod_ref[0, 5:6, :]

    y_b = _tn_dot(ot_ref[0], p_attn_ref[...])
    h = ((_rms(x) * g1_ref[...]) * (1.0 + scale1) + shift1).astype(BF16)
    u = jnp.dot(h, w_in_ref[:, 0:POOL_DIM], preferred_element_type=F32)
    gz = jnp.dot(h, w_in_ref[:, POOL_DIM:], preferred_element_type=F32)
    gz_a = gz[:, 0:D_MODEL]
    gz_b = gz[:, D_MODEL:]

    uext_ref[POOL_HALO:POOL_HALO + tm, :] = u
    head_pos = si * tm + lax.broadcasted_iota(jnp.int32, (POOL_HALO, 1), 0)
    pooled = []
    for g, w in enumerate(POOL_WINDOWS):
        eg = uext_ref[:, g * POOL_GROUP_DIM:(g + 1) * POOL_GROUP_DIM]
        win = eg
        k = 1
        while k < w:
            win = win + pltpu.roll(win, k, axis=0)
            k *= 2
        win = win[POOL_HALO:]
        ug = eg[POOL_HALO:]
        inv_head = 1.0 / jnp.minimum(head_pos + 1, w).astype(F32)
        mean = jnp.concatenate([win[:POOL_HALO] * inv_head, win[POOL_HALO:] * (1.0 / w)], axis=0)
        yg = jnp.dot((mean - ug).astype(BF16), w_pool_ref[g], preferred_element_type=F32)
        pooled.append(yg)
    y_pool = jnp.concatenate(pooled, axis=-1) * pscale_ref[...]
    y_a = jnp.dot(y_pool.astype(BF16), p_pool_ref[...], preferred_element_type=F32)

    merged = _sigmoid(gz_a) * y_a + _sigmoid(gz_b) * y_b
    x1 = x + gate1 * jnp.dot(merged.astype(BF16), w_out_ref[...], preferred_element_type=F32)

    h2 = ((_rms(x1) * g2_ref[...]) * (1.0 + scale2) + shift2).astype(BF16)
    ff = jnp.zeros((tm, D_MODEL), F32)
    for c0 in range(0, D_FF, FF_CHUNK):
        t = jnp.dot(h2, w_ff1_ref[:, c0:c0 + FF_CHUNK], preferred_element_type=F32)
        t = jnp.square(jnp.maximum(t, 0.0)).astype(BF16)
        ff = ff + jnp.dot(t, w_ff2_ref[c0:c0 + FF_CHUNK, :], preferred_element_type=F32)
    x2 = x1 + gate2 * ff
    if final:
        x2 = _rms(x2) * gf_ref[...]
    o_ref[0] = x2


def _mix(x, ot, mod_l, g1, g2, gf, w_in_b, w_pool, pscale, p_pool, p_attn, w_out, w_ff1, w_ff2,
         *, final):
    b, s, d = x.shape
    tm = TM_MIX

    def const(shape):
        return pl.BlockSpec(shape, lambda i, j: (0,) * len(shape), pipeline_mode=pl.Buffered(1))

    return pl.pallas_call(
        functools.partial(_mix_kernel, final=final),
        grid=(b, s // tm),
        in_specs=[
            pl.BlockSpec((1, tm, d), lambda i, j: (i, j, 0)),
            pl.BlockSpec((1, ATTN_DIM, tm), lambda i, j: (i, 0, j)),
            pl.BlockSpec((1, N_MOD, d), lambda i, j: (i, 0, 0)),
            const((1, d)),
            const((1, d)),
            const((1, d)),
            const(w_in_b.shape),
            const(w_pool.shape),
            const((1, POOL_DIM)),
            const(p_pool.shape),
            const(p_attn.shape),
            const(w_out.shape),
            const(w_ff1.shape),
            const(w_ff2.shape),
        ],
        out_specs=pl.BlockSpec((1, tm, d), lambda i, j: (i, j, 0)),
        out_shape=jax.ShapeDtypeStruct((b, s, d), F32),
        scratch_shapes=[pltpu.VMEM((tm + POOL_HALO, POOL_DIM), F32)],
        compiler_params=pltpu.CompilerParams(
            dimension_semantics=("arbitrary", "arbitrary"),
            vmem_limit_bytes=VMEM_LIMIT_BYTES),
        name="mix_mlp",
    )(x, ot, mod_l, g1, g2, gf, w_in_b, w_pool, pscale, p_pool, p_attn, w_out, w_ff1, w_ff2)


def _layer_weights(w_in, w_uq, w_uk, w_uv):
    d = w_in.shape[0]
    c0 = POOL_DIM
    c1 = c0 + Q_LORA
    c2 = c1 + KV_LORA
    c3 = c2 + QK_ROPE
    half = QK_ROPE // 2
    w_kr = w_in[:, c2:c3]
    zl = jnp.zeros((d, ROPE_LO), F32)
    zr = jnp.zeros((d, HEAD_PAD - ROPE_HI), F32)
    kr_pad = jnp.concatenate([zl, w_kr, zr], axis=1)
    kr_rot_pad = jnp.concatenate([zl, -w_kr[:, half:], w_kr[:, :half], zr], axis=1)
    w_in_a = jnp.concatenate([w_in[:, c0:c2], kr_pad, kr_rot_pad], axis=1).astype(BF16)
    w_in_b = jnp.concatenate([w_in[:, :c0], w_in[:, c3:]], axis=1).astype(BF16)

    pad_q = HEAD_PAD - (QK_NOPE + QK_ROPE)
    wuq_pad = jnp.pad(w_uq, ((0, 0), (0, 0), (0, pad_q)))
    wuqt = wuq_pad.reshape(Q_LORA, N_HEADS * HEAD_PAD).T.astype(BF16)
    r = w_uq[:, :, QK_NOPE:]
    rot = jnp.concatenate([-r[..., half:], r[..., :half]], axis=-1)
    wuqrt = rot.reshape(Q_LORA, N_HEADS * QK_ROPE).T.astype(BF16)
    wuk = jnp.pad(w_uk, ((0, 0), (0, 0), (0, HEAD_PAD - QK_NOPE)))
    wuk = wuk.reshape(KV_LORA, N_HEADS * HEAD_PAD).astype(BF16)
    wuvt = w_uv.reshape(KV_LORA, ATTN_DIM).T.astype(BF16)
    return w_in_a, w_in_b, wuqt, wuqrt, wuk, wuvt


def kernel(x, c, positions, ln1_g, ln2_g, w_ada, b_ada, w_in, q_norm_g, w_uq, kv_norm_g, w_uk,
           w_uv, w_pool, pool_scale, p_pool, p_attn, w_out, w_ff1, w_ff2, final_g):
    depth = w_in.shape[0]
    tables = _rope_tables(positions)
    mod = _modulation(c, w_ada, b_ada)
    gf = final_g.reshape(1, D_MODEL)
    for l in range(depth):
        w_in_a, w_in_b, wuqt, wuqrt, wuk, wuvt = _layer_weights(w_in[l], w_uq[l], w_uk[l], w_uv[l])
        g1 = ln1_g[l].reshape(1, D_MODEL)
        g2 = ln2_g[l].reshape(1, D_MODEL)
        qt, k, vt = _qkv(x, mod[l], g1, w_in_a, q_norm_g[l].reshape(1, Q_LORA),
                         kv_norm_g[l].reshape(1, KV_LORA), wuqt, wuqrt, wuk, wuvt, tables)
        ot = _attention(qt, k, vt)
        x = _mix(x, ot, mod[l], g1, g2, gf, w_in_b, w_pool[l].astype(BF16),
                 pool_scale[l].reshape(1, POOL_DIM), p_pool[l].astype(BF16), p_attn[l].astype(BF16),
                 w_out[l].astype(BF16), w_ff1[l].astype(BF16), w_ff2[l].astype(BF16),
                 final=(l == depth - 1))
    return x
```

```python
import functools
import math

import jax
import jax.numpy as jnp
from jax import lax
from jax.experimental import pallas as pl
from jax.experimental.pallas import tpu as pltpu

D_MODEL = 1024
N_HEADS = 8
QK_NOPE = 64
QK_ROPE = 32
V_DIM = 64
Q_LORA = 384
KV_LORA = 256
POOL_WINDOWS = (2, 4, 8, 16)
POOL_GROUP_DIM = 128
POOL_DIM = len(POOL_WINDOWS) * POOL_GROUP_DIM
ATTN_DIM = N_HEADS * V_DIM
D_FF = 4 * D_MODEL
N_MOD = 6
EPS = 1e-6
ROPE_THETA = 10000.0

HEAD_PAD = 128
ROPE_LO = QK_NOPE
ROPE_HI = QK_NOPE + QK_ROPE
POOL_HALO = 16

VMEM_LIMIT_BYTES = 56 * 1024 * 1024

F32 = jnp.float32
BF16 = jnp.bfloat16

TM_QKV = 512
TM_MIX = 512
TQ = 512
TK = 512
ATTN_HEADS = 4
BF16_SUBLANES = 16
ACC_ROWS = V_DIM + BF16_SUBLANES
MASK_BIAS = -1e30
FF_CHUNK = 1024
MOD_TN = 1536


def _nt_dot(a, b):
    return lax.dot_general(a, b, (((1,), (1,)), ((), ())), preferred_element_type=F32)


def _tn_dot(a, b):
    return lax.dot_general(a, b, (((0,), (0,)), ((), ())), preferred_element_type=F32)


def _rms(x):
    return x * lax.rsqrt(jnp.mean(x * x, axis=-1, keepdims=True) + EPS)


def _sigmoid(x):
    return 1.0 / (1.0 + jnp.exp(-x))


def _mod_rows(mod_ref, chunks):
    row = pl.ds(pl.program_id(0), 1)
    return tuple(mod_ref[0, row, c * D_MODEL:(c + 1) * D_MODEL] for c in chunks)


def _layer_spec(arr, layer, **kw):
    tail = (0,) * (arr.ndim - 1)
    return pl.BlockSpec((1,) + arr.shape[1:], lambda i, j: (layer,) + tail, **kw)


def _whole_spec(arr, **kw):
    zeros = (0,) * arr.ndim
    return pl.BlockSpec(arr.shape, lambda i, j: zeros, **kw)


def _rope_tables_kernel(pos_ref, invf_ref, cost_ref, sint_ref):
    ang = invf_ref[...] * pos_ref[0].astype(F32)
    cost_ref[0] = jnp.cos(ang)
    sint_ref[0] = jnp.sin(ang)


def _rope_tables(positions):
    b, s = positions.shape
    inv_freq = ROPE_THETA ** (-jnp.arange(0, QK_ROPE, 2, dtype=F32) / QK_ROPE)
    invf_col = jnp.concatenate([inv_freq, inv_freq]).reshape(QK_ROPE, 1)
    return pl.pallas_call(
        _rope_tables_kernel,
        grid=(b,),
        in_specs=[
            pl.BlockSpec((1, 1, s), lambda i: (i, 0, 0)),
            pl.BlockSpec((QK_ROPE, 1), lambda i: (0, 0)),
        ],
        out_specs=[
            pl.BlockSpec((1, QK_ROPE, s), lambda i: (i, 0, 0)),
            pl.BlockSpec((1, QK_ROPE, s), lambda i: (i, 0, 0)),
        ],
        out_shape=[
            jax.ShapeDtypeStruct((b, QK_ROPE, s), F32),
            jax.ShapeDtypeStruct((b, QK_ROPE, s), F32),
        ],
        compiler_params=pltpu.CompilerParams(vmem_limit_bytes=VMEM_LIMIT_BYTES),
        name="rope_tables",
    )(positions.reshape(b, 1, s), invf_col)


def _mod_kernel(c_ref, w_ref, b_ref, o_ref):
    c = c_ref[...]
    c_act = c * _sigmoid(c)
    o_ref[0] = jnp.dot(c_act.astype(BF16), w_ref[0].astype(BF16),
                       preferred_element_type=F32) + b_ref[0]


def _modulation(c, w_ada, b_ada):
    depth, d, n = w_ada.shape
    b = c.shape[0]
    rows = 8
    c_pad = jnp.pad(c, ((0, rows - b), (0, 0)))
    out = pl.pallas_call(
        _mod_kernel,
        grid=(depth, n // MOD_TN),
        in_specs=[
            pl.BlockSpec((rows, d), lambda l, j: (0, 0)),
            pl.BlockSpec((1, d, MOD_TN), lambda l, j: (l, 0, j)),
            pl.BlockSpec((1, 1, MOD_TN), lambda l, j: (l, 0, j)),
        ],
        out_specs=pl.BlockSpec((1, rows, MOD_TN), lambda l, j: (l, 0, j)),
        out_shape=jax.ShapeDtypeStruct((depth, rows, n), F32),
        compiler_params=pltpu.CompilerParams(vmem_limit_bytes=VMEM_LIMIT_BYTES),
        name="adaln_mod",
    )(c_pad, w_ada, b_ada.reshape(depth, 1, n))
    return out


def _qkv_kernel(x_ref, mod_ref, g_ref, w_in_ref, gq_ref, gkv_ref, wuqt_ref, wuqrt_ref,
                wuk_ref, wuvt_ref, cost_ref, sint_ref,
                qt_ref, k_ref, vt_ref, *, scale, layer):
    x = x_ref[0]
    shift, scl = _mod_rows(mod_ref, (0, 1))
    h = (_rms(x) * g_ref[layer:layer + 1, :]) * (1.0 + scl) + shift
    z = jnp.dot(h.astype(BF16), w_in_ref[0], preferred_element_type=F32)
    c_q = z[:, 0:Q_LORA]
    c_kv = z[:, Q_LORA:Q_LORA + KV_LORA]
    kr = z[:, Q_LORA + KV_LORA:Q_LORA + KV_LORA + HEAD_PAD]
    kr_rot = z[:, Q_LORA + KV_LORA + HEAD_PAD:]
    cqn = (_rms(c_q) * gq_ref[layer:layer + 1, :]).astype(BF16)
    ckvn = (_rms(c_kv) * gkv_ref[layer:layer + 1, :]).astype(BF16)

    qt = _nt_dot(wuqt_ref[0], cqn)
    qt_rot = _nt_dot(wuqrt_ref[0], cqn)
    cost = cost_ref[0]
    sint = sint_ref[0]
    for hd in range(N_HEADS):
        base = hd * HEAD_PAD
        qt_ref[0, base:base + ROPE_LO, :] = (qt[base:base + ROPE_LO] * scale).astype(BF16)
        roped = (qt[base + ROPE_LO:base + ROPE_HI] * cost
                 + qt_rot[hd * QK_ROPE:(hd + 1) * QK_ROPE] * sint)
        qt_ref[0, base + ROPE_LO:base + ROPE_HI, :] = (roped * scale).astype(BF16)
        qt_ref[0, base + ROPE_HI:base + HEAD_PAD, :] = (
            qt[base + ROPE_HI:base + HEAD_PAD] * scale).astype(BF16)

    k = jnp.dot(ckvn, wuk_ref[0], preferred_element_type=F32)
    tm = x.shape[0]
    z_lo = jnp.zeros((ROPE_LO, tm), F32)
    z_hi = jnp.zeros((HEAD_PAD - ROPE_HI, tm), F32)
    cos_tok = jnp.concatenate([z_lo, cost, z_hi], axis=0).T
    sin_tok = jnp.concatenate([z_lo, sint, z_hi], axis=0).T
    kr_full = kr * cos_tok + kr_rot * sin_tok
    for hd in range(N_HEADS):
        base = hd * HEAD_PAD
        k_ref[0, :, base:base + HEAD_PAD] = (k[:, base:base + HEAD_PAD] + kr_full).astype(BF16)

    vt_ref[0] = _nt_dot(wuvt_ref[0], ckvn).astype(BF16)


def _qkv(x, mod, tables, wts, layer):
    b, s, d = x.shape
    tm = TM_QKV
    cost, sint = tables
    scale = math.log2(math.e) / math.sqrt(QK_NOPE + QK_ROPE)
    stacked = [wts["ln1_g"], wts["w_in_a"], wts["q_norm_g"], wts["kv_norm_g"], wts["wuqt"], wts["wuqrt"],
               wts["wuk"], wts["wuvt"]]
    return pl.pallas_call(
        functools.partial(_qkv_kernel, scale=scale, layer=layer),
        grid=(b, s // tm),
        in_specs=[
            pl.BlockSpec((1, tm, d), lambda i, j: (i, j, 0)),
            _layer_spec(mod, layer),
            _whole_spec(wts["ln1_g"]),
            _layer_spec(wts["w_in_a"], layer),
            _whole_spec(wts["q_norm_g"]),
            _whole_spec(wts["kv_norm_g"]),
            _layer_spec(wts["wuqt"], layer),
            _layer_spec(wts["wuqrt"], layer),
            _layer_spec(wts["wuk"], layer),
            _layer_spec(wts["wuvt"], layer),
            pl.BlockSpec((1, QK_ROPE, tm), lambda i, j: (i, 0, j)),
            pl.BlockSpec((1, QK_ROPE, tm), lambda i, j: (i, 0, j)),
        ],
        out_specs=[
            pl.BlockSpec((1, N_HEADS * HEAD_PAD, tm), lambda i, j: (i, 0, j)),
            pl.BlockSpec((1, tm, N_HEADS * HEAD_PAD), lambda i, j: (i, j, 0)),
            pl.BlockSpec((1, ATTN_DIM, tm), lambda i, j: (i, 0, j)),
        ],
        out_shape=[
            jax.ShapeDtypeStruct((b, N_HEADS * HEAD_PAD, s), BF16),
            jax.ShapeDtypeStruct((b, s, N_HEADS * HEAD_PAD), BF16),
            jax.ShapeDtypeStruct((b, ATTN_DIM, s), BF16),
        ],
        compiler_params=pltpu.CompilerParams(
            dimension_semantics=("arbitrary", "arbitrary"),
            vmem_limit_bytes=VMEM_LIMIT_BYTES),
        name="qkv_proj",
    )(x, mod, *stacked, cost, sint)


def _attn_kernel(qt_ref, k_ref, vt_ref, o_ref, s_ref, acc_ref, mask_ref, *, n_q):
    n_tiles = n_q * (n_q + 1) // 2
    ones_rows = jnp.ones((ACC_ROWS - V_DIM, TK), BF16)
    m_init = jnp.finfo(F32).min

    key_pos = lax.broadcasted_iota(jnp.int32, (TK, TQ), 0)
    qry_pos = lax.broadcasted_iota(jnp.int32, (TK, TQ), 1)
    mask_ref[0] = jnp.zeros((TK, TQ), F32)
    mask_ref[1] = jnp.where(qry_pos >= key_pos, 0.0, MASK_BIAS)
    acc_ref[...] = jnp.zeros(acc_ref.shape, F32)

    def advance(qi, j):
        last = j == qi
        return jnp.where(last, qi + 1, qi), jnp.where(last, 0, j + 1)

    def produce(slot, hd, qi, j):
        k_off = pl.multiple_of(j * TK, TK)
        q_off = pl.multiple_of(qi * TQ, TQ)
        kj = k_ref[0, pl.ds(k_off, TK), hd * HEAD_PAD:(hd + 1) * HEAD_PAD]
        qt = qt_ref[0, hd * HEAD_PAD:(hd + 1) * HEAD_PAD, pl.ds(q_off, TQ)]
        s = jnp.dot(kj, qt, preferred_element_type=F32)
        s = s + mask_ref[(j == qi).astype(jnp.int32)]
        s_ref[slot, hd] = s
        return jnp.max(s, axis=0, keepdims=True)

    def consume(slot, hd, qi, j, m, tile_max):
        k_off = pl.multiple_of(j * TK, TK)
        q_off = pl.multiple_of(qi * TQ, TQ)
        m_new = jnp.maximum(m, tile_max)
        alpha = jnp.exp2(m - m_new)
        p = jnp.exp2(s_ref[slot, hd] - m_new).astype(BF16)
        v_ext = jnp.concatenate(
            [vt_ref[0, hd * V_DIM:(hd + 1) * V_DIM, pl.ds(k_off, TK)], ones_rows], axis=0)
        acc = alpha * acc_ref[hd] + jnp.dot(v_ext, p, preferred_element_type=F32)
        o_ref[0, hd * V_DIM:(hd + 1) * V_DIM, pl.ds(q_off, TQ)] = (
            acc[0:V_DIM] * (1.0 / acc[V_DIM:V_DIM + 1])).astype(o_ref.dtype)
        last = j == qi
        acc_ref[hd] = jnp.where(last, 0.0, acc)
        return jnp.where(last, m_init, m_new)

    def step(slot, tile, ms, maxes, *, produce_next=True):
        qi, j = tile
        nxt = advance(qi, j)
        new_ms, new_maxes = [], []
        for hd in range(ATTN_HEADS):
            if produce_next:
                new_maxes.append(produce(1 - slot, hd, *nxt))
            new_ms.append(consume(slot, hd, qi, j, ms[hd], maxes[hd]))
        return nxt, tuple(new_ms), tuple(new_maxes)

    zero = jnp.int32(0)
    tile0 = (zero, zero)
    maxes0 = tuple(produce(0, hd, *tile0) for hd in range(ATTN_HEADS))
    ms0 = tuple(jnp.full((1, TQ), m_init, F32) for _ in range(ATTN_HEADS))

    def body(_, carry):
        tile, ms, maxes = carry
        tile, ms, maxes = step(0, tile, ms, maxes)
        return step(1, tile, ms, maxes)

    tile, ms, maxes = lax.fori_loop(0, n_tiles // 2 - 1, body, (tile0, ms0, maxes0))
    tile, ms, maxes = step(0, tile, ms, maxes)
    step(1, tile, ms, maxes, produce_next=False)


def _attention(qt, k, vt):
    b, _, s = qt.shape
    hb = ATTN_HEADS
    n_q = s // TQ
    assert (n_q * (n_q + 1) // 2) % 2 == 0, "the tile stream is consumed two tiles per loop trip"
    return pl.pallas_call(
        functools.partial(_attn_kernel, n_q=n_q),
        grid=(b, N_HEADS // hb),
        in_specs=[
            pl.BlockSpec((1, hb * HEAD_PAD, s), lambda i, h: (i, h, 0)),
            pl.BlockSpec((1, s, hb * HEAD_PAD), lambda i, h: (i, 0, h)),
            pl.BlockSpec((1, hb * V_DIM, s), lambda i, h: (i, h, 0)),
        ],
        out_specs=pl.BlockSpec((1, hb * V_DIM, s), lambda i, h: (i, h, 0)),
        out_shape=jax.ShapeDtypeStruct((b, ATTN_DIM, s), BF16),
        scratch_shapes=[
            pltpu.VMEM((2, hb, TK, TQ), F32),
            pltpu.VMEM((hb, ACC_ROWS, TQ), F32),
            pltpu.VMEM((2, TK, TQ), F32),
        ],
        compiler_params=pltpu.CompilerParams(
            dimension_semantics=("arbitrary", "arbitrary"),
            vmem_limit_bytes=VMEM_LIMIT_BYTES),
        name="mla_attention",
    )(qt, k, vt)


def _mix_kernel(x_ref, ot_ref, mod_ref, g1_ref, g2_ref, gf_ref, w_in_ref, w_pool_ref, pscale_ref,
                p_pool_ref, p_attn_ref, w_out_ref, w_ff1_ref, w_ff2_ref,
                o_ref, uext_ref, *, final, layer):
    tm = x_ref.shape[1]
    si = pl.program_id(1)

    @pl.when(si == 0)
    def _():
        uext_ref[0:POOL_HALO, :] = jnp.zeros((POOL_HALO, POOL_DIM), F32)

    @pl.when(si > 0)
    def _():
        uext_ref[0:POOL_HALO, :] = uext_ref[tm:tm + POOL_HALO, :]

    x = x_ref[0]
    shift1, scale1, gate1, shift2, scale2, gate2 = _mod_rows(mod_ref, range(N_MOD))

    y_b = _tn_dot(ot_ref[0], p_attn_ref[0])
    h = ((_rms(x) * g1_ref[layer:layer + 1, :]) * (1.0 + scale1) + shift1).astype(BF16)
    u = jnp.dot(h, w_in_ref[0, :, 0:POOL_DIM], preferred_element_type=F32)
    gz = jnp.dot(h, w_in_ref[0, :, POOL_DIM:], preferred_element_type=F32)
    gz_a = gz[:, 0:D_MODEL]
    gz_b = gz[:, D_MODEL:]

    uext_ref[POOL_HALO:POOL_HALO + tm, :] = u
    head_pos = si * tm + lax.broadcasted_iota(jnp.int32, (POOL_HALO, 1), 0)
    pooled = []
    for g, w in enumerate(POOL_WINDOWS):
        eg = uext_ref[:, g * POOL_GROUP_DIM:(g + 1) * POOL_GROUP_DIM]
        win = eg
        k = 1
        while k < w:
            win = win + pltpu.roll(win, k, axis=0)
            k *= 2
        win = win[POOL_HALO:]
        ug = eg[POOL_HALO:]
        inv_head = 1.0 / jnp.minimum(head_pos + 1, w).astype(F32)
        mean = jnp.concatenate([win[:POOL_HALO] * inv_head, win[POOL_HALO:] * (1.0 / w)], axis=0)
        yg = jnp.dot((mean - ug).astype(BF16), w_pool_ref[0, g], preferred_element_type=F32)
        pooled.append(yg)
    y_pool = jnp.concatenate(pooled, axis=-1) * pscale_ref[layer:layer + 1, :]
    y_a = jnp.dot(y_pool.astype(BF16), p_pool_ref[0], preferred_element_type=F32)

    merged = _sigmoid(gz_a) * y_a + _sigmoid(gz_b) * y_b
    x1 = x + gate1 * jnp.dot(merged.astype(BF16), w_out_ref[0], preferred_element_type=F32)

    h2 = ((_rms(x1) * g2_ref[layer:layer + 1, :]) * (1.0 + scale2) + shift2).astype(BF16)
    ff = jnp.zeros((tm, D_MODEL), F32)
    for c0 in range(0, D_FF, FF_CHUNK):
        t = jnp.dot(h2, w_ff1_ref[0, :, c0:c0 + FF_CHUNK], preferred_element_type=F32)
        t = jnp.square(jnp.maximum(t, 0.0)).astype(BF16)
        ff = ff + jnp.dot(t, w_ff2_ref[0, c0:c0 + FF_CHUNK, :], preferred_element_type=F32)
    x2 = x1 + gate2 * ff
    if final:
        x2 = _rms(x2) * gf_ref[...]
    o_ref[0] = x2


def _mix(x, ot, mod, wts, layer, *, final):
    b, s, d = x.shape
    tm = TM_MIX
    once = dict(pipeline_mode=pl.Buffered(1))
    names = ["ln1_g", "ln2_g", "final_g", "w_in_b", "w_pool", "pool_scale", "p_pool", "p_attn", "w_out",
             "w_ff1", "w_ff2"]
    whole = {"ln1_g", "ln2_g", "final_g", "pool_scale"}
    specs = [_whole_spec(wts[n], **once) if n in whole else _layer_spec(wts[n], layer, **once)
             for n in names]
    return pl.pallas_call(
        functools.partial(_mix_kernel, final=final, layer=layer),
        grid=(b, s // tm),
        in_specs=[
            pl.BlockSpec((1, tm, d), lambda i, j: (i, j, 0)),
            pl.BlockSpec((1, ATTN_DIM, tm), lambda i, j: (i, 0, j)),
            _layer_spec(mod, layer, **once),
        ] + specs,
        out_specs=pl.BlockSpec((1, tm, d), lambda i, j: (i, j, 0)),
        out_shape=jax.ShapeDtypeStruct((b, s, d), F32),
        scratch_shapes=[pltpu.VMEM((tm + POOL_HALO, POOL_DIM), F32)],
        compiler_params=pltpu.CompilerParams(
            dimension_semantics=("arbitrary", "arbitrary"),
            vmem_limit_bytes=VMEM_LIMIT_BYTES),
        name="mix_mlp",
    )(x, ot, mod, *[wts[n] for n in names])


def _prep_weights(ln1_g, ln2_g, w_in, q_norm_g, w_uq, kv_norm_g, w_uk, w_uv, w_pool, pool_scale,
                  p_pool, p_attn, w_out, w_ff1, w_ff2, final_g):
    depth, d, _ = w_in.shape
    c0 = POOL_DIM
    c1 = c0 + Q_LORA
    c2 = c1 + KV_LORA
    c3 = c2 + QK_ROPE
    half = QK_ROPE // 2
    w_kr = w_in[:, :, c2:c3]
    zl = jnp.zeros((depth, d, ROPE_LO), F32)
    zr = jnp.zeros((depth, d, HEAD_PAD - ROPE_HI), F32)
    w_in_a = jnp.concatenate([w_in[:, :, c0:c2], zl, w_kr, zr, zl, -w_kr[:, :, half:], w_kr[:, :, :half], zr],
                             axis=2).astype(BF16)
    w_in_b = jnp.concatenate([w_in[:, :, :c0], w_in[:, :, c3:]], axis=2).astype(BF16)

    pad_q = HEAD_PAD - (QK_NOPE + QK_ROPE)
    wuq_pad = jnp.pad(w_uq, ((0, 0), (0, 0), (0, 0), (0, pad_q)))
    wuqt = wuq_pad.reshape(depth, Q_LORA, N_HEADS * HEAD_PAD).transpose(0, 2, 1).astype(BF16)
    r = w_uq[..., QK_NOPE:]
    rot = jnp.concatenate([-r[..., half:], r[..., :half]], axis=-1)
    wuqrt = rot.reshape(depth, Q_LORA, N_HEADS * QK_ROPE).transpose(0, 2, 1).astype(BF16)
    wuk = jnp.pad(w_uk, ((0, 0), (0, 0), (0, 0), (0, HEAD_PAD - QK_NOPE)))
    wuk = wuk.reshape(depth, KV_LORA, N_HEADS * HEAD_PAD).astype(BF16)
    wuvt = w_uv.reshape(depth, KV_LORA, ATTN_DIM).transpose(0, 2, 1).astype(BF16)
    return dict(
        ln1_g=ln1_g, ln2_g=ln2_g, final_g=final_g.reshape(1, D_MODEL), q_norm_g=q_norm_g,
        kv_norm_g=kv_norm_g, pool_scale=pool_scale, w_in_a=w_in_a, w_in_b=w_in_b, wuqt=wuqt, wuqrt=wuqrt,
        wuk=wuk, wuvt=wuvt, w_pool=w_pool.astype(BF16), p_pool=p_pool.astype(BF16),
        p_attn=p_attn.astype(BF16), w_out=w_out.astype(BF16), w_ff1=w_ff1.astype(BF16),
        w_ff2=w_ff2.astype(BF16))


def kernel(x, c, positions, ln1_g, ln2_g, w_ada, b_ada, w_in, q_norm_g, w_uq, kv_norm_g, w_uk,
           w_uv, w_pool, pool_scale, p_pool, p_attn, w_out, w_ff1, w_ff2, final_g):
    depth = w_in.shape[0]
    tables = _rope_tables(positions)
    mod = _modulation(c, w_ada, b_ada)
    wts = _prep_weights(ln1_g, ln2_g, w_in, q_norm_g, w_uq, kv_norm_g, w_uk, w_uv, w_pool, pool_scale,
                        p_pool, p_attn, w_out, w_ff1, w_ff2, final_g)
    for layer in range(depth):
        qt, k, vt = _qkv(x, mod, tables, wts, layer)
        ot = _attention(qt, k, vt)
        x = _mix(x, ot, mod, wts, layer, final=(layer == depth - 1))
    return x
```

```python
import functools
import math

import jax
import jax.numpy as jnp
from jax import lax
from jax.experimental import pallas as pl
from jax.experimental.pallas import tpu as pltpu

D_MODEL = 1024
N_HEADS = 8
QK_NOPE = 64
QK_ROPE = 32
V_DIM = 64
Q_LORA = 384
KV_LORA = 256
POOL_WINDOWS = (2, 4, 8, 16)
POOL_GROUP_DIM = 128
POOL_DIM = len(POOL_WINDOWS) * POOL_GROUP_DIM
ATTN_DIM = N_HEADS * V_DIM
D_FF = 4 * D_MODEL
N_MOD = 6
EPS = 1e-6
ROPE_THETA = 10000.0

HEAD_PAD = 128
ROPE_LO = QK_NOPE
ROPE_HI = QK_NOPE + QK_ROPE
POOL_HALO = 16

VMEM_LIMIT_BYTES = 56 * 1024 * 1024

F32 = jnp.float32
BF16 = jnp.bfloat16

TM_QKV = 512
TM_MIX = 512
TQ = 512
TK = 512
ATTN_HEADS = 4
BF16_SUBLANES = 16
ACC_ROWS = V_DIM + BF16_SUBLANES
FF_CHUNK = 1024
MOD_TN = 1536


def _nt_dot(a, b):
    return lax.dot_general(a, b, (((1,), (1,)), ((), ())), preferred_element_type=F32)


def _tn_dot(a, b):
    return lax.dot_general(a, b, (((0,), (0,)), ((), ())), preferred_element_type=F32)


def _rms(x):
    return x * lax.rsqrt(jnp.mean(x * x, axis=-1, keepdims=True) + EPS)


def _sigmoid(x):
    return 1.0 / (1.0 + jnp.exp(-x))


def _mod_rows(mod_ref, chunks):
    row = pl.ds(pl.program_id(0), 1)
    return tuple(mod_ref[0, row, c * D_MODEL:(c + 1) * D_MODEL] for c in chunks)


def _layer_spec(arr, layer, **kw):
    tail = (0,) * (arr.ndim - 1)
    return pl.BlockSpec((1,) + arr.shape[1:], lambda i, j: (layer,) + tail, **kw)


def _whole_spec(arr, **kw):
    zeros = (0,) * arr.ndim
    return pl.BlockSpec(arr.shape, lambda i, j: zeros, **kw)


def _rope_tables_kernel(pos_ref, invf_ref, cost_ref, sint_ref):
    ang = invf_ref[...] * pos_ref[0].astype(F32)
    cost_ref[0] = jnp.cos(ang)
    sint_ref[0] = jnp.sin(ang)


def _rope_tables(positions):
    b, s = positions.shape
    inv_freq = ROPE_THETA ** (-jnp.arange(0, QK_ROPE, 2, dtype=F32) / QK_ROPE)
    invf_col = jnp.concatenate([inv_freq, inv_freq]).reshape(QK_ROPE, 1)
    return pl.pallas_call(
        _rope_tables_kernel,
        grid=(b,),
        in_specs=[
            pl.BlockSpec((1, 1, s), lambda i: (i, 0, 0)),
            pl.BlockSpec((QK_ROPE, 1), lambda i: (0, 0)),
        ],
        out_specs=[
            pl.BlockSpec((1, QK_ROPE, s), lambda i: (i, 0, 0)),
            pl.BlockSpec((1, QK_ROPE, s), lambda i: (i, 0, 0)),
        ],
        out_shape=[
            jax.ShapeDtypeStruct((b, QK_ROPE, s), F32),
            jax.ShapeDtypeStruct((b, QK_ROPE, s), F32),
        ],
        compiler_params=pltpu.CompilerParams(vmem_limit_bytes=VMEM_LIMIT_BYTES),
        name="rope_tables",
    )(positions.reshape(b, 1, s), invf_col)


def _mod_kernel(c_ref, w_ref, b_ref, o_ref):
    c = c_ref[...]
    c_act = c * _sigmoid(c)
    o_ref[0] = jnp.dot(c_act.astype(BF16), w_ref[0].astype(BF16),
                       preferred_element_type=F32) + b_ref[0]


def _modulation(c, w_ada, b_ada):
    depth, d, n = w_ada.shape
    b = c.shape[0]
    rows = 8
    c_pad = jnp.pad(c, ((0, rows - b), (0, 0)))
    out = pl.pallas_call(
        _mod_kernel,
        grid=(depth, n // MOD_TN),
        in_specs=[
            pl.BlockSpec((rows, d), lambda l, j: (0, 0)),
            pl.BlockSpec((1, d, MOD_TN), lambda l, j: (l, 0, j)),
            pl.BlockSpec((1, 1, MOD_TN), lambda l, j: (l, 0, j)),
        ],
        out_specs=pl.BlockSpec((1, rows, MOD_TN), lambda l, j: (l, 0, j)),
        out_shape=jax.ShapeDtypeStruct((depth, rows, n), F32),
        compiler_params=pltpu.CompilerParams(vmem_limit_bytes=VMEM_LIMIT_BYTES),
        name="adaln_mod",
    )(c_pad, w_ada, b_ada.reshape(depth, 1, n))
    return out


def _qkv_kernel(x_ref, mod_ref, g_ref, w_in_ref, gq_ref, gkv_ref, wuqt_ref, wuqrt_ref,
                wuk_ref, wuvt_ref, cost_ref, sint_ref,
                qt_ref, k_ref, vt_ref, *, scale, layer):
    x = x_ref[0]
    shift, scl = _mod_rows(mod_ref, (0, 1))
    h = (_rms(x) * g_ref[layer:layer + 1, :]) * (1.0 + scl) + shift
    z = jnp.dot(h.astype(BF16), w_in_ref[0], preferred_element_type=F32)
    c_q = z[:, 0:Q_LORA]
    c_kv = z[:, Q_LORA:Q_LORA + KV_LORA]
    kr = z[:, Q_LORA + KV_LORA:Q_LORA + KV_LORA + HEAD_PAD]
    kr_rot = z[:, Q_LORA + KV_LORA + HEAD_PAD:]
    cqn = (_rms(c_q) * gq_ref[layer:layer + 1, :]).astype(BF16)
    ckvn = (_rms(c_kv) * gkv_ref[layer:layer + 1, :]).astype(BF16)

    qt = _nt_dot(wuqt_ref[0], cqn)
    qt_rot = _nt_dot(wuqrt_ref[0], cqn)
    cost = cost_ref[0]
    sint = sint_ref[0]
    for hd in range(N_HEADS):
        base = hd * HEAD_PAD
        qt_ref[0, base:base + ROPE_LO, :] = (qt[base:base + ROPE_LO] * scale).astype(BF16)
        roped = (qt[base + ROPE_LO:base + ROPE_HI] * cost
                 + qt_rot[hd * QK_ROPE:(hd + 1) * QK_ROPE] * sint)
        qt_ref[0, base + ROPE_LO:base + ROPE_HI, :] = (roped * scale).astype(BF16)
        qt_ref[0, base + ROPE_HI:base + HEAD_PAD, :] = (
            qt[base + ROPE_HI:base + HEAD_PAD] * scale).astype(BF16)

    k = jnp.dot(ckvn, wuk_ref[0], preferred_element_type=F32)
    tm = x.shape[0]
    z_lo = jnp.zeros((ROPE_LO, tm), F32)
    z_hi = jnp.zeros((HEAD_PAD - ROPE_HI, tm), F32)
    cos_tok = jnp.concatenate([z_lo, cost, z_hi], axis=0).T
    sin_tok = jnp.concatenate([z_lo, sint, z_hi], axis=0).T
    kr_full = kr * cos_tok + kr_rot * sin_tok
    for hd in range(N_HEADS):
        base = hd * HEAD_PAD
        k_ref[0, :, base:base + HEAD_PAD] = (k[:, base:base + HEAD_PAD] + kr_full).astype(BF16)

    vt_ref[0] = _nt_dot(wuvt_ref[0], ckvn).astype(BF16)


def _qkv(x, mod, tables, wts, layer):
    b, s, d = x.shape
    tm = TM_QKV
    cost, sint = tables
    scale = math.log2(math.e) / math.sqrt(QK_NOPE + QK_ROPE)
    stacked = [wts["ln1_g"], wts["w_in_a"], wts["q_norm_g"], wts["kv_norm_g"], wts["wuqt"], wts["wuqrt"],
               wts["wuk"], wts["wuvt"]]
    return pl.pallas_call(
        functools.partial(_qkv_kernel, scale=scale, layer=layer),
        grid=(b, s // tm),
        in_specs=[
            pl.BlockSpec((1, tm, d), lambda i, j: (i, j, 0)),
            _layer_spec(mod, layer),
            _whole_spec(wts["ln1_g"]),
            _layer_spec(wts["w_in_a"], layer),
            _whole_spec(wts["q_norm_g"]),
            _whole_spec(wts["kv_norm_g"]),
            _layer_spec(wts["wuqt"], layer),
            _layer_spec(wts["wuqrt"], layer),
            _layer_spec(wts["wuk"], layer),
            _layer_spec(wts["wuvt"], layer),
            pl.BlockSpec((1, QK_ROPE, tm), lambda i, j: (i, 0, j)),
            pl.BlockSpec((1, QK_ROPE, tm), lambda i, j: (i, 0, j)),
        ],
        out_specs=[
            pl.BlockSpec((1, N_HEADS * HEAD_PAD, tm), lambda i, j: (i, 0, j)),
            pl.BlockSpec((1, tm, N_HEADS * HEAD_PAD), lambda i, j: (i, j, 0)),
            pl.BlockSpec((1, ATTN_DIM, tm), lambda i, j: (i, 0, j)),
        ],
        out_shape=[
            jax.ShapeDtypeStruct((b, N_HEADS * HEAD_PAD, s), BF16),
            jax.ShapeDtypeStruct((b, s, N_HEADS * HEAD_PAD), BF16),
            jax.ShapeDtypeStruct((b, ATTN_DIM, s), BF16),
        ],
        compiler_params=pltpu.CompilerParams(
            dimension_semantics=("arbitrary", "arbitrary"),
            vmem_limit_bytes=VMEM_LIMIT_BYTES),
        name="qkv_proj",
    )(x, mod, *stacked, cost, sint)


def _attn_kernel(qt_ref, k_ref, vt_ref, o_ref, s_ref, acc_ref, m_ref, *, n_q):
    n_below = n_q * (n_q - 1) // 2
    hq = TQ // 2
    ones_rows = jnp.ones((ACC_ROWS - V_DIM, TK), BF16)
    half_mask = (lax.broadcasted_iota(jnp.int32, (hq, hq), 1)
                 >= lax.broadcasted_iota(jnp.int32, (hq, hq), 0))

    acc_ref[...] = jnp.zeros(acc_ref.shape, F32)
    m_ref[...] = jnp.full(m_ref.shape, jnp.finfo(F32).min, F32)

    def keys(hd, off, n):
        return k_ref[0, pl.ds(off, n), hd * HEAD_PAD:(hd + 1) * HEAD_PAD]

    def queries(hd, off, n):
        return qt_ref[0, hd * HEAD_PAD:(hd + 1) * HEAD_PAD, pl.ds(off, n)]

    def values(hd, off, n):
        return jnp.concatenate(
            [vt_ref[0, hd * V_DIM:(hd + 1) * V_DIM, pl.ds(off, n)], ones_rows[:, :n]], axis=0)

    def col_max(s):
        return jnp.max(s, axis=0, keepdims=True)

    def produce_below(slot, hd, qi, j):
        s = jnp.dot(keys(hd, pl.multiple_of(j * TK, TK), TK), queries(hd, pl.multiple_of(qi * TQ, TQ), TQ),
                    preferred_element_type=F32)
        s_ref[slot, hd] = s
        return col_max(s)

    def consume_below(slot, hd, qi, j, tile_max):
        m = m_ref[qi, hd]
        m_new = jnp.maximum(m, tile_max)
        p = jnp.exp2(s_ref[slot, hd] - m_new).astype(BF16)
        acc_ref[qi, hd] = (jnp.exp2(m - m_new) * acc_ref[qi, hd]
                           + jnp.dot(values(hd, pl.multiple_of(j * TK, TK), TK), p, preferred_element_type=F32))
        m_ref[qi, hd] = m_new

    def produce_diag(slot, hd, d):
        off = pl.multiple_of(d * TQ, TQ)
        s_ref[slot, hd, 0:hq, :] = jnp.dot(keys(hd, off, hq), queries(hd, off, TQ),
                                           preferred_element_type=F32)
        s_ref[slot, hd, hq:TK, hq:TQ] = jnp.dot(keys(hd, off + hq, hq), queries(hd, off + hq, hq),
                                                preferred_element_type=F32)

    def consume_diag(slot, hd, d):
        off = pl.multiple_of(d * TQ, TQ)
        neg = jnp.finfo(F32).min
        s_tl = jnp.where(half_mask, s_ref[slot, hd, 0:hq, 0:hq], neg)
        s_tr = s_ref[slot, hd, 0:hq, hq:TQ]
        s_br = jnp.where(half_mask, s_ref[slot, hd, hq:TK, hq:TQ], neg)
        m = m_ref[d, hd]
        m_l = jnp.maximum(m[:, 0:hq], col_max(s_tl))
        m_r = jnp.maximum(m[:, hq:TQ], jnp.maximum(col_max(s_tr), col_max(s_br)))
        p_l = jnp.exp2(s_tl - m_l).astype(BF16)
        p_r = jnp.concatenate([jnp.exp2(s_tr - m_r), jnp.exp2(s_br - m_r)], axis=0).astype(BF16)
        acc = acc_ref[d, hd]
        acc_l = (jnp.exp2(m[:, 0:hq] - m_l) * acc[:, 0:hq]
                 + jnp.dot(values(hd, off, hq), p_l, preferred_element_type=F32))
        acc_r = (jnp.exp2(m[:, hq:TQ] - m_r) * acc[:, hq:TQ]
                 + jnp.dot(values(hd, off, TK), p_r, preferred_element_type=F32))
        rows = slice(hd * V_DIM, (hd + 1) * V_DIM)
        o_ref[0, rows, pl.ds(off, hq)] = (
            acc_l[0:V_DIM] * (1.0 / acc_l[V_DIM:V_DIM + 1])).astype(o_ref.dtype)
        o_ref[0, rows, pl.ds(off + hq, hq)] = (
            acc_r[0:V_DIM] * (1.0 / acc_r[V_DIM:V_DIM + 1])).astype(o_ref.dtype)

    heads = range(ATTN_HEADS)

    def next_below(qi, j):
        row_end = j + 1 == qi
        return jnp.where(row_end, qi + 1, qi), jnp.where(row_end, 0, j + 1)

    def below_step(slot, tile, maxes, to_diag=False):
        nxt = next_below(*tile)
        new_maxes = []
        for hd in heads:
            if to_diag:
                produce_diag(1 - slot, hd, 0)
            else:
                new_maxes.append(produce_below(1 - slot, hd, *nxt))
            consume_below(slot, hd, *tile, maxes[hd])
        return nxt, tuple(new_maxes)

    def diag_step(slot, d, produce_next=True):
        for hd in heads:
            if produce_next:
                produce_diag(1 - slot, hd, d + 1)
            consume_diag(slot, hd, d)
        return d + 1

    tile0 = (jnp.int32(1), jnp.int32(0))
    maxes0 = tuple(produce_below(0, hd, *tile0) for hd in heads)

    def below_pair(_, carry):
        tile, maxes = carry
        tile, maxes = below_step(0, tile, maxes)
        return below_step(1, tile, maxes)

    tile, maxes = lax.fori_loop(0, n_below // 2 - 1, below_pair, (tile0, maxes0))
    tile, maxes = below_step(0, tile, maxes)
    below_step(1, tile, maxes, to_diag=True)

    def diag_pair(_, d):
        return diag_step(1, diag_step(0, d))

    d = lax.fori_loop(0, n_q // 2 - 1, diag_pair, jnp.int32(0))
    d = diag_step(0, d)
    diag_step(1, d, produce_next=False)


def _attention(qt, k, vt):
    b, _, s = qt.shape
    hb = ATTN_HEADS
    n_q = s // TQ
    assert n_q % 2 == 0 and (n_q * (n_q - 1) // 2) % 2 == 0, "both tile streams run two tiles per loop trip"
    return pl.pallas_call(
        functools.partial(_attn_kernel, n_q=n_q),
        grid=(b, N_HEADS // hb),
        in_specs=[
            pl.BlockSpec((1, hb * HEAD_PAD, s), lambda i, h: (i, h, 0)),
            pl.BlockSpec((1, s, hb * HEAD_PAD), lambda i, h: (i, 0, h)),
            pl.BlockSpec((1, hb * V_DIM, s), lambda i, h: (i, h, 0)),
        ],
        out_specs=pl.BlockSpec((1, hb * V_DIM, s), lambda i, h: (i, h, 0)),
        out_shape=jax.ShapeDtypeStruct((b, ATTN_DIM, s), BF16),
        scratch_shapes=[
            pltpu.VMEM((2, hb, TK, TQ), F32),
            pltpu.VMEM((n_q, hb, ACC_ROWS, TQ), F32),
            pltpu.VMEM((n_q, hb, 1, TQ), F32),
        ],
        compiler_params=pltpu.CompilerParams(
            dimension_semantics=("arbitrary", "arbitrary"),
            vmem_limit_bytes=VMEM_LIMIT_BYTES),
        name="mla_attention",
    )(qt, k, vt)


def _mix_kernel(x_ref, ot_ref, mod_ref, g1_ref, g2_ref, gf_ref, w_in_ref, w_pool_ref, pscale_ref,
                p_pool_ref, p_attn_ref, w_out_ref, w_ff1_ref, w_ff2_ref,
                o_ref, uext_ref, *, final, layer):
    tm = x_ref.shape[1]
    si = pl.program_id(1)

    @pl.when(si == 0)
    def _():
        uext_ref[0:POOL_HALO, :] = jnp.zeros((POOL_HALO, POOL_DIM), F32)

    @pl.when(si > 0)
    def _():
        uext_ref[0:POOL_HALO, :] = uext_ref[tm:tm + POOL_HALO, :]

    x = x_ref[0]
    shift1, scale1, gate1, shift2, scale2, gate2 = _mod_rows(mod_ref, range(N_MOD))

    y_b = _tn_dot(ot_ref[0], p_attn_ref[0])
    h = ((_rms(x) * g1_ref[layer:layer + 1, :]) * (1.0 + scale1) + shift1).astype(BF16)
    u = jnp.dot(h, w_in_ref[0, :, 0:POOL_DIM], preferred_element_type=F32)
    gz = jnp.dot(h, w_in_ref[0, :, POOL_DIM:], preferred_element_type=F32)
    gz_a = gz[:, 0:D_MODEL]
    gz_b = gz[:, D_MODEL:]

    uext_ref[POOL_HALO:POOL_HALO + tm, :] = u
    head_pos = si * tm + lax.broadcasted_iota(jnp.int32, (POOL_HALO, 1), 0)
    pooled = []
    for g, w in enumerate(POOL_WINDOWS):
        eg = uext_ref[:, g * POOL_GROUP_DIM:(g + 1) * POOL_GROUP_DIM]
        win = eg
        k = 1
        while k < w:
            win = win + pltpu.roll(win, k, axis=0)
            k *= 2
        win = win[POOL_HALO:]
        ug = eg[POOL_HALO:]
        inv_head = 1.0 / jnp.minimum(head_pos + 1, w).astype(F32)
        mean = jnp.concatenate([win[:POOL_HALO] * inv_head, win[POOL_HALO:] * (1.0 / w)], axis=0)
        yg = jnp.dot((mean - ug).astype(BF16), w_pool_ref[0, g], preferred_element_type=F32)
        pooled.append(yg)
    y_pool = jnp.concatenate(pooled, axis=-1) * pscale_ref[layer:layer + 1, :]
    y_a = jnp.dot(y_pool.astype(BF16), p_pool_ref[0], preferred_element_type=F32)

    merged = _sigmoid(gz_a) * y_a + _sigmoid(gz_b) * y_b
    x1 = x + gate1 * jnp.dot(merged.astype(BF16), w_out_ref[0], preferred_element_type=F32)

    h2 = ((_rms(x1) * g2_ref[layer:layer + 1, :]) * (1.0 + scale2) + shift2).astype(BF16)
    ff = jnp.zeros((tm, D_MODEL), F32)
    for c0 in range(0, D_FF, FF_CHUNK):
        t = jnp.dot(h2, w_ff1_ref[0, :, c0:c0 + FF_CHUNK], preferred_element_type=F32)
        t = jnp.square(jnp.maximum(t, 0.0)).astype(BF16)
        ff = ff + jnp.dot(t, w_ff2_ref[0, c0:c0 + FF_CHUNK, :], preferred_element_type=F32)
    x2 = x1 + gate2 * ff
    if final:
        x2 = _rms(x2) * gf_ref[...]
    o_ref[0] = x2


def _mix(x, ot, mod, wts, layer, *, final):
    b, s, d = x.shape
    tm = TM_MIX
    once = dict(pipeline_mode=pl.Buffered(1))
    names = ["ln1_g", "ln2_g", "final_g", "w_in_b", "w_pool", "pool_scale", "p_pool", "p_attn", "w_out",
             "w_ff1", "w_ff2"]
    whole = {"ln1_g", "ln2_g", "final_g", "pool_scale"}
    specs = [_whole_spec(wts[n], **once) if n in whole else _layer_spec(wts[n], layer, **once)
             for n in names]
    return pl.pallas_call(
        functools.partial(_mix_kernel, final=final, layer=layer),
        grid=(b, s // tm),
        in_specs=[
            pl.BlockSpec((1, tm, d), lambda i, j: (i, j, 0)),
            pl.BlockSpec((1, ATTN_DIM, tm), lambda i, j: (i, 0, j)),
            _layer_spec(mod, layer, **once),
        ] + specs,
        out_specs=pl.BlockSpec((1, tm, d), lambda i, j: (i, j, 0)),
        out_shape=jax.ShapeDtypeStruct((b, s, d), F32),
        scratch_shapes=[pltpu.VMEM((tm + POOL_HALO, POOL_DIM), F32)],
        compiler_params=pltpu.CompilerParams(
            dimension_semantics=("arbitrary", "arbitrary"),
            vmem_limit_bytes=VMEM_LIMIT_BYTES),
        name="mix_mlp",
    )(x, ot, mod, *[wts[n] for n in names])


def _prep_weights(ln1_g, ln2_g, w_in, q_norm_g, w_uq, kv_norm_g, w_uk, w_uv, w_pool, pool_scale,
                  p_pool, p_attn, w_out, w_ff1, w_ff2, final_g):
    depth, d, _ = w_in.shape
    c0 = POOL_DIM
    c1 = c0 + Q_LORA
    c2 = c1 + KV_LORA
    c3 = c2 + QK_ROPE
    half = QK_ROPE // 2
    w_kr = w_in[:, :, c2:c3]
    zl = jnp.zeros((depth, d, ROPE_LO), F32)
    zr = jnp.zeros((depth, d, HEAD_PAD - ROPE_HI), F32)
    w_in_a = jnp.concatenate([w_in[:, :, c0:c2], zl, w_kr, zr, zl, -w_kr[:, :, half:], w_kr[:, :, :half], zr],
                             axis=2).astype(BF16)
    w_in_b = jnp.concatenate([w_in[:, :, :c0], w_in[:, :, c3:]], axis=2).astype(BF16)

    pad_q = HEAD_PAD - (QK_NOPE + QK_ROPE)
    wuq_pad = jnp.pad(w_uq, ((0, 0), (0, 0), (0, 0), (0, pad_q)))
    wuqt = wuq_pad.reshape(depth, Q_LORA, N_HEADS * HEAD_PAD).transpose(0, 2, 1).astype(BF16)
    r = w_uq[..., QK_NOPE:]
    rot = jnp.concatenate([-r[..., half:], r[..., :half]], axis=-1)
    wuqrt = rot.reshape(depth, Q_LORA, N_HEADS * QK_ROPE).transpose(0, 2, 1).astype(BF16)
    wuk = jnp.pad(w_uk, ((0, 0), (0, 0), (0, 0), (0, HEAD_PAD - QK_NOPE)))
    wuk = wuk.reshape(depth, KV_LORA, N_HEADS * HEAD_PAD).astype(BF16)
    wuvt = w_uv.reshape(depth, KV_LORA, ATTN_DIM).transpose(0, 2, 1).astype(BF16)
    return dict(
        ln1_g=ln1_g, ln2_g=ln2_g, final_g=final_g.reshape(1, D_MODEL), q_norm_g=q_norm_g,
        kv_norm_g=kv_norm_g, pool_scale=pool_scale, w_in_a=w_in_a, w_in_b=w_in_b, wuqt=wuqt, wuqrt=wuqrt,
        wuk=wuk, wuvt=wuvt, w_pool=w_pool.astype(BF16), p_pool=p_pool.astype(BF16),
        p_attn=p_attn.astype(BF16), w_out=w_out.astype(BF16), w_ff1=w_ff1.astype(BF16),
        w_ff2=w_ff2.astype(BF16))


def kernel(x, c, positions, ln1_g, ln2_g, w_ada, b_ada, w_in, q_norm_g, w_uq, kv_norm_g, w_uk,
           w_uv, w_pool, pool_scale, p_pool, p_attn, w_out, w_ff1, w_ff2, final_g):
    depth = w_in.shape[0]
    tables = _rope_tables(positions)
    mod = _modulation(c, w_ada, b_ada)
    wts = _prep_weights(ln1_g, ln2_g, w_in, q_norm_g, w_uq, kv_norm_g, w_uk, w_uv, w_pool, pool_scale,
                        p_pool, p_attn, w_out, w_ff1, w_ff2, final_g)
    for layer in range(depth):
        qt, k, vt = _qkv(x, mod, tables, wts, layer)
        ot = _attention(qt, k, vt)
        x = _mix(x, ot, mod, wts, layer, final=(layer == depth - 1))
    return x
```

```python
import functools
import math

import jax
import jax.numpy as jnp
from jax import lax
from jax.experimental import pallas as pl
from jax.experimental.pallas import tpu as pltpu

D_MODEL = 1024
N_HEADS = 8
QK_NOPE = 64
QK_ROPE = 32
V_DIM = 64
Q_LORA = 384
KV_LORA = 256
POOL_WINDOWS = (2, 4, 8, 16)
POOL_GROUP_DIM = 128
POOL_DIM = len(POOL_WINDOWS) * POOL_GROUP_DIM
ATTN_DIM = N_HEADS * V_DIM
D_FF = 4 * D_MODEL
N_MOD = 6
EPS = 1e-6
ROPE_THETA = 10000.0

HEAD_PAD = 128
ROPE_LO = QK_NOPE
ROPE_HI = QK_NOPE + QK_ROPE
POOL_HALO = 16

VMEM_LIMIT_BYTES = 56 * 1024 * 1024

F32 = jnp.float32
BF16 = jnp.bfloat16

TM_QKV = 512
TM_MIX = 512
TQ = 512
TK = 512
ATTN_HEADS = 4
BF16_SUBLANES = 16
ACC_ROWS = V_DIM + BF16_SUBLANES
FF_CHUNK = 1024
MOD_TN = 1536


def _nt_dot(a, b):
    return lax.dot_general(a, b, (((1,), (1,)), ((), ())), preferred_element_type=F32)


def _tn_dot(a, b):
    return lax.dot_general(a, b, (((0,), (0,)), ((), ())), preferred_element_type=F32)


def _rms(x):
    return x * lax.rsqrt(jnp.mean(x * x, axis=-1, keepdims=True) + EPS)


def _sigmoid(x):
    return 1.0 / (1.0 + jnp.exp(-x))


def _mod_rows(mod_ref, chunks):
    row = pl.ds(pl.program_id(0), 1)
    return tuple(mod_ref[0, row, c * D_MODEL:(c + 1) * D_MODEL] for c in chunks)


def _layer_spec(arr, layer, **kw):
    tail = (0,) * (arr.ndim - 1)
    return pl.BlockSpec((1,) + arr.shape[1:], lambda i, j: (layer,) + tail, **kw)


def _whole_spec(arr, **kw):
    zeros = (0,) * arr.ndim
    return pl.BlockSpec(arr.shape, lambda i, j: zeros, **kw)


def _rope_tables_kernel(pos_ref, invf_ref, cost_ref, sint_ref):
    ang = invf_ref[...] * pos_ref[0].astype(F32)
    cost_ref[0] = jnp.cos(ang)
    sint_ref[0] = jnp.sin(ang)


def _rope_tables(positions):
    b, s = positions.shape
    inv_freq = ROPE_THETA ** (-jnp.arange(0, QK_ROPE, 2, dtype=F32) / QK_ROPE)
    invf_col = jnp.concatenate([inv_freq, inv_freq]).reshape(QK_ROPE, 1)
    return pl.pallas_call(
        _rope_tables_kernel,
        grid=(b,),
        in_specs=[
            pl.BlockSpec((1, 1, s), lambda i: (i, 0, 0)),
            pl.BlockSpec((QK_ROPE, 1), lambda i: (0, 0)),
        ],
        out_specs=[
            pl.BlockSpec((1, QK_ROPE, s), lambda i: (i, 0, 0)),
            pl.BlockSpec((1, QK_ROPE, s), lambda i: (i, 0, 0)),
        ],
        out_shape=[
            jax.ShapeDtypeStruct((b, QK_ROPE, s), F32),
            jax.ShapeDtypeStruct((b, QK_ROPE, s), F32),
        ],
        compiler_params=pltpu.CompilerParams(vmem_limit_bytes=VMEM_LIMIT_BYTES),
        name="rope_tables",
    )(positions.reshape(b, 1, s), invf_col)


def _mod_kernel(c_ref, w_ref, b_ref, o_ref):
    c = c_ref[...]
    c_act = c * _sigmoid(c)
    o_ref[0] = jnp.dot(c_act.astype(BF16), w_ref[0].astype(BF16),
                       preferred_element_type=F32) + b_ref[0]


def _modulation(c, w_ada, b_ada):
    depth, d, n = w_ada.shape
    b = c.shape[0]
    rows = 8
    c_pad = jnp.pad(c, ((0, rows - b), (0, 0)))
    out = pl.pallas_call(
        _mod_kernel,
        grid=(depth, n // MOD_TN),
        in_specs=[
            pl.BlockSpec((rows, d), lambda l, j: (0, 0)),
            pl.BlockSpec((1, d, MOD_TN), lambda l, j: (l, 0, j)),
            pl.BlockSpec((1, 1, MOD_TN), lambda l, j: (l, 0, j)),
        ],
        out_specs=pl.BlockSpec((1, rows, MOD_TN), lambda l, j: (l, 0, j)),
        out_shape=jax.ShapeDtypeStruct((depth, rows, n), F32),
        compiler_params=pltpu.CompilerParams(vmem_limit_bytes=VMEM_LIMIT_BYTES),
        name="adaln_mod",
    )(c_pad, w_ada, b_ada.reshape(depth, 1, n))
    return out


def _qkv_kernel(x_ref, mod_ref, g_ref, w_in_ref, gq_ref, gkv_ref, wuqt_ref, wuqrt_ref,
                wuk_ref, wuvt_ref, cost_ref, sint_ref,
                qt_ref, k_ref, vt_ref, *, scale, layer):
    x = x_ref[0]
    shift, scl = _mod_rows(mod_ref, (0, 1))
    h = (_rms(x) * g_ref[layer:layer + 1, :]) * (1.0 + scl) + shift
    z = jnp.dot(h.astype(BF16), w_in_ref[0], preferred_element_type=F32)
    c_q = z[:, 0:Q_LORA]
    c_kv = z[:, Q_LORA:Q_LORA + KV_LORA]
    kr = z[:, Q_LORA + KV_LORA:Q_LORA + KV_LORA + HEAD_PAD]
    kr_rot = z[:, Q_LORA + KV_LORA + HEAD_PAD:]
    cqn = (_rms(c_q) * gq_ref[layer:layer + 1, :]).astype(BF16)
    ckvn = (_rms(c_kv) * gkv_ref[layer:layer + 1, :]).astype(BF16)

    qt = _nt_dot(wuqt_ref[0], cqn)
    qt_rot = _nt_dot(wuqrt_ref[0], cqn)
    cost = cost_ref[0]
    sint = sint_ref[0]
    for hd in range(N_HEADS):
        base = hd * HEAD_PAD
        qt_ref[0, base:base + ROPE_LO, :] = (qt[base:base + ROPE_LO] * scale).astype(BF16)
        roped = (qt[base + ROPE_LO:base + ROPE_HI] * cost
                 + qt_rot[hd * QK_ROPE:(hd + 1) * QK_ROPE] * sint)
        qt_ref[0, base + ROPE_LO:base + ROPE_HI, :] = (roped * scale).astype(BF16)
        qt_ref[0, base + ROPE_HI:base + HEAD_PAD, :] = (
            qt[base + ROPE_HI:base + HEAD_PAD] * scale).astype(BF16)

    k = jnp.dot(ckvn, wuk_ref[0], preferred_element_type=F32)
    tm = x.shape[0]
    z_lo = jnp.zeros((ROPE_LO, tm), F32)
    z_hi = jnp.zeros((HEAD_PAD - ROPE_HI, tm), F32)
    cos_tok = jnp.concatenate([z_lo, cost, z_hi], axis=0).T
    sin_tok = jnp.concatenate([z_lo, sint, z_hi], axis=0).T
    kr_full = kr * cos_tok + kr_rot * sin_tok
    for hd in range(N_HEADS):
        base = hd * HEAD_PAD
        k_ref[0, :, base:base + HEAD_PAD] = (k[:, base:base + HEAD_PAD] + kr_full).astype(BF16)

    vt_ref[0] = _nt_dot(wuvt_ref[0], ckvn).astype(BF16)


def _qkv(x, mod, tables, wts, layer):
    b, s, d = x.shape
    tm = TM_QKV
    cost, sint = tables
    scale = math.log2(math.e) / math.sqrt(QK_NOPE + QK_ROPE)
    stacked = [wts["ln1_g"], wts["w_in_a"], wts["q_norm_g"], wts["kv_norm_g"], wts["wuqt"], wts["wuqrt"],
               wts["wuk"], wts["wuvt"]]
    return pl.pallas_call(
        functools.partial(_qkv_kernel, scale=scale, layer=layer),
        grid=(b, s // tm),
        in_specs=[
            pl.BlockSpec((1, tm, d), lambda i, j: (i, j, 0)),
            _layer_spec(mod, layer),
            _whole_spec(wts["ln1_g"]),
            _layer_spec(wts["w_in_a"], layer),
            _whole_spec(wts["q_norm_g"]),
            _whole_spec(wts["kv_norm_g"]),
            _layer_spec(wts["wuqt"], layer),
            _layer_spec(wts["wuqrt"], layer),
            _layer_spec(wts["wuk"], layer),
            _layer_spec(wts["wuvt"], layer),
            pl.BlockSpec((1, QK_ROPE, tm), lambda i, j: (i, 0, j)),
            pl.BlockSpec((1, QK_ROPE, tm), lambda i, j: (i, 0, j)),
        ],
        out_specs=[
            pl.BlockSpec((1, N_HEADS * HEAD_PAD, tm), lambda i, j: (i, 0, j)),
            pl.BlockSpec((1, tm, N_HEADS * HEAD_PAD), lambda i, j: (i, j, 0)),
            pl.BlockSpec((1, ATTN_DIM, tm), lambda i, j: (i, 0, j)),
        ],
        out_shape=[
            jax.ShapeDtypeStruct((b, N_HEADS * HEAD_PAD, s), BF16),
            jax.ShapeDtypeStruct((b, s, N_HEADS * HEAD_PAD), BF16),
            jax.ShapeDtypeStruct((b, ATTN_DIM, s), BF16),
        ],
        compiler_params=pltpu.CompilerParams(
            dimension_semantics=("arbitrary", "arbitrary"),
            vmem_limit_bytes=VMEM_LIMIT_BYTES),
        name="qkv_proj",
    )(x, mod, *stacked, cost, sint)


def _attn_kernel(qt_ref, k_ref, vt_ref, o_ref, s_ref, acc_ref, m_ref, *, n_q):
    n_below = n_q * (n_q - 1) // 2
    hq = TQ // 2
    ones_rows = jnp.ones((ACC_ROWS - V_DIM, TK), BF16)
    half_mask = (lax.broadcasted_iota(jnp.int32, (hq, hq), 1)
                 >= lax.broadcasted_iota(jnp.int32, (hq, hq), 0))

    acc_ref[...] = jnp.zeros(acc_ref.shape, F32)
    m_ref[...] = jnp.full(m_ref.shape, jnp.finfo(F32).min, F32)

    def keys(hd, off, n):
        return k_ref[0, pl.ds(off, n), hd * HEAD_PAD:(hd + 1) * HEAD_PAD]

    def queries(hd, off, n):
        return qt_ref[0, hd * HEAD_PAD:(hd + 1) * HEAD_PAD, pl.ds(off, n)]

    def values(hd, off, n):
        return jnp.concatenate(
            [vt_ref[0, hd * V_DIM:(hd + 1) * V_DIM, pl.ds(off, n)], ones_rows[:, :n]], axis=0)

    def col_max(s):
        return jnp.max(s, axis=0, keepdims=True)

    def produce_below(slot, hd, qi, j):
        s = jnp.dot(keys(hd, pl.multiple_of(j * TK, TK), TK), queries(hd, pl.multiple_of(qi * TQ, TQ), TQ),
                    preferred_element_type=F32)
        s_ref[slot, hd] = s
        return col_max(s)

    def consume_below(slot, hd, qi, j, tile_max):
        m = m_ref[qi, hd]
        m_new = jnp.maximum(m, tile_max)
        p = jnp.exp2(s_ref[slot, hd] - m_new).astype(BF16)
        acc_ref[qi, hd] = (jnp.exp2(m - m_new) * acc_ref[qi, hd]
                           + jnp.dot(values(hd, pl.multiple_of(j * TK, TK), TK), p, preferred_element_type=F32))
        m_ref[qi, hd] = m_new

    def produce_diag(slot, hd, d):
        off = pl.multiple_of(d * TQ, TQ)
        s_ref[slot, hd, 0:hq, :] = jnp.dot(keys(hd, off, hq), queries(hd, off, TQ),
                                           preferred_element_type=F32)
        s_ref[slot, hd, hq:TK, hq:TQ] = jnp.dot(keys(hd, off + hq, hq), queries(hd, off + hq, hq),
                                                preferred_element_type=F32)

    def consume_diag(slot, hd, d):
        off = pl.multiple_of(d * TQ, TQ)
        neg = jnp.finfo(F32).min
        s_tl = jnp.where(half_mask, s_ref[slot, hd, 0:hq, 0:hq], neg)
        s_tr = s_ref[slot, hd, 0:hq, hq:TQ]
        s_br = jnp.where(half_mask, s_ref[slot, hd, hq:TK, hq:TQ], neg)
        m = m_ref[d, hd]
        m_l = jnp.maximum(m[:, 0:hq], col_max(s_tl))
        m_r = jnp.maximum(m[:, hq:TQ], jnp.maximum(col_max(s_tr), col_max(s_br)))
        p_l = jnp.exp2(s_tl - m_l).astype(BF16)
        p_r = jnp.concatenate([jnp.exp2(s_tr - m_r), jnp.exp2(s_br - m_r)], axis=0).astype(BF16)
        acc = acc_ref[d, hd]
        acc_l = (jnp.exp2(m[:, 0:hq] - m_l) * acc[:, 0:hq]
                 + jnp.dot(values(hd, off, hq), p_l, preferred_element_type=F32))
        acc_r = (jnp.exp2(m[:, hq:TQ] - m_r) * acc[:, hq:TQ]
                 + jnp.dot(values(hd, off, TK), p_r, preferred_element_type=F32))
        rows = slice(hd * V_DIM, (hd + 1) * V_DIM)
        o_ref[0, rows, pl.ds(off, hq)] = (
            acc_l[0:V_DIM] * (1.0 / acc_l[V_DIM:V_DIM + 1])).astype(o_ref.dtype)
        o_ref[0, rows, pl.ds(off + hq, hq)] = (
            acc_r[0:V_DIM] * (1.0 / acc_r[V_DIM:V_DIM + 1])).astype(o_ref.dtype)

    heads = range(ATTN_HEADS)

    def next_below(qi, j):
        row_end = j + 1 == qi
        return jnp.where(row_end, qi + 1, qi), jnp.where(row_end, 0, j + 1)

    def below_step(slot, tile, maxes, to_diag=False):
        nxt = next_below(*tile)
        new_maxes = []
        for hd in heads:
            if to_diag:
                produce_diag(1 - slot, hd, 0)
            else:
                new_maxes.append(produce_below(1 - slot, hd, *nxt))
            consume_below(slot, hd, *tile, maxes[hd])
        return nxt, tuple(new_maxes)

    def diag_step(slot, d, produce_next=True):
        for hd in heads:
            if produce_next:
                produce_diag(1 - slot, hd, d + 1)
            consume_diag(slot, hd, d)
        return d + 1

    tile0 = (jnp.int32(1), jnp.int32(0))
    maxes0 = tuple(produce_below(0, hd, *tile0) for hd in heads)

    def below_pair(_, carry):
        tile, maxes = carry
        tile, maxes = below_step(0, tile, maxes)
        return below_step(1, tile, maxes)

    tile, maxes = lax.fori_loop(0, n_below // 2 - 1, below_pair, (tile0, maxes0))
    tile, maxes = below_step(0, tile, maxes)
    below_step(1, tile, maxes, to_diag=True)

    def diag_pair(_, d):
        return diag_step(1, diag_step(0, d))

    d = lax.fori_loop(0, n_q // 2 - 1, diag_pair, jnp.int32(0))
    d = diag_step(0, d)
    diag_step(1, d, produce_next=False)


def _attention(qt, k, vt):
    b, _, s = qt.shape
    hb = ATTN_HEADS
    n_q = s // TQ
    assert n_q % 2 == 0 and (n_q * (n_q - 1) // 2) % 2 == 0, "both tile streams run two tiles per loop trip"
    return pl.pallas_call(
        functools.partial(_attn_kernel, n_q=n_q),
        grid=(b, N_HEADS // hb),
        in_specs=[
            pl.BlockSpec((1, hb * HEAD_PAD, s), lambda i, h: (i, h, 0)),
            pl.BlockSpec((1, s, hb * HEAD_PAD), lambda i, h: (i, 0, h)),
            pl.BlockSpec((1, hb * V_DIM, s), lambda i, h: (i, h, 0)),
        ],
        out_specs=pl.BlockSpec((1, hb * V_DIM, s), lambda i, h: (i, h, 0)),
        out_shape=jax.ShapeDtypeStruct((b, ATTN_DIM, s), BF16),
        scratch_shapes=[
            pltpu.VMEM((2, hb, TK, TQ), F32),
            pltpu.VMEM((n_q, hb, ACC_ROWS, TQ), F32),
            pltpu.VMEM((n_q, hb, 1, TQ), F32),
        ],
        compiler_params=pltpu.CompilerParams(
            dimension_semantics=("arbitrary", "arbitrary"),
            vmem_limit_bytes=VMEM_LIMIT_BYTES),
        name="mla_attention",
    )(qt, k, vt)


def _mix_kernel(x_ref, ot_ref, mod_ref, g1_ref, g2_ref, gf_ref, w_in_ref, w_pool_ref, pscale_ref,
                p_pool_ref, p_attn_ref, w_out_ref, w_ff1_ref, w_ff2_ref,
                o_ref, uext_ref, *, final, layer):
    tm = x_ref.shape[1]
    si = pl.program_id(1)

    @pl.when(si == 0)
    def _():
        uext_ref[0:POOL_HALO, :] = jnp.zeros((POOL_HALO, POOL_DIM), F32)

    @pl.when(si > 0)
    def _():
        uext_ref[0:POOL_HALO, :] = uext_ref[tm:tm + POOL_HALO, :]

    x = x_ref[0]
    shift1, scale1, gate1, shift2, scale2, gate2 = _mod_rows(mod_ref, range(N_MOD))

    y_b = _tn_dot(ot_ref[0], p_attn_ref[0])
    h = ((_rms(x) * g1_ref[layer:layer + 1, :]) * (1.0 + scale1) + shift1).astype(BF16)
    u = jnp.dot(h, w_in_ref[0, :, 0:POOL_DIM], preferred_element_type=F32)
    gz = jnp.dot(h, w_in_ref[0, :, POOL_DIM:], preferred_element_type=F32)
    gz_a = gz[:, 0:D_MODEL]
    gz_b = gz[:, D_MODEL:]

    uext_ref[POOL_HALO:POOL_HALO + tm, :] = u
    head_pos = si * tm + lax.broadcasted_iota(jnp.int32, (POOL_HALO, 1), 0)
    pooled = []
    for g, w in enumerate(POOL_WINDOWS):
        eg = uext_ref[:, g * POOL_GROUP_DIM:(g + 1) * POOL_GROUP_DIM]
        win = eg
        k = 1
        while k < w:
            win = win + pltpu.roll(win, k, axis=0)
            k *= 2
        win = win[POOL_HALO:]
        ug = eg[POOL_HALO:]
        inv_head = 1.0 / jnp.minimum(head_pos + 1, w).astype(F32)
        mean = jnp.concatenate([win[:POOL_HALO] * inv_head, win[POOL_HALO:] * (1.0 / w)], axis=0)
        yg = jnp.dot((mean - ug).astype(BF16), w_pool_ref[0, g], preferred_element_type=F32)
        pooled.append(yg)
    y_pool = jnp.concatenate(pooled, axis=-1) * pscale_ref[layer:layer + 1, :]
    y_a = jnp.dot(y_pool.astype(BF16), p_pool_ref[0], preferred_element_type=F32)

    merged = _sigmoid(gz_a) * y_a + _sigmoid(gz_b) * y_b
    x1 = x + gate1 * jnp.dot(merged.astype(BF16), w_out_ref[0], preferred_element_type=F32)

    h2 = ((_rms(x1) * g2_ref[layer:layer + 1, :]) * (1.0 + scale2) + shift2).astype(BF16)
    hidden = [jnp.square(jnp.maximum(
        jnp.dot(h2, w_ff1_ref[0, :, c0:c0 + FF_CHUNK], preferred_element_type=F32), 0.0)).astype(BF16)
        for c0 in range(0, D_FF, FF_CHUNK)]
    ff = jnp.dot(jnp.concatenate(hidden, axis=-1), w_ff2_ref[0], preferred_element_type=F32)
    x2 = x1 + gate2 * ff
    if final:
        x2 = _rms(x2) * gf_ref[...]
    o_ref[0] = x2


def _mix(x, ot, mod, wts, layer, *, final):
    b, s, d = x.shape
    tm = TM_MIX
    once = dict(pipeline_mode=pl.Buffered(1))
    names = ["ln1_g", "ln2_g", "final_g", "w_in_b", "w_pool", "pool_scale", "p_pool", "p_attn", "w_out",
             "w_ff1", "w_ff2"]
    whole = {"ln1_g", "ln2_g", "final_g", "pool_scale"}
    specs = [_whole_spec(wts[n], **once) if n in whole else _layer_spec(wts[n], layer, **once)
             for n in names]
    return pl.pallas_call(
        functools.partial(_mix_kernel, final=final, layer=layer),
        grid=(b, s // tm),
        in_specs=[
            pl.BlockSpec((1, tm, d), lambda i, j: (i, j, 0)),
            pl.BlockSpec((1, ATTN_DIM, tm), lambda i, j: (i, 0, j)),
            _layer_spec(mod, layer, **once),
        ] + specs,
        out_specs=pl.BlockSpec((1, tm, d), lambda i, j: (i, j, 0)),
        out_shape=jax.ShapeDtypeStruct((b, s, d), F32),
        scratch_shapes=[pltpu.VMEM((tm + POOL_HALO, POOL_DIM), F32)],
        compiler_params=pltpu.CompilerParams(
            dimension_semantics=("arbitrary", "arbitrary"),
            vmem_limit_bytes=VMEM_LIMIT_BYTES),
        name="mix_mlp",
    )(x, ot, mod, *[wts[n] for n in names])


def _prep_weights(ln1_g, ln2_g, w_in, q_norm_g, w_uq, kv_norm_g, w_uk, w_uv, w_pool, pool_scale,
                  p_pool, p_attn, w_out, w_ff1, w_ff2, final_g):
    depth, d, _ = w_in.shape
    c0 = POOL_DIM
    c1 = c0 + Q_LORA
    c2 = c1 + KV_LORA
    c3 = c2 + QK_ROPE
    half = QK_ROPE // 2
    w_kr = w_in[:, :, c2:c3]
    zl = jnp.zeros((depth, d, ROPE_LO), F32)
    zr = jnp.zeros((depth, d, HEAD_PAD - ROPE_HI), F32)
    w_in_a = jnp.concatenate([w_in[:, :, c0:c2], zl, w_kr, zr, zl, -w_kr[:, :, half:], w_kr[:, :, :half], zr],
                             axis=2).astype(BF16)
    w_in_b = jnp.concatenate([w_in[:, :, :c0], w_in[:, :, c3:]], axis=2).astype(BF16)

    pad_q = HEAD_PAD - (QK_NOPE + QK_ROPE)
    wuq_pad = jnp.pad(w_uq, ((0, 0), (0, 0), (0, 0), (0, pad_q)))
    wuqt = wuq_pad.reshape(depth, Q_LORA, N_HEADS * HEAD_PAD).transpose(0, 2, 1).astype(BF16)
    r = w_uq[..., QK_NOPE:]
    rot = jnp.concatenate([-r[..., half:], r[..., :half]], axis=-1)
    wuqrt = rot.reshape(depth, Q_LORA, N_HEADS * QK_ROPE).transpose(0, 2, 1).astype(BF16)
    wuk = jnp.pad(w_uk, ((0, 0), (0, 0), (0, 0), (0, HEAD_PAD - QK_NOPE)))
    wuk = wuk.reshape(depth, KV_LORA, N_HEADS * HEAD_PAD).astype(BF16)
    wuvt = w_uv.reshape(depth, KV_LORA, ATTN_DIM).transpose(0, 2, 1).astype(BF16)
    return dict(
        ln1_g=ln1_g, ln2_g=ln2_g, final_g=final_g.reshape(1, D_MODEL), q_norm_g=q_norm_g,
        kv_norm_g=kv_norm_g, pool_scale=pool_scale, w_in_a=w_in_a, w_in_b=w_in_b, wuqt=wuqt, wuqrt=wuqrt,
        wuk=wuk, wuvt=wuvt, w_pool=w_pool.astype(BF16), p_pool=p_pool.astype(BF16),
        p_attn=p_attn.astype(BF16), w_out=w_out.astype(BF16), w_ff1=w_ff1.astype(BF16),
        w_ff2=w_ff2.astype(BF16))


def kernel(x, c, positions, ln1_g, ln2_g, w_ada, b_ada, w_in, q_norm_g, w_uq, kv_norm_g, w_uk,
           w_uv, w_pool, pool_scale, p_pool, p_attn, w_out, w_ff1, w_ff2, final_g):
    depth = w_in.shape[0]
    tables = _rope_tables(positions)
    mod = _modulation(c, w_ada, b_ada)
    wts = _prep_weights(ln1_g, ln2_g, w_in, q_norm_g, w_uq, kv_norm_g, w_uk, w_uv, w_pool, pool_scale,
                        p_pool, p_attn, w_out, w_ff1, w_ff2, final_g)
    for layer in range(depth):
        qt, k, vt = _qkv(x, mod, tables, wts, layer)
        ot = _attention(qt, k, vt)
        x = _mix(x, ot, mod, wts, layer, final=(layer == depth - 1))
    return x
```

```python
import functools
import math

import jax
import jax.numpy as jnp
from jax import lax
from jax.experimental import pallas as pl
from jax.experimental.pallas import tpu as pltpu

D_MODEL = 1024
N_HEADS = 8
QK_NOPE = 64
QK_ROPE = 32
V_DIM = 64
Q_LORA = 384
KV_LORA = 256
POOL_WINDOWS = (2, 4, 8, 16)
POOL_GROUP_DIM = 128
POOL_DIM = len(POOL_WINDOWS) * POOL_GROUP_DIM
ATTN_DIM = N_HEADS * V_DIM
D_FF = 4 * D_MODEL
N_MOD = 6
EPS = 1e-6
ROPE_THETA = 10000.0

HEAD_PAD = 128
ROPE_LO = QK_NOPE
ROPE_HI = QK_NOPE + QK_ROPE
POOL_HALO = 16

VMEM_LIMIT_BYTES = 56 * 1024 * 1024

F32 = jnp.float32
BF16 = jnp.bfloat16

TM_QKV = 512
TM_MIX = 512
TQ = 512
TK = 512
ATTN_HEADS = 4
BF16_SUBLANES = 16
ACC_ROWS = V_DIM + BF16_SUBLANES
FF_CHUNK = 1024
MOD_TN = 1536


def _nt_dot(a, b):
    return lax.dot_general(a, b, (((1,), (1,)), ((), ())), preferred_element_type=F32)


def _tn_dot(a, b):
    return lax.dot_general(a, b, (((0,), (0,)), ((), ())), preferred_element_type=F32)


def _rms(x):
    return x * lax.rsqrt(jnp.mean(x * x, axis=-1, keepdims=True) + EPS)


def _sigmoid(x):
    return 1.0 / (1.0 + jnp.exp(-x))


def _mod_rows(mod_ref, chunks):
    row = pl.ds(pl.program_id(0), 1)
    return tuple(mod_ref[0, row, c * D_MODEL:(c + 1) * D_MODEL] for c in chunks)


def _layer_spec(arr, layer, **kw):
    tail = (0,) * (arr.ndim - 1)
    return pl.BlockSpec((1,) + arr.shape[1:], lambda i, j: (layer,) + tail, **kw)


def _whole_spec(arr, **kw):
    zeros = (0,) * arr.ndim
    return pl.BlockSpec(arr.shape, lambda i, j: zeros, **kw)


def _rope_tables_kernel(pos_ref, invf_ref, cost_ref, sint_ref):
    ang = invf_ref[...] * pos_ref[0].astype(F32)
    cost_ref[0] = jnp.cos(ang)
    sint_ref[0] = jnp.sin(ang)


def _rope_tables(positions):
    b, s = positions.shape
    inv_freq = ROPE_THETA ** (-jnp.arange(0, QK_ROPE, 2, dtype=F32) / QK_ROPE)
    invf_col = jnp.concatenate([inv_freq, inv_freq]).reshape(QK_ROPE, 1)
    return pl.pallas_call(
        _rope_tables_kernel,
        grid=(b,),
        in_specs=[
            pl.BlockSpec((1, 1, s), lambda i: (i, 0, 0)),
            pl.BlockSpec((QK_ROPE, 1), lambda i: (0, 0)),
        ],
        out_specs=[
            pl.BlockSpec((1, QK_ROPE, s), lambda i: (i, 0, 0)),
            pl.BlockSpec((1, QK_ROPE, s), lambda i: (i, 0, 0)),
        ],
        out_shape=[
            jax.ShapeDtypeStruct((b, QK_ROPE, s), F32),
            jax.ShapeDtypeStruct((b, QK_ROPE, s), F32),
        ],
        compiler_params=pltpu.CompilerParams(vmem_limit_bytes=VMEM_LIMIT_BYTES),
        name="rope_tables",
    )(positions.reshape(b, 1, s), invf_col)


def _mod_kernel(c_ref, w_ref, b_ref, o_ref):
    c = c_ref[...]
    c_act = c * _sigmoid(c)
    o_ref[0] = jnp.dot(c_act.astype(BF16), w_ref[0].astype(BF16),
                       preferred_element_type=F32) + b_ref[0]


def _modulation(c, w_ada, b_ada):
    depth, d, n = w_ada.shape
    b = c.shape[0]
    rows = 8
    c_pad = jnp.pad(c, ((0, rows - b), (0, 0)))
    out = pl.pallas_call(
        _mod_kernel,
        grid=(depth, n // MOD_TN),
        in_specs=[
            pl.BlockSpec((rows, d), lambda l, j: (0, 0)),
            pl.BlockSpec((1, d, MOD_TN), lambda l, j: (l, 0, j)),
            pl.BlockSpec((1, 1, MOD_TN), lambda l, j: (l, 0, j)),
        ],
        out_specs=pl.BlockSpec((1, rows, MOD_TN), lambda l, j: (l, 0, j)),
        out_shape=jax.ShapeDtypeStruct((depth, rows, n), F32),
        compiler_params=pltpu.CompilerParams(vmem_limit_bytes=VMEM_LIMIT_BYTES),
        name="adaln_mod",
    )(c_pad, w_ada, b_ada.reshape(depth, 1, n))
    return out


def _qkv_kernel(x_ref, mod_ref, g_ref, w_in_ref, gq_ref, gkv_ref, wuqt_ref, wuqrt_ref,
                wuk_ref, wuvt_ref, cost_ref, sint_ref,
                qt_ref, k_ref, vt_ref, *, scale, layer):
    x = x_ref[0]
    shift, scl = _mod_rows(mod_ref, (0, 1))
    h = (_rms(x) * g_ref[layer:layer + 1, :]) * (1.0 + scl) + shift
    z = jnp.dot(h.astype(BF16), w_in_ref[0], preferred_element_type=F32)
    c_q = z[:, 0:Q_LORA]
    c_kv = z[:, Q_LORA:Q_LORA + KV_LORA]
    kr = z[:, Q_LORA + KV_LORA:Q_LORA + KV_LORA + HEAD_PAD]
    kr_rot = z[:, Q_LORA + KV_LORA + HEAD_PAD:]
    cqn = (_rms(c_q) * gq_ref[layer:layer + 1, :]).astype(BF16)
    ckvn = (_rms(c_kv) * gkv_ref[layer:layer + 1, :]).astype(BF16)

    qt = _nt_dot(wuqt_ref[0], cqn)
    qt_rot = _nt_dot(wuqrt_ref[0], cqn)
    cost = cost_ref[0]
    sint = sint_ref[0]
    for hd in range(N_HEADS):
        base = hd * HEAD_PAD
        qt_ref[0, base:base + ROPE_LO, :] = (qt[base:base + ROPE_LO] * scale).astype(BF16)
        roped = (qt[base + ROPE_LO:base + ROPE_HI] * cost
                 + qt_rot[hd * QK_ROPE:(hd + 1) * QK_ROPE] * sint)
        qt_ref[0, base + ROPE_LO:base + ROPE_HI, :] = (roped * scale).astype(BF16)
        qt_ref[0, base + ROPE_HI:base + HEAD_PAD, :] = (
            qt[base + ROPE_HI:base + HEAD_PAD] * scale).astype(BF16)

    k = jnp.dot(ckvn, wuk_ref[0], preferred_element_type=F32)
    tm = x.shape[0]
    z_lo = jnp.zeros((ROPE_LO, tm), F32)
    z_hi = jnp.zeros((HEAD_PAD - ROPE_HI, tm), F32)
    cos_tok = jnp.concatenate([z_lo, cost, z_hi], axis=0).T
    sin_tok = jnp.concatenate([z_lo, sint, z_hi], axis=0).T
    kr_full = kr * cos_tok + kr_rot * sin_tok
    for hd in range(N_HEADS):
        base = hd * HEAD_PAD
        k_ref[0, :, base:base + HEAD_PAD] = (k[:, base:base + HEAD_PAD] + kr_full).astype(BF16)

    vt_ref[0] = _nt_dot(wuvt_ref[0], ckvn).astype(BF16)


def _qkv(x, mod, tables, wts, layer):
    b, s, d = x.shape
    tm = TM_QKV
    cost, sint = tables
    scale = math.log2(math.e) / math.sqrt(QK_NOPE + QK_ROPE)
    stacked = [wts["ln1_g"], wts["w_in_a"], wts["q_norm_g"], wts["kv_norm_g"], wts["wuqt"], wts["wuqrt"],
               wts["wuk"], wts["wuvt"]]
    return pl.pallas_call(
        functools.partial(_qkv_kernel, scale=scale, layer=layer),
        grid=(b, s // tm),
        in_specs=[
            pl.BlockSpec((1, tm, d), lambda i, j: (i, j, 0)),
            _layer_spec(mod, layer),
            _whole_spec(wts["ln1_g"]),
            _layer_spec(wts["w_in_a"], layer),
            _whole_spec(wts["q_norm_g"]),
            _whole_spec(wts["kv_norm_g"]),
            _layer_spec(wts["wuqt"], layer),
            _layer_spec(wts["wuqrt"], layer),
            _layer_spec(wts["wuk"], layer),
            _layer_spec(wts["wuvt"], layer),
            pl.BlockSpec((1, QK_ROPE, tm), lambda i, j: (i, 0, j)),
            pl.BlockSpec((1, QK_ROPE, tm), lambda i, j: (i, 0, j)),
        ],
        out_specs=[
            pl.BlockSpec((1, N_HEADS * HEAD_PAD, tm), lambda i, j: (i, 0, j)),
            pl.BlockSpec((1, tm, N_HEADS * HEAD_PAD), lambda i, j: (i, j, 0)),
            pl.BlockSpec((1, ATTN_DIM, tm), lambda i, j: (i, 0, j)),
        ],
        out_shape=[
            jax.ShapeDtypeStruct((b, N_HEADS * HEAD_PAD, s), BF16),
            jax.ShapeDtypeStruct((b, s, N_HEADS * HEAD_PAD), BF16),
            jax.ShapeDtypeStruct((b, ATTN_DIM, s), BF16),
        ],
        compiler_params=pltpu.CompilerParams(
            dimension_semantics=("arbitrary", "arbitrary"),
            vmem_limit_bytes=VMEM_LIMIT_BYTES),
        name="qkv_proj",
    )(x, mod, *stacked, cost, sint)


def _attn_kernel(qt_ref, k_ref, vt_ref, *refs, n_q, n_cast):
    cast_in, o_ref, cast_out = refs[:n_cast], refs[n_cast], refs[n_cast + 1:2 * n_cast + 1]
    s_ref, acc_ref, m_ref = refs[2 * n_cast + 1:]
    for src, dst in zip(cast_in, cast_out):
        dst[...] = src[...].astype(BF16)

    n_below = n_q * (n_q - 1) // 2
    hq = TQ // 2
    ones_rows = jnp.ones((ACC_ROWS - V_DIM, TK), BF16)
    half_mask = (lax.broadcasted_iota(jnp.int32, (hq, hq), 1)
                 >= lax.broadcasted_iota(jnp.int32, (hq, hq), 0))

    acc_ref[...] = jnp.zeros(acc_ref.shape, F32)
    m_ref[...] = jnp.full(m_ref.shape, jnp.finfo(F32).min, F32)

    def keys(hd, off, n):
        return k_ref[0, pl.ds(off, n), hd * HEAD_PAD:(hd + 1) * HEAD_PAD]

    def queries(hd, off, n):
        return qt_ref[0, hd * HEAD_PAD:(hd + 1) * HEAD_PAD, pl.ds(off, n)]

    def values(hd, off, n):
        return jnp.concatenate(
            [vt_ref[0, hd * V_DIM:(hd + 1) * V_DIM, pl.ds(off, n)], ones_rows[:, :n]], axis=0)

    def col_max(s):
        return jnp.max(s, axis=0, keepdims=True)

    def produce_below(slot, hd, qi, j):
        s = jnp.dot(keys(hd, pl.multiple_of(j * TK, TK), TK), queries(hd, pl.multiple_of(qi * TQ, TQ), TQ),
                    preferred_element_type=F32)
        s_ref[slot, hd] = s
        return col_max(s)

    def consume_below(slot, hd, qi, j, tile_max):
        m = m_ref[qi, hd]
        m_new = jnp.maximum(m, tile_max)
        p = jnp.exp2(s_ref[slot, hd] - m_new).astype(BF16)
        acc_ref[qi, hd] = (jnp.exp2(m - m_new) * acc_ref[qi, hd]
                           + jnp.dot(values(hd, pl.multiple_of(j * TK, TK), TK), p, preferred_element_type=F32))
        m_ref[qi, hd] = m_new

    def produce_diag(slot, hd, d):
        off = pl.multiple_of(d * TQ, TQ)
        s_ref[slot, hd, 0:hq, :] = jnp.dot(keys(hd, off, hq), queries(hd, off, TQ),
                                           preferred_element_type=F32)
        s_ref[slot, hd, hq:TK, hq:TQ] = jnp.dot(keys(hd, off + hq, hq), queries(hd, off + hq, hq),
                                                preferred_element_type=F32)

    def consume_diag(slot, hd, d):
        off = pl.multiple_of(d * TQ, TQ)
        neg = jnp.finfo(F32).min
        s_tl = jnp.where(half_mask, s_ref[slot, hd, 0:hq, 0:hq], neg)
        s_tr = s_ref[slot, hd, 0:hq, hq:TQ]
        s_br = jnp.where(half_mask, s_ref[slot, hd, hq:TK, hq:TQ], neg)
        m = m_ref[d, hd]
        m_l = jnp.maximum(m[:, 0:hq], col_max(s_tl))
        m_r = jnp.maximum(m[:, hq:TQ], jnp.maximum(col_max(s_tr), col_max(s_br)))
        p_l = jnp.exp2(s_tl - m_l).astype(BF16)
        p_r = jnp.concatenate([jnp.exp2(s_tr - m_r), jnp.exp2(s_br - m_r)], axis=0).astype(BF16)
        acc = acc_ref[d, hd]
        acc_l = (jnp.exp2(m[:, 0:hq] - m_l) * acc[:, 0:hq]
                 + jnp.dot(values(hd, off, hq), p_l, preferred_element_type=F32))
        acc_r = (jnp.exp2(m[:, hq:TQ] - m_r) * acc[:, hq:TQ]
                 + jnp.dot(values(hd, off, TK), p_r, preferred_element_type=F32))
        rows = slice(hd * V_DIM, (hd + 1) * V_DIM)
        o_ref[0, rows, pl.ds(off, hq)] = (
            acc_l[0:V_DIM] * (1.0 / acc_l[V_DIM:V_DIM + 1])).astype(o_ref.dtype)
        o_ref[0, rows, pl.ds(off + hq, hq)] = (
            acc_r[0:V_DIM] * (1.0 / acc_r[V_DIM:V_DIM + 1])).astype(o_ref.dtype)

    heads = range(ATTN_HEADS)

    def next_below(qi, j):
        row_end = j + 1 == qi
        return jnp.where(row_end, qi + 1, qi), jnp.where(row_end, 0, j + 1)

    def below_step(slot, tile, maxes, to_diag=False):
        nxt = next_below(*tile)
        new_maxes = []
        for hd in heads:
            if to_diag:
                produce_diag(1 - slot, hd, 0)
            else:
                new_maxes.append(produce_below(1 - slot, hd, *nxt))
            consume_below(slot, hd, *tile, maxes[hd])
        return nxt, tuple(new_maxes)

    def diag_step(slot, d, produce_next=True):
        for hd in heads:
            if produce_next:
                produce_diag(1 - slot, hd, d + 1)
            consume_diag(slot, hd, d)
        return d + 1

    tile0 = (jnp.int32(1), jnp.int32(0))
    maxes0 = tuple(produce_below(0, hd, *tile0) for hd in heads)

    def below_pair(_, carry):
        tile, maxes = carry
        tile, maxes = below_step(0, tile, maxes)
        return below_step(1, tile, maxes)

    tile, maxes = lax.fori_loop(0, n_below // 2 - 1, below_pair, (tile0, maxes0))
    tile, maxes = below_step(0, tile, maxes)
    below_step(1, tile, maxes, to_diag=True)

    def diag_pair(_, d):
        return diag_step(1, diag_step(0, d))

    d = lax.fori_loop(0, n_q // 2 - 1, diag_pair, jnp.int32(0))
    d = diag_step(0, d)
    diag_step(1, d, produce_next=False)


def _attention(qt, k, vt, f32_weights, layer):
    b, _, s = qt.shape
    hb = ATTN_HEADS
    groups = N_HEADS // hb
    n_steps = b * groups
    n_q = s // TQ
    assert n_q % 2 == 0 and (n_q * (n_q - 1) // 2) % 2 == 0, "both tile streams run two tiles per loop trip"

    def rows_per_step(w):
        rows = w.shape[1] // n_steps
        assert rows * n_steps == w.shape[1] and rows % BF16_SUBLANES == 0, w.shape
        return rows

    cast_in = [pl.BlockSpec((1, rows_per_step(w), w.shape[2]), lambda i, h: (layer, i * groups + h, 0))
               for w in f32_weights]
    cast_out = [pl.BlockSpec((1, rows_per_step(w), w.shape[2]), lambda i, h: (0, i * groups + h, 0))
                for w in f32_weights]
    out = pl.pallas_call(
        functools.partial(_attn_kernel, n_q=n_q, n_cast=len(f32_weights)),
        grid=(b, groups),
        in_specs=[
            pl.BlockSpec((1, hb * HEAD_PAD, s), lambda i, h: (i, h, 0)),
            pl.BlockSpec((1, s, hb * HEAD_PAD), lambda i, h: (i, 0, h)),
            pl.BlockSpec((1, hb * V_DIM, s), lambda i, h: (i, h, 0)),
        ] + cast_in,
        out_specs=[pl.BlockSpec((1, hb * V_DIM, s), lambda i, h: (i, h, 0))] + cast_out,
        out_shape=[jax.ShapeDtypeStruct((b, ATTN_DIM, s), BF16)]
        + [jax.ShapeDtypeStruct((1,) + w.shape[1:], BF16) for w in f32_weights],
        scratch_shapes=[
            pltpu.VMEM((2, hb, TK, TQ), F32),
            pltpu.VMEM((n_q, hb, ACC_ROWS, TQ), F32),
            pltpu.VMEM((n_q, hb, 1, TQ), F32),
        ],
        compiler_params=pltpu.CompilerParams(
            dimension_semantics=("arbitrary", "arbitrary"),
            vmem_limit_bytes=VMEM_LIMIT_BYTES),
        name="mla_attention",
    )(qt, k, vt, *f32_weights)
    return out[0], out[1:]


def _mix_kernel(x_ref, ot_ref, mod_ref, g1_ref, g2_ref, gf_ref, w_in_ref, w_pool_ref, pscale_ref,
                p_pool_ref, p_attn_ref, w_out_ref, w_ff1_ref, w_ff2_ref,
                o_ref, uext_ref, *, final, layer):
    tm = x_ref.shape[1]
    si = pl.program_id(1)

    @pl.when(si == 0)
    def _():
        uext_ref[0:POOL_HALO, :] = jnp.zeros((POOL_HALO, POOL_DIM), F32)

    @pl.when(si > 0)
    def _():
        uext_ref[0:POOL_HALO, :] = uext_ref[tm:tm + POOL_HALO, :]

    x = x_ref[0]
    shift1, scale1, gate1, shift2, scale2, gate2 = _mod_rows(mod_ref, range(N_MOD))

    y_b = _tn_dot(ot_ref[0], p_attn_ref[0])
    h = ((_rms(x) * g1_ref[layer:layer + 1, :]) * (1.0 + scale1) + shift1).astype(BF16)
    u = jnp.dot(h, w_in_ref[0, :, 0:POOL_DIM], preferred_element_type=F32)
    gz = jnp.dot(h, w_in_ref[0, :, POOL_DIM:], preferred_element_type=F32)
    gz_a = gz[:, 0:D_MODEL]
    gz_b = gz[:, D_MODEL:]

    uext_ref[POOL_HALO:POOL_HALO + tm, :] = u
    head_pos = si * tm + lax.broadcasted_iota(jnp.int32, (POOL_HALO, 1), 0)
    pooled = []
    for g, w in enumerate(POOL_WINDOWS):
        eg = uext_ref[:, g * POOL_GROUP_DIM:(g + 1) * POOL_GROUP_DIM]
        win = eg
        k = 1
        while k < w:
            win = win + pltpu.roll(win, k, axis=0)
            k *= 2
        win = win[POOL_HALO:]
        ug = eg[POOL_HALO:]
        inv_head = 1.0 / jnp.minimum(head_pos + 1, w).astype(F32)
        mean = jnp.concatenate([win[:POOL_HALO] * inv_head, win[POOL_HALO:] * (1.0 / w)], axis=0)
        yg = jnp.dot((mean - ug).astype(BF16), w_pool_ref[0, g], preferred_element_type=F32)
        pooled.append(yg)
    y_pool = jnp.concatenate(pooled, axis=-1) * pscale_ref[layer:layer + 1, :]
    y_a = jnp.dot(y_pool.astype(BF16), p_pool_ref[0], preferred_element_type=F32)

    merged = _sigmoid(gz_a) * y_a + _sigmoid(gz_b) * y_b
    x1 = x + gate1 * jnp.dot(merged.astype(BF16), w_out_ref[0], preferred_element_type=F32)

    h2 = ((_rms(x1) * g2_ref[layer:layer + 1, :]) * (1.0 + scale2) + shift2).astype(BF16)
    hidden = [jnp.square(jnp.maximum(
        jnp.dot(h2, w_ff1_ref[0, :, c0:c0 + FF_CHUNK], preferred_element_type=F32), 0.0)).astype(BF16)
        for c0 in range(0, D_FF, FF_CHUNK)]
    ff = jnp.dot(jnp.concatenate(hidden, axis=-1), w_ff2_ref[0], preferred_element_type=F32)
    x2 = x1 + gate2 * ff
    if final:
        x2 = _rms(x2) * gf_ref[...]
    o_ref[0] = x2


def _mix(x, ot, mod, wts, layer_wts, layer, *, final):
    b, s, d = x.shape
    tm = TM_MIX
    once = dict(pipeline_mode=pl.Buffered(1))
    names = ["ln1_g", "ln2_g", "final_g", "w_in_b", "w_pool", "pool_scale", "p_pool", "p_attn", "w_out",
             "w_ff1", "w_ff2"]
    whole = {"ln1_g", "ln2_g", "final_g", "pool_scale"}
    wts = {**wts, **layer_wts}
    specs = [_whole_spec(wts[n], **once) if n in whole
             else _layer_spec(wts[n], 0 if n in layer_wts else layer, **once)
             for n in names]
    return pl.pallas_call(
        functools.partial(_mix_kernel, final=final, layer=layer),
        grid=(b, s // tm),
        in_specs=[
            pl.BlockSpec((1, tm, d), lambda i, j: (i, j, 0)),
            pl.BlockSpec((1, ATTN_DIM, tm), lambda i, j: (i, 0, j)),
            _layer_spec(mod, layer, **once),
        ] + specs,
        out_specs=pl.BlockSpec((1, tm, d), lambda i, j: (i, j, 0)),
        out_shape=jax.ShapeDtypeStruct((b, s, d), F32),
        scratch_shapes=[pltpu.VMEM((tm + POOL_HALO, POOL_DIM), F32)],
        compiler_params=pltpu.CompilerParams(
            dimension_semantics=("arbitrary", "arbitrary"),
            vmem_limit_bytes=VMEM_LIMIT_BYTES),
        name="mix_mlp",
    )(x, ot, mod, *[wts[n] for n in names])


def _prep_weights(ln1_g, ln2_g, w_in, q_norm_g, w_uq, kv_norm_g, w_uk, w_uv, w_pool, pool_scale, final_g):
    depth, d, _ = w_in.shape
    c0 = POOL_DIM
    c1 = c0 + Q_LORA
    c2 = c1 + KV_LORA
    c3 = c2 + QK_ROPE
    half = QK_ROPE // 2
    w_kr = w_in[:, :, c2:c3]
    zl = jnp.zeros((depth, d, ROPE_LO), F32)
    zr = jnp.zeros((depth, d, HEAD_PAD - ROPE_HI), F32)
    w_in_a = jnp.concatenate([w_in[:, :, c0:c2], zl, w_kr, zr, zl, -w_kr[:, :, half:], w_kr[:, :, :half], zr],
                             axis=2).astype(BF16)
    w_in_b = jnp.concatenate([w_in[:, :, :c0], w_in[:, :, c3:]], axis=2).astype(BF16)

    pad_q = HEAD_PAD - (QK_NOPE + QK_ROPE)
    wuq_pad = jnp.pad(w_uq, ((0, 0), (0, 0), (0, 0), (0, pad_q)))
    wuqt = wuq_pad.reshape(depth, Q_LORA, N_HEADS * HEAD_PAD).transpose(0, 2, 1).astype(BF16)
    r = w_uq[..., QK_NOPE:]
    rot = jnp.concatenate([-r[..., half:], r[..., :half]], axis=-1)
    wuqrt = rot.reshape(depth, Q_LORA, N_HEADS * QK_ROPE).transpose(0, 2, 1).astype(BF16)
    wuk = jnp.pad(w_uk, ((0, 0), (0, 0), (0, 0), (0, HEAD_PAD - QK_NOPE)))
    wuk = wuk.reshape(depth, KV_LORA, N_HEADS * HEAD_PAD).astype(BF16)
    wuvt = w_uv.reshape(depth, KV_LORA, ATTN_DIM).transpose(0, 2, 1).astype(BF16)
    return dict(
        ln1_g=ln1_g, ln2_g=ln2_g, final_g=final_g.reshape(1, D_MODEL), q_norm_g=q_norm_g,
        kv_norm_g=kv_norm_g, pool_scale=pool_scale, w_in_a=w_in_a, w_in_b=w_in_b, wuqt=wuqt, wuqrt=wuqrt,
        wuk=wuk, wuvt=wuvt, w_pool=w_pool.astype(BF16))


def kernel(x, c, positions, ln1_g, ln2_g, w_ada, b_ada, w_in, q_norm_g, w_uq, kv_norm_g, w_uk,
           w_uv, w_pool, pool_scale, p_pool, p_attn, w_out, w_ff1, w_ff2, final_g):
    depth = w_in.shape[0]
    tables = _rope_tables(positions)
    mod = _modulation(c, w_ada, b_ada)
    wts = _prep_weights(ln1_g, ln2_g, w_in, q_norm_g, w_uq, kv_norm_g, w_uk, w_uv, w_pool, pool_scale,
                        final_g)
    cast_names = ("p_pool", "p_attn", "w_out", "w_ff1", "w_ff2")
    f32_weights = (p_pool, p_attn, w_out, w_ff1, w_ff2)
    for layer in range(depth):
        qt, k, vt = _qkv(x, mod, tables, wts, layer)
        ot, cast = _attention(qt, k, vt, f32_weights, layer)
        x = _mix(x, ot, mod, wts, dict(zip(cast_names, cast)), layer, final=(layer == depth - 1))
    return x
```

```python
import functools
import math

import jax
import jax.numpy as jnp
from jax import lax
from jax.experimental import pallas as pl
from jax.experimental.pallas import tpu as pltpu

D_MODEL = 1024
N_HEADS = 8
QK_NOPE = 64
QK_ROPE = 32
V_DIM = 64
Q_LORA = 384
KV_LORA = 256
POOL_WINDOWS = (2, 4, 8, 16)
POOL_GROUP_DIM = 128
POOL_DIM = len(POOL_WINDOWS) * POOL_GROUP_DIM
ATTN_DIM = N_HEADS * V_DIM
D_FF = 4 * D_MODEL
N_MOD = 6
EPS = 1e-6
ROPE_THETA = 10000.0

HEAD_PAD = 128
ROPE_LO = QK_NOPE
ROPE_HI = QK_NOPE + QK_ROPE
POOL_HALO = 16

VMEM_LIMIT_BYTES = 56 * 1024 * 1024

F32 = jnp.float32
BF16 = jnp.bfloat16

TM_QKV = 1024
TM_MIX = 512
TQ = 512
TK = 512
ATTN_HEADS = 4
BF16_SUBLANES = 16
ACC_ROWS = V_DIM + BF16_SUBLANES
FF_CHUNK = 1024
MOD_TN = 1536


def _nt_dot(a, b):
    return lax.dot_general(a, b, (((1,), (1,)), ((), ())), preferred_element_type=F32)


def _tn_dot(a, b):
    return lax.dot_general(a, b, (((0,), (0,)), ((), ())), preferred_element_type=F32)


def _rms(x):
    return x * lax.rsqrt(jnp.mean(x * x, axis=-1, keepdims=True) + EPS)


def _sigmoid(x):
    return 1.0 / (1.0 + jnp.exp(-x))


def _mod_rows(mod_ref, chunks):
    row = pl.ds(pl.program_id(0), 1)
    return tuple(mod_ref[0, row, c * D_MODEL:(c + 1) * D_MODEL] for c in chunks)


def _layer_spec(arr, layer, **kw):
    tail = (0,) * (arr.ndim - 1)
    return pl.BlockSpec((1,) + arr.shape[1:], lambda i, j: (layer,) + tail, **kw)


def _whole_spec(arr, **kw):
    zeros = (0,) * arr.ndim
    return pl.BlockSpec(arr.shape, lambda i, j: zeros, **kw)


def _mod_kernel(c_ref, w_ref, b_ref, o_ref):
    c = c_ref[...]
    c_act = c * _sigmoid(c)
    o_ref[0] = jnp.dot(c_act.astype(BF16), w_ref[0].astype(BF16),
                       preferred_element_type=F32) + b_ref[0]


def _modulation(c, w_ada, b_ada):
    depth, d, n = w_ada.shape
    b = c.shape[0]
    rows = 8
    c_pad = jnp.pad(c, ((0, rows - b), (0, 0)))
    out = pl.pallas_call(
        _mod_kernel,
        grid=(depth, n // MOD_TN),
        in_specs=[
            pl.BlockSpec((rows, d), lambda l, j: (0, 0)),
            pl.BlockSpec((1, d, MOD_TN), lambda l, j: (l, 0, j)),
            pl.BlockSpec((1, 1, MOD_TN), lambda l, j: (l, 0, j)),
        ],
        out_specs=pl.BlockSpec((1, rows, MOD_TN), lambda l, j: (l, 0, j)),
        out_shape=jax.ShapeDtypeStruct((depth, rows, n), F32),
        compiler_params=pltpu.CompilerParams(vmem_limit_bytes=VMEM_LIMIT_BYTES),
        name="adaln_mod",
    )(c_pad, w_ada, b_ada.reshape(depth, 1, n))
    return out


def _qkv_kernel(x_ref, mod_ref, g_ref, w_in_ref, gq_ref, gkv_ref, wuqt_ref, wuqrt_ref,
                wuk_ref, wuvt_ref, pos_ref, invf_ref,
                qt_ref, k_ref, vt_ref, *, scale, layer):
    x = x_ref[0]
    shift, scl = _mod_rows(mod_ref, (0, 1))
    h = (_rms(x) * g_ref[layer:layer + 1, :]) * (1.0 + scl) + shift
    z = jnp.dot(h.astype(BF16), w_in_ref[0], preferred_element_type=F32)
    c_q = z[:, 0:Q_LORA]
    c_kv = z[:, Q_LORA:Q_LORA + KV_LORA]
    kr = z[:, Q_LORA + KV_LORA:Q_LORA + KV_LORA + HEAD_PAD]
    kr_rot = z[:, Q_LORA + KV_LORA + HEAD_PAD:]
    cqn = (_rms(c_q) * gq_ref[layer:layer + 1, :]).astype(BF16)
    ckvn = (_rms(c_kv) * gkv_ref[layer:layer + 1, :]).astype(BF16)

    qt = _nt_dot(wuqt_ref[0], cqn)
    qt_rot = _nt_dot(wuqrt_ref[0], cqn)
    ang = invf_ref[...] * pos_ref[0].astype(F32)
    cost = jnp.cos(ang)
    sint = jnp.sin(ang)
    for hd in range(N_HEADS):
        base = hd * HEAD_PAD
        qt_ref[0, base:base + ROPE_LO, :] = (qt[base:base + ROPE_LO] * scale).astype(BF16)
        roped = (qt[base + ROPE_LO:base + ROPE_HI] * cost
                 + qt_rot[hd * QK_ROPE:(hd + 1) * QK_ROPE] * sint)
        qt_ref[0, base + ROPE_LO:base + ROPE_HI, :] = (roped * scale).astype(BF16)
        qt_ref[0, base + ROPE_HI:base + HEAD_PAD, :] = (
            qt[base + ROPE_HI:base + HEAD_PAD] * scale).astype(BF16)

    k = jnp.dot(ckvn, wuk_ref[0], preferred_element_type=F32)
    tm = x.shape[0]
    z_lo = jnp.zeros((ROPE_LO, tm), F32)
    z_hi = jnp.zeros((HEAD_PAD - ROPE_HI, tm), F32)
    cos_tok = jnp.concatenate([z_lo, cost, z_hi], axis=0).T
    sin_tok = jnp.concatenate([z_lo, sint, z_hi], axis=0).T
    kr_full = kr * cos_tok + kr_rot * sin_tok
    for hd in range(N_HEADS):
        base = hd * HEAD_PAD
        k_ref[0, :, base:base + HEAD_PAD] = (k[:, base:base + HEAD_PAD] + kr_full).astype(BF16)

    vt_ref[0] = _nt_dot(wuvt_ref[0], ckvn).astype(BF16)


def _qkv(x, mod, positions, wts, layer):
    b, s, d = x.shape
    tm = TM_QKV
    inv_freq = ROPE_THETA ** (-jnp.arange(0, QK_ROPE, 2, dtype=F32) / QK_ROPE)
    invf_col = jnp.concatenate([inv_freq, inv_freq]).reshape(QK_ROPE, 1)
    scale = math.log2(math.e) / math.sqrt(QK_NOPE + QK_ROPE)
    stacked = [wts["ln1_g"], wts["w_in_a"], wts["q_norm_g"], wts["kv_norm_g"], wts["wuqt"], wts["wuqrt"],
               wts["wuk"], wts["wuvt"]]
    return pl.pallas_call(
        functools.partial(_qkv_kernel, scale=scale, layer=layer),
        grid=(b, s // tm),
        in_specs=[
            pl.BlockSpec((1, tm, d), lambda i, j: (i, j, 0)),
            _layer_spec(mod, layer),
            _whole_spec(wts["ln1_g"]),
            _layer_spec(wts["w_in_a"], layer),
            _whole_spec(wts["q_norm_g"]),
            _whole_spec(wts["kv_norm_g"]),
            _layer_spec(wts["wuqt"], layer),
            _layer_spec(wts["wuqrt"], layer),
            _layer_spec(wts["wuk"], layer),
            _layer_spec(wts["wuvt"], layer),
            pl.BlockSpec((1, 1, tm), lambda i, j: (i, 0, j)),
            _whole_spec(invf_col),
        ],
        out_specs=[
            pl.BlockSpec((1, N_HEADS * HEAD_PAD, tm), lambda i, j: (i, 0, j)),
            pl.BlockSpec((1, tm, N_HEADS * HEAD_PAD), lambda i, j: (i, j, 0)),
            pl.BlockSpec((1, ATTN_DIM, tm), lambda i, j: (i, 0, j)),
        ],
        out_shape=[
            jax.ShapeDtypeStruct((b, N_HEADS * HEAD_PAD, s), BF16),
            jax.ShapeDtypeStruct((b, s, N_HEADS * HEAD_PAD), BF16),
            jax.ShapeDtypeStruct((b, ATTN_DIM, s), BF16),
        ],
        compiler_params=pltpu.CompilerParams(
            dimension_semantics=("arbitrary", "arbitrary"),
            vmem_limit_bytes=VMEM_LIMIT_BYTES),
        name="qkv_proj",
    )(x, mod, *stacked, positions.reshape(b, 1, s), invf_col)


def _attn_kernel(qt_ref, k_ref, vt_ref, *refs, n_q, n_cast):
    cast_in, o_ref, cast_out = refs[:n_cast], refs[n_cast], refs[n_cast + 1:2 * n_cast + 1]
    s_ref, acc_ref, m_ref = refs[2 * n_cast + 1:]
    for src, dst in zip(cast_in, cast_out):
        dst[...] = src[...].astype(BF16)

    n_below = n_q * (n_q - 1) // 2
    hq = TQ // 2
    ones_rows = jnp.ones((ACC_ROWS - V_DIM, TK), BF16)
    half_mask = (lax.broadcasted_iota(jnp.int32, (hq, hq), 1)
                 >= lax.broadcasted_iota(jnp.int32, (hq, hq), 0))

    acc_ref[...] = jnp.zeros(acc_ref.shape, F32)
    m_ref[...] = jnp.full(m_ref.shape, jnp.finfo(F32).min, F32)

    def keys(hd, off, n):
        return k_ref[0, pl.ds(off, n), hd * HEAD_PAD:(hd + 1) * HEAD_PAD]

    def queries(hd, off, n):
        return qt_ref[0, hd * HEAD_PAD:(hd + 1) * HEAD_PAD, pl.ds(off, n)]

    def values(hd, off, n):
        return jnp.concatenate(
            [vt_ref[0, hd * V_DIM:(hd + 1) * V_DIM, pl.ds(off, n)], ones_rows[:, :n]], axis=0)

    def col_max(s):
        return jnp.max(s, axis=0, keepdims=True)

    def produce_below(slot, hd, qi, j):
        s = jnp.dot(keys(hd, pl.multiple_of(j * TK, TK), TK), queries(hd, pl.multiple_of(qi * TQ, TQ), TQ),
                    preferred_element_type=F32)
        s_ref[slot, hd] = s
        return col_max(s)

    def consume_below(slot, hd, qi, j, tile_max):
        m = m_ref[qi, hd]
        m_new = jnp.maximum(m, tile_max)
        p = jnp.exp2(s_ref[slot, hd] - m_new).astype(BF16)
        acc_ref[qi, hd] = (jnp.exp2(m - m_new) * acc_ref[qi, hd]
                           + jnp.dot(values(hd, pl.multiple_of(j * TK, TK), TK), p, preferred_element_type=F32))
        m_ref[qi, hd] = m_new

    def produce_diag(slot, hd, d):
        off = pl.multiple_of(d * TQ, TQ)
        s_ref[slot, hd, 0:hq, :] = jnp.dot(keys(hd, off, hq), queries(hd, off, TQ),
                                           preferred_element_type=F32)
        s_ref[slot, hd, hq:TK, hq:TQ] = jnp.dot(keys(hd, off + hq, hq), queries(hd, off + hq, hq),
                                                preferred_element_type=F32)

    def consume_diag(slot, hd, d):
        off = pl.multiple_of(d * TQ, TQ)
        neg = jnp.finfo(F32).min
        s_tl = jnp.where(half_mask, s_ref[slot, hd, 0:hq, 0:hq], neg)
        s_tr = s_ref[slot, hd, 0:hq, hq:TQ]
        s_br = jnp.where(half_mask, s_ref[slot, hd, hq:TK, hq:TQ], neg)
        m = m_ref[d, hd]
        m_l = jnp.maximum(m[:, 0:hq], col_max(s_tl))
        m_r = jnp.maximum(m[:, hq:TQ], jnp.maximum(col_max(s_tr), col_max(s_br)))
        p_l = jnp.exp2(s_tl - m_l).astype(BF16)
        p_r = jnp.concatenate([jnp.exp2(s_tr - m_r), jnp.exp2(s_br - m_r)], axis=0).astype(BF16)
        acc = acc_ref[d, hd]
        acc_l = (jnp.exp2(m[:, 0:hq] - m_l) * acc[:, 0:hq]
                 + jnp.dot(values(hd, off, hq), p_l, preferred_element_type=F32))
        acc_r = (jnp.exp2(m[:, hq:TQ] - m_r) * acc[:, hq:TQ]
                 + jnp.dot(values(hd, off, TK), p_r, preferred_element_type=F32))
        rows = slice(hd * V_DIM, (hd + 1) * V_DIM)
        o_ref[0, rows, pl.ds(off, hq)] = (
            acc_l[0:V_DIM] * (1.0 / acc_l[V_DIM:V_DIM + 1])).astype(o_ref.dtype)
        o_ref[0, rows, pl.ds(off + hq, hq)] = (
            acc_r[0:V_DIM] * (1.0 / acc_r[V_DIM:V_DIM + 1])).astype(o_ref.dtype)

    heads = range(ATTN_HEADS)

    def next_below(qi, j):
        row_end = j + 1 == qi
        return jnp.where(row_end, qi + 1, qi), jnp.where(row_end, 0, j + 1)

    def below_step(slot, tile, maxes, to_diag=False):
        nxt = next_below(*tile)
        new_maxes = []
        for hd in heads:
            if to_diag:
                produce_diag(1 - slot, hd, 0)
            else:
                new_maxes.append(produce_below(1 - slot, hd, *nxt))
            consume_below(slot, hd, *tile, maxes[hd])
        return nxt, tuple(new_maxes)

    def diag_step(slot, d, produce_next=True):
        for hd in heads:
            if produce_next:
                produce_diag(1 - slot, hd, d + 1)
            consume_diag(slot, hd, d)
        return d + 1

    tile0 = (jnp.int32(1), jnp.int32(0))
    maxes0 = tuple(produce_below(0, hd, *tile0) for hd in heads)

    def below_pair(_, carry):
        tile, maxes = carry
        tile, maxes = below_step(0, tile, maxes)
        return below_step(1, tile, maxes)

    tile, maxes = lax.fori_loop(0, n_below // 2 - 1, below_pair, (tile0, maxes0))
    tile, maxes = below_step(0, tile, maxes)
    below_step(1, tile, maxes, to_diag=True)

    def diag_pair(_, d):
        return diag_step(1, diag_step(0, d))

    d = lax.fori_loop(0, n_q // 2 - 1, diag_pair, jnp.int32(0))
    d = diag_step(0, d)
    diag_step(1, d, produce_next=False)


def _attention(qt, k, vt, f32_weights, layer):
    b, _, s = qt.shape
    hb = ATTN_HEADS
    groups = N_HEADS // hb
    n_steps = b * groups
    n_q = s // TQ
    assert n_q % 2 == 0 and (n_q * (n_q - 1) // 2) % 2 == 0, "both tile streams run two tiles per loop trip"

    def rows_per_step(w):
        rows = w.shape[1] // n_steps
        assert rows * n_steps == w.shape[1] and rows % BF16_SUBLANES == 0, w.shape
        return rows

    cast_in = [pl.BlockSpec((1, rows_per_step(w), w.shape[2]), lambda i, h: (layer, i * groups + h, 0))
               for w in f32_weights]
    cast_out = [pl.BlockSpec((1, rows_per_step(w), w.shape[2]), lambda i, h: (0, i * groups + h, 0))
                for w in f32_weights]
    out = pl.pallas_call(
        functools.partial(_attn_kernel, n_q=n_q, n_cast=len(f32_weights)),
        grid=(b, groups),
        in_specs=[
            pl.BlockSpec((1, hb * HEAD_PAD, s), lambda i, h: (i, h, 0)),
            pl.BlockSpec((1, s, hb * HEAD_PAD), lambda i, h: (i, 0, h)),
            pl.BlockSpec((1, hb * V_DIM, s), lambda i, h: (i, h, 0)),
        ] + cast_in,
        out_specs=[pl.BlockSpec((1, hb * V_DIM, s), lambda i, h: (i, h, 0))] + cast_out,
        out_shape=[jax.ShapeDtypeStruct((b, ATTN_DIM, s), BF16)]
        + [jax.ShapeDtypeStruct((1,) + w.shape[1:], BF16) for w in f32_weights],
        scratch_shapes=[
            pltpu.VMEM((2, hb, TK, TQ), F32),
            pltpu.VMEM((n_q, hb, ACC_ROWS, TQ), F32),
            pltpu.VMEM((n_q, hb, 1, TQ), F32),
        ],
        compiler_params=pltpu.CompilerParams(
            dimension_semantics=("arbitrary", "arbitrary"),
            vmem_limit_bytes=VMEM_LIMIT_BYTES),
        name="mla_attention",
    )(qt, k, vt, *f32_weights)
    return out[0], out[1:]


def _mix_kernel(x_ref, ot_ref, mod_ref, g1_ref, g2_ref, gf_ref, w_in_ref, w_pool_ref, pscale_ref,
                p_pool_ref, p_attn_ref, w_out_ref, w_ff1_ref, w_ff2_ref,
                o_ref, uext_ref, *, final, layer):
    tm = x_ref.shape[1]
    si = pl.program_id(1)

    @pl.when(si == 0)
    def _():
        uext_ref[0:POOL_HALO, :] = jnp.zeros((POOL_HALO, POOL_DIM), F32)

    @pl.when(si > 0)
    def _():
        uext_ref[0:POOL_HALO, :] = uext_ref[tm:tm + POOL_HALO, :]

    x = x_ref[0]
    shift1, scale1, gate1, shift2, scale2, gate2 = _mod_rows(mod_ref, range(N_MOD))

    y_b = _tn_dot(ot_ref[0], p_attn_ref[0])
    h = ((_rms(x) * g1_ref[layer:layer + 1, :]) * (1.0 + scale1) + shift1).astype(BF16)
    u = jnp.dot(h, w_in_ref[0, :, 0:POOL_DIM], preferred_element_type=F32)
    gz = jnp.dot(h, w_in_ref[0, :, POOL_DIM:], preferred_element_type=F32)
    gz_a = gz[:, 0:D_MODEL]
    gz_b = gz[:, D_MODEL:]

    uext_ref[POOL_HALO:POOL_HALO + tm, :] = u
    head_pos = si * tm + lax.broadcasted_iota(jnp.int32, (POOL_HALO, 1), 0)
    pooled = []
    for g, w in enumerate(POOL_WINDOWS):
        eg = uext_ref[:, g * POOL_GROUP_DIM:(g + 1) * POOL_GROUP_DIM]
        win = eg
        k = 1
        while k < w:
            win = win + pltpu.roll(win, k, axis=0)
            k *= 2
        win = win[POOL_HALO:]
        ug = eg[POOL_HALO:]
        inv_head = 1.0 / jnp.minimum(head_pos + 1, w).astype(F32)
        mean = jnp.concatenate([win[:POOL_HALO] * inv_head, win[POOL_HALO:] * (1.0 / w)], axis=0)
        yg = jnp.dot((mean - ug).astype(BF16), w_pool_ref[0, g], preferred_element_type=F32)
        pooled.append(yg)
    y_pool = jnp.concatenate(pooled, axis=-1) * pscale_ref[layer:layer + 1, :]
    y_a = jnp.dot(y_pool.astype(BF16), p_pool_ref[0], preferred_element_type=F32)

    merged = _sigmoid(gz_a) * y_a + _sigmoid(gz_b) * y_b
    x1 = x + gate1 * jnp.dot(merged.astype(BF16), w_out_ref[0], preferred_element_type=F32)

    h2 = ((_rms(x1) * g2_ref[layer:layer + 1, :]) * (1.0 + scale2) + shift2).astype(BF16)
    hidden = [jnp.square(jnp.maximum(
        jnp.dot(h2, w_ff1_ref[0, :, c0:c0 + FF_CHUNK], preferred_element_type=F32), 0.0)).astype(BF16)
        for c0 in range(0, D_FF, FF_CHUNK)]
    ff = jnp.dot(jnp.concatenate(hidden, axis=-1), w_ff2_ref[0], preferred_element_type=F32)
    x2 = x1 + gate2 * ff
    if final:
        x2 = _rms(x2) * gf_ref[...]
    o_ref[0] = x2


def _mix(x, ot, mod, wts, layer_wts, layer, *, final):
    b, s, d = x.shape
    tm = TM_MIX
    once = dict(pipeline_mode=pl.Buffered(1))
    names = ["ln1_g", "ln2_g", "final_g", "w_in_b", "w_pool", "pool_scale", "p_pool", "p_attn", "w_out",
             "w_ff1", "w_ff2"]
    whole = {"ln1_g", "ln2_g", "final_g", "pool_scale"}
    wts = {**wts, **layer_wts}
    specs = [_whole_spec(wts[n], **once) if n in whole
             else _layer_spec(wts[n], 0 if n in layer_wts else layer, **once)
             for n in names]
    return pl.pallas_call(
        functools.partial(_mix_kernel, final=final, layer=layer),
        grid=(b, s // tm),
        in_specs=[
            pl.BlockSpec((1, tm, d), lambda i, j: (i, j, 0)),
            pl.BlockSpec((1, ATTN_DIM, tm), lambda i, j: (i, 0, j)),
            _layer_spec(mod, layer, **once),
        ] + specs,
        out_specs=pl.BlockSpec((1, tm, d), lambda i, j: (i, j, 0)),
        out_shape=jax.ShapeDtypeStruct((b, s, d), F32),
        scratch_shapes=[pltpu.VMEM((tm + POOL_HALO, POOL_DIM), F32)],
        compiler_params=pltpu.CompilerParams(
            dimension_semantics=("arbitrary", "arbitrary"),
            vmem_limit_bytes=VMEM_LIMIT_BYTES),
        name="mix_mlp",
    )(x, ot, mod, *[wts[n] for n in names])


def _prep_weights(ln1_g, ln2_g, w_in, q_norm_g, w_uq, kv_norm_g, w_uk, w_uv, w_pool, pool_scale, final_g):
    depth, d, _ = w_in.shape
    c0 = POOL_DIM
    c1 = c0 + Q_LORA
    c2 = c1 + KV_LORA
    c3 = c2 + QK_ROPE
    half = QK_ROPE // 2
    w_kr = w_in[:, :, c2:c3]
    zl = jnp.zeros((depth, d, ROPE_LO), F32)
    zr = jnp.zeros((depth, d, HEAD_PAD - ROPE_HI), F32)
    w_in_a = jnp.concatenate([w_in[:, :, c0:c2], zl, w_kr, zr, zl, -w_kr[:, :, half:], w_kr[:, :, :half], zr],
                             axis=2).astype(BF16)
    w_in_b = jnp.concatenate([w_in[:, :, :c0], w_in[:, :, c3:]], axis=2).astype(BF16)

    pad_q = HEAD_PAD - (QK_NOPE + QK_ROPE)
    wuq_pad = jnp.pad(w_uq, ((0, 0), (0, 0), (0, 0), (0, pad_q)))
    wuqt = wuq_pad.reshape(depth, Q_LORA, N_HEADS * HEAD_PAD).transpose(0, 2, 1).astype(BF16)
    r = w_uq[..., QK_NOPE:]
    rot = jnp.concatenate([-r[..., half:], r[..., :half]], axis=-1)
    wuqrt = rot.reshape(depth, Q_LORA, N_HEADS * QK_ROPE).transpose(0, 2, 1).astype(BF16)
    wuk = jnp.pad(w_uk, ((0, 0), (0, 0), (0, 0), (0, HEAD_PAD - QK_NOPE)))
    wuk = wuk.reshape(depth, KV_LORA, N_HEADS * HEAD_PAD).astype(BF16)
    wuvt = w_uv.reshape(depth, KV_LORA, ATTN_DIM).transpose(0, 2, 1).astype(BF16)
    return dict(
        ln1_g=ln1_g, ln2_g=ln2_g, final_g=final_g.reshape(1, D_MODEL), q_norm_g=q_norm_g,
        kv_norm_g=kv_norm_g, pool_scale=pool_scale, w_in_a=w_in_a, w_in_b=w_in_b, wuqt=wuqt, wuqrt=wuqrt,
        wuk=wuk, wuvt=wuvt, w_pool=w_pool.astype(BF16))


def kernel(x, c, positions, ln1_g, ln2_g, w_ada, b_ada, w_in, q_norm_g, w_uq, kv_norm_g, w_uk,
           w_uv, w_pool, pool_scale, p_pool, p_attn, w_out, w_ff1, w_ff2, final_g):
    depth = w_in.shape[0]
    mod = _modulation(c, w_ada, b_ada)
    wts = _prep_weights(ln1_g, ln2_g, w_in, q_norm_g, w_uq, kv_norm_g, w_uk, w_uv, w_pool, pool_scale,
                        final_g)
    cast_names = ("p_pool", "p_attn", "w_out", "w_ff1", "w_ff2")
    f32_weights = (p_pool, p_attn, w_out, w_ff1, w_ff2)
    for layer in range(depth):
        qt, k, vt = _qkv(x, mod, positions, wts, layer)
        ot, cast = _attention(qt, k, vt, f32_weights, layer)
        x = _mix(x, ot, mod, wts, dict(zip(cast_names, cast)), layer, final=(layer == depth - 1))
    return x
```

```python
import functools
import math

import jax
import jax.numpy as jnp
from jax import lax
from jax.experimental import pallas as pl
from jax.experimental.pallas import tpu as pltpu

D_MODEL = 1024
N_HEADS = 8
QK_NOPE = 64
QK_ROPE = 32
V_DIM = 64
Q_LORA = 384
KV_LORA = 256
POOL_WINDOWS = (2, 4, 8, 16)
POOL_GROUP_DIM = 128
POOL_DIM = len(POOL_WINDOWS) * POOL_GROUP_DIM
ATTN_DIM = N_HEADS * V_DIM
D_FF = 4 * D_MODEL
N_MOD = 6
EPS = 1e-6
ROPE_THETA = 10000.0

HEAD_PAD = 128
ROPE_LO = QK_NOPE
ROPE_HI = QK_NOPE + QK_ROPE
POOL_HALO = 16

VMEM_LIMIT_BYTES = 56 * 1024 * 1024

F32 = jnp.float32
BF16 = jnp.bfloat16

TM_QKV = 1024
TM_MIX = 512
TQ = 512
TK = 512
ATTN_HEADS = 4
BF16_SUBLANES = 16
ACC_ROWS = V_DIM + BF16_SUBLANES
FF_CHUNK = 1024
MOD_TN = 1536


def _nt_dot(a, b):
    return lax.dot_general(a, b, (((1,), (1,)), ((), ())), preferred_element_type=F32)


def _tn_dot(a, b):
    return lax.dot_general(a, b, (((0,), (0,)), ((), ())), preferred_element_type=F32)


def _rms(x):
    return x * lax.rsqrt(jnp.mean(x * x, axis=-1, keepdims=True) + EPS)


def _sigmoid(x):
    return 1.0 / (1.0 + jnp.exp(-x))


def _mod_rows(mod_ref, chunks):
    row = pl.ds(pl.program_id(0), 1)
    return tuple(mod_ref[0, row, c * D_MODEL:(c + 1) * D_MODEL] for c in chunks)


def _layer_spec(arr, layer, **kw):
    tail = (0,) * (arr.ndim - 1)
    return pl.BlockSpec((1,) + arr.shape[1:], lambda i, j: (layer,) + tail, **kw)


def _whole_spec(arr, **kw):
    zeros = (0,) * arr.ndim
    return pl.BlockSpec(arr.shape, lambda i, j: zeros, **kw)


def _mod_kernel(c_ref, w_ref, b_ref, o_ref):
    c = c_ref[...]
    c_act = c * _sigmoid(c)
    o_ref[0] = jnp.dot(c_act.astype(BF16), w_ref[0].astype(BF16),
                       preferred_element_type=F32) + b_ref[0]


def _modulation(c, w_ada, b_ada):
    depth, d, n = w_ada.shape
    b = c.shape[0]
    rows = 8
    c_pad = jnp.pad(c, ((0, rows - b), (0, 0)))
    out = pl.pallas_call(
        _mod_kernel,
        grid=(depth, n // MOD_TN),
        in_specs=[
            pl.BlockSpec((rows, d), lambda l, j: (0, 0)),
            pl.BlockSpec((1, d, MOD_TN), lambda l, j: (l, 0, j)),
            pl.BlockSpec((1, 1, MOD_TN), lambda l, j: (l, 0, j)),
        ],
        out_specs=pl.BlockSpec((1, rows, MOD_TN), lambda l, j: (l, 0, j)),
        out_shape=jax.ShapeDtypeStruct((depth, rows, n), F32),
        compiler_params=pltpu.CompilerParams(vmem_limit_bytes=VMEM_LIMIT_BYTES),
        name="adaln_mod",
    )(c_pad, w_ada, b_ada.reshape(depth, 1, n))
    return out


def _qkv_kernel(x_ref, mod_ref, g_ref, w_in_ref, gq_ref, gkv_ref, wuqt_ref, wuqrt_ref,
                wuk_ref, wuvt_ref, pos_ref, invf_ref,
                qt_ref, k_ref, vt_ref, *, scale, layer):
    x = x_ref[0]
    shift, scl = _mod_rows(mod_ref, (0, 1))
    h = (_rms(x) * g_ref[layer:layer + 1, :]) * (1.0 + scl) + shift
    z = jnp.dot(h.astype(BF16), w_in_ref[0], preferred_element_type=F32)
    c_q = z[:, 0:Q_LORA]
    c_kv = z[:, Q_LORA:Q_LORA + KV_LORA]
    kr = z[:, Q_LORA + KV_LORA:Q_LORA + KV_LORA + HEAD_PAD]
    kr_rot = z[:, Q_LORA + KV_LORA + HEAD_PAD:]
    cqn = (_rms(c_q) * gq_ref[layer:layer + 1, :]).astype(BF16)
    ckvn = (_rms(c_kv) * gkv_ref[layer:layer + 1, :]).astype(BF16)

    qt = _nt_dot(wuqt_ref[0], cqn)
    qt_rot = _nt_dot(wuqrt_ref[0], cqn)
    ang = invf_ref[...] * pos_ref[0].astype(F32)
    cost = jnp.cos(ang)
    sint = jnp.sin(ang)
    for hd in range(N_HEADS):
        base = hd * HEAD_PAD
        qt_ref[0, base:base + ROPE_LO, :] = (qt[base:base + ROPE_LO] * scale).astype(BF16)
        roped = (qt[base + ROPE_LO:base + ROPE_HI] * cost
                 + qt_rot[hd * QK_ROPE:(hd + 1) * QK_ROPE] * sint)
        qt_ref[0, base + ROPE_LO:base + ROPE_HI, :] = (roped * scale).astype(BF16)
        qt_ref[0, base + ROPE_HI:base + HEAD_PAD, :] = (
            qt[base + ROPE_HI:base + HEAD_PAD] * scale).astype(BF16)

    k = jnp.dot(ckvn, wuk_ref[0], preferred_element_type=F32)
    tm = x.shape[0]
    z_lo = jnp.zeros((ROPE_LO, tm), F32)
    z_hi = jnp.zeros((HEAD_PAD - ROPE_HI, tm), F32)
    cos_tok = jnp.concatenate([z_lo, cost, z_hi], axis=0).T
    sin_tok = jnp.concatenate([z_lo, sint, z_hi], axis=0).T
    kr_full = kr * cos_tok + kr_rot * sin_tok
    for hd in range(N_HEADS):
        base = hd * HEAD_PAD
        k_ref[0, :, base:base + HEAD_PAD] = (k[:, base:base + HEAD_PAD] + kr_full).astype(BF16)

    vt_ref[0] = _nt_dot(wuvt_ref[0], ckvn).astype(BF16)


def _qkv(x, mod, positions, wts, layer):
    b, s, d = x.shape
    tm = TM_QKV
    inv_freq = ROPE_THETA ** (-jnp.arange(0, QK_ROPE, 2, dtype=F32) / QK_ROPE)
    invf_col = jnp.concatenate([inv_freq, inv_freq]).reshape(QK_ROPE, 1)
    scale = math.log2(math.e) / math.sqrt(QK_NOPE + QK_ROPE)
    stacked = [wts["ln1_g"], wts["w_in_a"], wts["q_norm_g"], wts["kv_norm_g"], wts["wuqt"], wts["wuqrt"],
               wts["wuk"], wts["wuvt"]]
    return pl.pallas_call(
        functools.partial(_qkv_kernel, scale=scale, layer=layer),
        grid=(b, s // tm),
        in_specs=[
            pl.BlockSpec((1, tm, d), lambda i, j: (i, j, 0)),
            _layer_spec(mod, layer),
            _whole_spec(wts["ln1_g"]),
            _layer_spec(wts["w_in_a"], layer),
            _whole_spec(wts["q_norm_g"]),
            _whole_spec(wts["kv_norm_g"]),
            _layer_spec(wts["wuqt"], layer),
            _layer_spec(wts["wuqrt"], layer),
            _layer_spec(wts["wuk"], layer),
            _layer_spec(wts["wuvt"], layer),
            pl.BlockSpec((1, 1, tm), lambda i, j: (i, 0, j)),
            _whole_spec(invf_col),
        ],
        out_specs=[
            pl.BlockSpec((1, N_HEADS * HEAD_PAD, tm), lambda i, j: (i, 0, j)),
            pl.BlockSpec((1, tm, N_HEADS * HEAD_PAD), lambda i, j: (i, j, 0)),
            pl.BlockSpec((1, ATTN_DIM, tm), lambda i, j: (i, 0, j)),
        ],
        out_shape=[
            jax.ShapeDtypeStruct((b, N_HEADS * HEAD_PAD, s), BF16),
            jax.ShapeDtypeStruct((b, s, N_HEADS * HEAD_PAD), BF16),
            jax.ShapeDtypeStruct((b, ATTN_DIM, s), BF16),
        ],
        compiler_params=pltpu.CompilerParams(
            dimension_semantics=("arbitrary", "arbitrary"),
            vmem_limit_bytes=VMEM_LIMIT_BYTES),
        name="qkv_proj",
    )(x, mod, *stacked, positions.reshape(b, 1, s), invf_col)


def _attn_kernel(qt_ref, k_ref, vt_ref, *refs, n_q, n_cast):
    cast_in, o_ref, cast_out = refs[:n_cast], refs[n_cast], refs[n_cast + 1:2 * n_cast + 1]
    s_ref, acc_ref, m_ref = refs[2 * n_cast + 1:]
    for src, dst in zip(cast_in, cast_out):
        dst[...] = src[...].astype(BF16)

    n_below = n_q * (n_q - 1) // 2
    hq = TQ // 2
    ones_rows = jnp.ones((ACC_ROWS - V_DIM, TK), BF16)
    half_mask = (lax.broadcasted_iota(jnp.int32, (hq, hq), 1)
                 >= lax.broadcasted_iota(jnp.int32, (hq, hq), 0))

    acc_ref[...] = jnp.zeros(acc_ref.shape, F32)
    m_ref[...] = jnp.full(m_ref.shape, jnp.finfo(F32).min, F32)

    def keys(hd, off, n):
        return k_ref[0, pl.ds(off, n), hd * HEAD_PAD:(hd + 1) * HEAD_PAD]

    def queries(hd, off, n):
        return qt_ref[0, hd * HEAD_PAD:(hd + 1) * HEAD_PAD, pl.ds(off, n)]

    def values(hd, off, n):
        return jnp.concatenate(
            [vt_ref[0, hd * V_DIM:(hd + 1) * V_DIM, pl.ds(off, n)], ones_rows[:, :n]], axis=0)

    def col_max(s):
        return jnp.max(s, axis=0, keepdims=True)

    def produce_below(slot, hd, qi, j):
        s = jnp.dot(keys(hd, pl.multiple_of(j * TK, TK), TK), queries(hd, pl.multiple_of(qi * TQ, TQ), TQ),
                    preferred_element_type=F32)
        s_ref[slot, hd] = s
        return col_max(s)

    def consume_below(slot, hd, qi, j, tile_max):
        m = m_ref[qi, hd]
        m_new = jnp.maximum(m, tile_max)
        p = jnp.exp2(s_ref[slot, hd] - m_new).astype(BF16)
        acc_ref[qi, hd] = (jnp.exp2(m - m_new) * acc_ref[qi, hd]
                           + jnp.dot(values(hd, pl.multiple_of(j * TK, TK), TK), p, preferred_element_type=F32))
        m_ref[qi, hd] = m_new

    def produce_diag(slot, hd, d):
        off = pl.multiple_of(d * TQ, TQ)
        s_ref[slot, hd, 0:hq, :] = jnp.dot(keys(hd, off, hq), queries(hd, off, TQ),
                                           preferred_element_type=F32)
        s_ref[slot, hd, hq:TK, hq:TQ] = jnp.dot(keys(hd, off + hq, hq), queries(hd, off + hq, hq),
                                                preferred_element_type=F32)

    def consume_diag(slot, hd, d):
        off = pl.multiple_of(d * TQ, TQ)
        neg = jnp.finfo(F32).min
        s_tl = jnp.where(half_mask, s_ref[slot, hd, 0:hq, 0:hq], neg)
        s_tr = s_ref[slot, hd, 0:hq, hq:TQ]
        s_br = jnp.where(half_mask, s_ref[slot, hd, hq:TK, hq:TQ], neg)
        m = m_ref[d, hd]
        m_l = jnp.maximum(m[:, 0:hq], col_max(s_tl))
        m_r = jnp.maximum(m[:, hq:TQ], jnp.maximum(col_max(s_tr), col_max(s_br)))
        p_l = jnp.exp2(s_tl - m_l).astype(BF16)
        p_r = jnp.concatenate([jnp.exp2(s_tr - m_r), jnp.exp2(s_br - m_r)], axis=0).astype(BF16)
        acc = acc_ref[d, hd]
        acc_l = (jnp.exp2(m[:, 0:hq] - m_l) * acc[:, 0:hq]
                 + jnp.dot(values(hd, off, hq), p_l, preferred_element_type=F32))
        acc_r = (jnp.exp2(m[:, hq:TQ] - m_r) * acc[:, hq:TQ]
                 + jnp.dot(values(hd, off, TK), p_r, preferred_element_type=F32))
        rows = slice(hd * V_DIM, (hd + 1) * V_DIM)
        o_ref[0, rows, pl.ds(off, hq)] = (
            acc_l[0:V_DIM] * (1.0 / acc_l[V_DIM:V_DIM + 1])).astype(o_ref.dtype)
        o_ref[0, rows, pl.ds(off + hq, hq)] = (
            acc_r[0:V_DIM] * (1.0 / acc_r[V_DIM:V_DIM + 1])).astype(o_ref.dtype)

    heads = range(ATTN_HEADS)

    def next_below(qi, j):
        row_end = j + 1 == qi
        return jnp.where(row_end, qi + 1, qi), jnp.where(row_end, 0, j + 1)

    def below_step(slot, tile, maxes, to_diag=False):
        nxt = next_below(*tile)
        new_maxes = []
        for hd in heads:
            if to_diag:
                produce_diag(1 - slot, hd, 0)
            else:
                new_maxes.append(produce_below(1 - slot, hd, *nxt))
            consume_below(slot, hd, *tile, maxes[hd])
        return nxt, tuple(new_maxes)

    def diag_step(slot, d, produce_next=True):
        for hd in heads:
            if produce_next:
                produce_diag(1 - slot, hd, d + 1)
            consume_diag(slot, hd, d)
        return d + 1

    tile0 = (jnp.int32(1), jnp.int32(0))
    maxes0 = tuple(produce_below(0, hd, *tile0) for hd in heads)

    def below_pair(_, carry):
        tile, maxes = carry
        tile, maxes = below_step(0, tile, maxes)
        return below_step(1, tile, maxes)

    tile, maxes = lax.fori_loop(0, n_below // 2 - 1, below_pair, (tile0, maxes0))
    tile, maxes = below_step(0, tile, maxes)
    below_step(1, tile, maxes, to_diag=True)

    def diag_pair(_, d):
        return diag_step(1, diag_step(0, d))

    d = lax.fori_loop(0, n_q // 2 - 1, diag_pair, jnp.int32(0))
    d = diag_step(0, d)
    diag_step(1, d, produce_next=False)


def _attention(qt, k, vt, f32_weights, layer):
    b, _, s = qt.shape
    hb = ATTN_HEADS
    groups = N_HEADS // hb
    n_steps = b * groups
    n_q = s // TQ
    assert n_q % 2 == 0 and (n_q * (n_q - 1) // 2) % 2 == 0, "both tile streams run two tiles per loop trip"

    def rows_per_step(w):
        rows = w.shape[1] // n_steps
        assert rows * n_steps == w.shape[1] and rows % BF16_SUBLANES == 0, w.shape
        return rows

    cast_in = [pl.BlockSpec((1, rows_per_step(w), w.shape[2]), lambda i, h: (layer, i * groups + h, 0))
               for w in f32_weights]
    cast_out = [pl.BlockSpec((1, rows_per_step(w), w.shape[2]), lambda i, h: (0, i * groups + h, 0))
                for w in f32_weights]
    out = pl.pallas_call(
        functools.partial(_attn_kernel, n_q=n_q, n_cast=len(f32_weights)),
        grid=(b, groups),
        in_specs=[
            pl.BlockSpec((1, hb * HEAD_PAD, s), lambda i, h: (i, h, 0)),
            pl.BlockSpec((1, s, hb * HEAD_PAD), lambda i, h: (i, 0, h)),
            pl.BlockSpec((1, hb * V_DIM, s), lambda i, h: (i, h, 0)),
        ] + cast_in,
        out_specs=[pl.BlockSpec((1, hb * V_DIM, s), lambda i, h: (i, h, 0))] + cast_out,
        out_shape=[jax.ShapeDtypeStruct((b, ATTN_DIM, s), BF16)]
        + [jax.ShapeDtypeStruct((1,) + w.shape[1:], BF16) for w in f32_weights],
        scratch_shapes=[
            pltpu.VMEM((2, hb, TK, TQ), F32),
            pltpu.VMEM((n_q, hb, ACC_ROWS, TQ), F32),
            pltpu.VMEM((n_q, hb, 1, TQ), F32),
        ],
        compiler_params=pltpu.CompilerParams(
            dimension_semantics=("arbitrary", "arbitrary"),
            vmem_limit_bytes=VMEM_LIMIT_BYTES),
        name="mla_attention",
    )(qt, k, vt, *f32_weights)
    return out[0], out[1:]


def _mix_kernel(x_ref, ot_ref, mod_ref, g1_ref, g2_ref, gf_ref, w_in_ref, w_pool_ref, pscale_ref,
                p_pool_ref, p_attn_ref, w_out_ref, w_ff1_ref, w_ff2_ref,
                o_ref, uext_ref, *, final, layer):
    tm = x_ref.shape[1]
    si = pl.program_id(1)

    @pl.when(si == 0)
    def _():
        uext_ref[0:POOL_HALO, :] = jnp.zeros((POOL_HALO, POOL_DIM), F32)

    @pl.when(si > 0)
    def _():
        uext_ref[0:POOL_HALO, :] = uext_ref[tm:tm + POOL_HALO, :]

    x = x_ref[0]
    shift1, scale1, gate1, shift2, scale2, gate2 = _mod_rows(mod_ref, range(N_MOD))

    y_b = _tn_dot(ot_ref[0], p_attn_ref[0])
    h = ((_rms(x) * g1_ref[layer:layer + 1, :]) * (1.0 + scale1) + shift1).astype(BF16)
    u = jnp.dot(h, w_in_ref[0, :, 0:POOL_DIM], preferred_element_type=F32)
    gz = jnp.dot(h, w_in_ref[0, :, POOL_DIM:], preferred_element_type=F32)
    gz_a = gz[:, 0:D_MODEL]
    gz_b = gz[:, D_MODEL:]

    uext_ref[POOL_HALO:POOL_HALO + tm, :] = u
    head_pos = si * tm + lax.broadcasted_iota(jnp.int32, (POOL_HALO, 1), 0)
    pooled = []
    for g, w in enumerate(POOL_WINDOWS):
        eg = uext_ref[:, g * POOL_GROUP_DIM:(g + 1) * POOL_GROUP_DIM]
        win = eg
        k = 1
        while k < w:
            win = win + pltpu.roll(win, k, axis=0)
            k *= 2
        win = win[POOL_HALO:]
        ug = eg[POOL_HALO:]
        inv_head = 1.0 / jnp.minimum(head_pos + 1, w).astype(F32)
        mean = jnp.concatenate([win[:POOL_HALO] * inv_head, win[POOL_HALO:] * (1.0 / w)], axis=0)
        yg = jnp.dot((mean - ug).astype(BF16), w_pool_ref[0, g], preferred_element_type=F32)
        pooled.append(yg)
    y_pool = jnp.concatenate(pooled, axis=-1) * pscale_ref[layer:layer + 1, :]
    y_a = jnp.dot(y_pool.astype(BF16), p_pool_ref[0], preferred_element_type=F32)

    merged = _sigmoid(gz_a) * y_a + _sigmoid(gz_b) * y_b
    x1 = x + gate1 * jnp.dot(merged.astype(BF16), w_out_ref[0], preferred_element_type=F32)

    h2 = ((_rms(x1) * g2_ref[layer:layer + 1, :]) * (1.0 + scale2) + shift2).astype(BF16)
    hidden = [jnp.square(jnp.maximum(
        jnp.dot(h2, w_ff1_ref[0, :, c0:c0 + FF_CHUNK], preferred_element_type=F32), 0.0)).astype(BF16)
        for c0 in range(0, D_FF, FF_CHUNK)]
    ff = jnp.dot(jnp.concatenate(hidden, axis=-1), w_ff2_ref[0], preferred_element_type=F32)
    x2 = x1 + gate2 * ff
    if final:
        x2 = _rms(x2) * gf_ref[...]
    o_ref[0] = x2


def _mix(x, ot, mod, wts, layer_wts, layer, *, final):
    b, s, d = x.shape
    tm = TM_MIX
    once = dict(pipeline_mode=pl.Buffered(1))
    names = ["ln1_g", "ln2_g", "final_g", "w_in_b", "w_pool", "pool_scale", "p_pool", "p_attn", "w_out",
             "w_ff1", "w_ff2"]
    whole = {"ln1_g", "ln2_g", "final_g", "pool_scale"}
    wts = {**wts, **layer_wts}
    specs = [_whole_spec(wts[n], **once) if n in whole
             else _layer_spec(wts[n], 0 if n in layer_wts else layer, **once)
             for n in names]
    return pl.pallas_call(
        functools.partial(_mix_kernel, final=final, layer=layer),
        grid=(b, s // tm),
        in_specs=[
            pl.BlockSpec((1, tm, d), lambda i, j: (i, j, 0)),
            pl.BlockSpec((1, ATTN_DIM, tm), lambda i, j: (i, 0, j)),
            _layer_spec(mod, layer, **once),
        ] + specs,
        out_specs=pl.BlockSpec((1, tm, d), lambda i, j: (i, j, 0)),
        out_shape=jax.ShapeDtypeStruct((b, s, d), F32),
        scratch_shapes=[pltpu.VMEM((tm + POOL_HALO, POOL_DIM), F32)],
        compiler_params=pltpu.CompilerParams(
            dimension_semantics=("arbitrary", "arbitrary"),
            vmem_limit_bytes=VMEM_LIMIT_BYTES),
        name="mix_mlp",
    )(x, ot, mod, *[wts[n] for n in names])


W_IN_PREP_ROWS = 256


def _w_in_prep_kernel(w_ref, a_ref, b_ref):
    w = w_ref[0]
    rows = w.shape[0]
    c0 = POOL_DIM
    c2 = c0 + Q_LORA + KV_LORA
    c3 = c2 + QK_ROPE
    half = QK_ROPE // 2
    kr = w[:, c2:c3]
    zl = jnp.zeros((rows, ROPE_LO), F32)
    zr = jnp.zeros((rows, HEAD_PAD - ROPE_HI), F32)
    a = jnp.concatenate([w[:, c0:c2], zl, kr, zr, zl, -kr[:, half:], kr[:, :half], zr], axis=1)
    a_ref[0] = a.astype(BF16)
    b_ref[0] = jnp.concatenate([w[:, :c0], w[:, c3:]], axis=1).astype(BF16)


def _w_in_prep(w_in):
    depth, d, n = w_in.shape
    na = Q_LORA + KV_LORA + 2 * HEAD_PAD
    nb = n - (Q_LORA + KV_LORA + QK_ROPE)
    rows = W_IN_PREP_ROWS
    return pl.pallas_call(
        _w_in_prep_kernel,
        grid=(depth, d // rows),
        in_specs=[pl.BlockSpec((1, rows, n), lambda l, r: (l, r, 0))],
        out_specs=[pl.BlockSpec((1, rows, na), lambda l, r: (l, r, 0)),
                   pl.BlockSpec((1, rows, nb), lambda l, r: (l, r, 0))],
        out_shape=[jax.ShapeDtypeStruct((depth, d, na), BF16), jax.ShapeDtypeStruct((depth, d, nb), BF16)],
        compiler_params=pltpu.CompilerParams(vmem_limit_bytes=VMEM_LIMIT_BYTES),
        name="w_in_prep",
    )(w_in)


def _prep_weights(ln1_g, ln2_g, w_in, q_norm_g, w_uq, kv_norm_g, w_uk, w_uv, w_pool, pool_scale, final_g):
    depth = w_in.shape[0]
    half = QK_ROPE // 2
    w_in_a, w_in_b = _w_in_prep(w_in)

    pad_q = HEAD_PAD - (QK_NOPE + QK_ROPE)
    wuq_pad = jnp.pad(w_uq, ((0, 0), (0, 0), (0, 0), (0, pad_q)))
    wuqt = wuq_pad.reshape(depth, Q_LORA, N_HEADS * HEAD_PAD).transpose(0, 2, 1).astype(BF16)
    r = w_uq[..., QK_NOPE:]
    rot = jnp.concatenate([-r[..., half:], r[..., :half]], axis=-1)
    wuqrt = rot.reshape(depth, Q_LORA, N_HEADS * QK_ROPE).transpose(0, 2, 1).astype(BF16)
    wuk = jnp.pad(w_uk, ((0, 0), (0, 0), (0, 0), (0, HEAD_PAD - QK_NOPE)))
    wuk = wuk.reshape(depth, KV_LORA, N_HEADS * HEAD_PAD).astype(BF16)
    wuvt = w_uv.reshape(depth, KV_LORA, ATTN_DIM).transpose(0, 2, 1).astype(BF16)
    return dict(
        ln1_g=ln1_g, ln2_g=ln2_g, final_g=final_g.reshape(1, D_MODEL), q_norm_g=q_norm_g,
        kv_norm_g=kv_norm_g, pool_scale=pool_scale, w_in_a=w_in_a, w_in_b=w_in_b, wuqt=wuqt, wuqrt=wuqrt,
        wuk=wuk, wuvt=wuvt, w_pool=w_pool.astype(BF16))


def kernel(x, c, positions, ln1_g, ln2_g, w_ada, b_ada, w_in, q_norm_g, w_uq, kv_norm_g, w_uk,
           w_uv, w_pool, pool_scale, p_pool, p_attn, w_out, w_ff1, w_ff2, final_g):
    depth = w_in.shape[0]
    mod = _modulation(c, w_ada, b_ada)
    wts = _prep_weights(ln1_g, ln2_g, w_in, q_norm_g, w_uq, kv_norm_g, w_uk, w_uv, w_pool, pool_scale,
                        final_g)
    cast_names = ("p_pool", "p_attn", "w_out", "w_ff1", "w_ff2")
    f32_weights = (p_pool, p_attn, w_out, w_ff1, w_ff2)
    for layer in range(depth):
        qt, k, vt = _qkv(x, mod, positions, wts, layer)
        ot, cast = _attention(qt, k, vt, f32_weights, layer)
        x = _mix(x, ot, mod, wts, dict(zip(cast_names, cast)), layer, final=(layer == depth - 1))
    return x
```

```python
import functools
import math

import jax
import jax.numpy as jnp
from jax import lax
from jax.experimental import pallas as pl
from jax.experimental.pallas import tpu as pltpu

D_MODEL = 1024
N_HEADS = 8
QK_NOPE = 64
QK_ROPE = 32
V_DIM = 64
Q_LORA = 384
KV_LORA = 256
POOL_WINDOWS = (2, 4, 8, 16)
POOL_GROUP_DIM = 128
POOL_DIM = len(POOL_WINDOWS) * POOL_GROUP_DIM
ATTN_DIM = N_HEADS * V_DIM
D_FF = 4 * D_MODEL
N_MOD = 6
EPS = 1e-6
ROPE_THETA = 10000.0

HEAD_PAD = 128
ROPE_LO = QK_NOPE
ROPE_HI = QK_NOPE + QK_ROPE
POOL_HALO = 16

VMEM_LIMIT_BYTES = 56 * 1024 * 1024

F32 = jnp.float32
BF16 = jnp.bfloat16

TM_QKV = 1024
QKV_SUB = 512
TM_MIX = 1024
MIX_SUB = 512
TQ = 512
TK = 512
ATTN_HEADS = 4
BF16_SUBLANES = 16
ACC_ROWS = V_DIM + BF16_SUBLANES
FF_CHUNK = 1024
MOD_TN = 1536


def _nt_dot(a, b):
    return lax.dot_general(a, b, (((1,), (1,)), ((), ())), preferred_element_type=F32)


def _tn_dot(a, b):
    return lax.dot_general(a, b, (((0,), (0,)), ((), ())), preferred_element_type=F32)


def _rms(x):
    return x * lax.rsqrt(jnp.mean(x * x, axis=-1, keepdims=True) + EPS)


def _sigmoid(x):
    return 1.0 / (1.0 + jnp.exp(-x))


def _mod_rows(mod_ref, chunks):
    row = pl.ds(pl.program_id(0), 1)
    return tuple(mod_ref[0, row, c * D_MODEL:(c + 1) * D_MODEL] for c in chunks)


def _layer_spec(arr, layer, **kw):
    tail = (0,) * (arr.ndim - 1)
    return pl.BlockSpec((1,) + arr.shape[1:], lambda i, j: (layer,) + tail, **kw)


def _whole_spec(arr, **kw):
    zeros = (0,) * arr.ndim
    return pl.BlockSpec(arr.shape, lambda i, j: zeros, **kw)


def _mod_kernel(c_ref, w_ref, b_ref, o_ref):
    c = c_ref[...]
    c_act = c * _sigmoid(c)
    o_ref[0] = jnp.dot(c_act.astype(BF16), w_ref[0].astype(BF16),
                       preferred_element_type=F32) + b_ref[0]


def _modulation(c, w_ada, b_ada):
    depth, d, n = w_ada.shape
    b = c.shape[0]
    rows = 8
    c_pad = jnp.pad(c, ((0, rows - b), (0, 0)))
    out = pl.pallas_call(
        _mod_kernel,
        grid=(depth, n // MOD_TN),
        in_specs=[
            pl.BlockSpec((rows, d), lambda l, j: (0, 0)),
            pl.BlockSpec((1, d, MOD_TN), lambda l, j: (l, 0, j)),
            pl.BlockSpec((1, 1, MOD_TN), lambda l, j: (l, 0, j)),
        ],
        out_specs=pl.BlockSpec((1, rows, MOD_TN), lambda l, j: (l, 0, j)),
        out_shape=jax.ShapeDtypeStruct((depth, rows, n), F32),
        compiler_params=pltpu.CompilerParams(vmem_limit_bytes=VMEM_LIMIT_BYTES),
        name="adaln_mod",
    )(c_pad, w_ada, b_ada.reshape(depth, 1, n))
    return out


def _qkv_kernel(x_ref, mod_ref, g_ref, w_in_ref, gq_ref, gkv_ref, wuqt_ref, wuqrt_ref,
                wuk_ref, wuvt_ref, pos_ref, invf_ref,
                qt_ref, k_ref, vt_ref, *, scale, layer):
    shift, scl = _mod_rows(mod_ref, (0, 1))
    toks = [slice(r0, r0 + QKV_SUB) for r0 in range(0, x_ref.shape[1], QKV_SUB)]

    hs = [((_rms(x_ref[0, tok, :]) * g_ref[layer:layer + 1, :]) * (1.0 + scl) + shift).astype(BF16)
          for tok in toks]
    zs = [jnp.dot(h, w_in_ref[0], preferred_element_type=F32) for h in hs]

    latents = []
    for z in zs:
        c_q = z[:, 0:Q_LORA]
        c_kv = z[:, Q_LORA:Q_LORA + KV_LORA]
        latents.append(((_rms(c_q) * gq_ref[layer:layer + 1, :]).astype(BF16),
                        (_rms(c_kv) * gkv_ref[layer:layer + 1, :]).astype(BF16)))

    prods = [(_nt_dot(wuqt_ref[0], cqn), _nt_dot(wuqrt_ref[0], cqn),
              jnp.dot(ckvn, wuk_ref[0], preferred_element_type=F32), _nt_dot(wuvt_ref[0], ckvn))
             for cqn, ckvn in latents]

    z_lo = jnp.zeros((ROPE_LO, QKV_SUB), F32)
    z_hi = jnp.zeros((HEAD_PAD - ROPE_HI, QKV_SUB), F32)
    for tok, z, (qt, qt_rot, k, vt) in zip(toks, zs, prods):
        ang = invf_ref[...] * pos_ref[0, :, tok].astype(F32)
        cost = jnp.cos(ang)
        sint = jnp.sin(ang)
        for hd in range(N_HEADS):
            base = hd * HEAD_PAD
            qt_ref[0, base:base + ROPE_LO, tok] = (qt[base:base + ROPE_LO] * scale).astype(BF16)
            roped = (qt[base + ROPE_LO:base + ROPE_HI] * cost
                     + qt_rot[hd * QK_ROPE:(hd + 1) * QK_ROPE] * sint)
            qt_ref[0, base + ROPE_LO:base + ROPE_HI, tok] = (roped * scale).astype(BF16)
            qt_ref[0, base + ROPE_HI:base + HEAD_PAD, tok] = (
                qt[base + ROPE_HI:base + HEAD_PAD] * scale).astype(BF16)

        kr = z[:, Q_LORA + KV_LORA:Q_LORA + KV_LORA + HEAD_PAD]
        kr_rot = z[:, Q_LORA + KV_LORA + HEAD_PAD:]
        cos_tok = jnp.concatenate([z_lo, cost, z_hi], axis=0).T
        sin_tok = jnp.concatenate([z_lo, sint, z_hi], axis=0).T
        kr_full = kr * cos_tok + kr_rot * sin_tok
        for hd in range(N_HEADS):
            base = hd * HEAD_PAD
            k_ref[0, tok, base:base + HEAD_PAD] = (k[:, base:base + HEAD_PAD] + kr_full).astype(BF16)

        vt_ref[0, :, tok] = vt.astype(BF16)


def _qkv(x, mod, positions, wts, layer):
    b, s, d = x.shape
    tm = TM_QKV
    inv_freq = ROPE_THETA ** (-jnp.arange(0, QK_ROPE, 2, dtype=F32) / QK_ROPE)
    invf_col = jnp.concatenate([inv_freq, inv_freq]).reshape(QK_ROPE, 1)
    scale = math.log2(math.e) / math.sqrt(QK_NOPE + QK_ROPE)
    stacked = [wts["ln1_g"], wts["w_in_a"], wts["q_norm_g"], wts["kv_norm_g"], wts["wuqt"], wts["wuqrt"],
               wts["wuk"], wts["wuvt"]]
    return pl.pallas_call(
        functools.partial(_qkv_kernel, scale=scale, layer=layer),
        grid=(b, s // tm),
        in_specs=[
            pl.BlockSpec((1, tm, d), lambda i, j: (i, j, 0)),
            _layer_spec(mod, layer),
            _whole_spec(wts["ln1_g"]),
            _layer_spec(wts["w_in_a"], layer),
            _whole_spec(wts["q_norm_g"]),
            _whole_spec(wts["kv_norm_g"]),
            _layer_spec(wts["wuqt"], layer),
            _layer_spec(wts["wuqrt"], layer),
            _layer_spec(wts["wuk"], layer),
            _layer_spec(wts["wuvt"], layer),
            pl.BlockSpec((1, 1, tm), lambda i, j: (i, 0, j)),
            _whole_spec(invf_col),
        ],
        out_specs=[
            pl.BlockSpec((1, N_HEADS * HEAD_PAD, tm), lambda i, j: (i, 0, j)),
            pl.BlockSpec((1, tm, N_HEADS * HEAD_PAD), lambda i, j: (i, j, 0)),
            pl.BlockSpec((1, ATTN_DIM, tm), lambda i, j: (i, 0, j)),
        ],
        out_shape=[
            jax.ShapeDtypeStruct((b, N_HEADS * HEAD_PAD, s), BF16),
            jax.ShapeDtypeStruct((b, s, N_HEADS * HEAD_PAD), BF16),
            jax.ShapeDtypeStruct((b, ATTN_DIM, s), BF16),
        ],
        compiler_params=pltpu.CompilerParams(
            dimension_semantics=("arbitrary", "arbitrary"),
            vmem_limit_bytes=VMEM_LIMIT_BYTES),
        name="qkv_proj",
    )(x, mod, *stacked, positions.reshape(b, 1, s), invf_col)


def _attn_kernel(qt_ref, k_ref, vt_ref, *refs, n_q, n_cast):
    cast_in, o_ref, cast_out = refs[:n_cast], refs[n_cast], refs[n_cast + 1:2 * n_cast + 1]
    s_ref, acc_ref, m_ref = refs[2 * n_cast + 1:]
    for src, dst in zip(cast_in, cast_out):
        dst[...] = src[...].astype(BF16)

    n_below = n_q * (n_q - 1) // 2
    hq = TQ // 2
    ones_rows = jnp.ones((ACC_ROWS - V_DIM, TK), BF16)
    half_mask = (lax.broadcasted_iota(jnp.int32, (hq, hq), 1)
                 >= lax.broadcasted_iota(jnp.int32, (hq, hq), 0))

    acc_ref[...] = jnp.zeros(acc_ref.shape, F32)
    m_ref[...] = jnp.full(m_ref.shape, jnp.finfo(F32).min, F32)

    def keys(hd, off, n):
        return k_ref[0, pl.ds(off, n), hd * HEAD_PAD:(hd + 1) * HEAD_PAD]

    def queries(hd, off, n):
        return qt_ref[0, hd * HEAD_PAD:(hd + 1) * HEAD_PAD, pl.ds(off, n)]

    def values(hd, off, n):
        return jnp.concatenate(
            [vt_ref[0, hd * V_DIM:(hd + 1) * V_DIM, pl.ds(off, n)], ones_rows[:, :n]], axis=0)

    def col_max(s):
        return jnp.max(s, axis=0, keepdims=True)

    def produce_below(slot, hd, qi, j):
        s = jnp.dot(keys(hd, pl.multiple_of(j * TK, TK), TK), queries(hd, pl.multiple_of(qi * TQ, TQ), TQ),
                    preferred_element_type=F32)
        s_ref[slot, hd] = s
        return col_max(s)

    def consume_below(slot, hd, qi, j, tile_max):
        m = m_ref[qi, hd]
        m_new = jnp.maximum(m, tile_max)
        p = jnp.exp2(s_ref[slot, hd] - m_new).astype(BF16)
        acc_ref[qi, hd] = (jnp.exp2(m - m_new) * acc_ref[qi, hd]
                           + jnp.dot(values(hd, pl.multiple_of(j * TK, TK), TK), p, preferred_element_type=F32))
        m_ref[qi, hd] = m_new

    def produce_diag(slot, hd, d):
        off = pl.multiple_of(d * TQ, TQ)
        s_ref[slot, hd, 0:hq, :] = jnp.dot(keys(hd, off, hq), queries(hd, off, TQ),
                                           preferred_element_type=F32)
        s_ref[slot, hd, hq:TK, hq:TQ] = jnp.dot(keys(hd, off + hq, hq), queries(hd, off + hq, hq),
                                                preferred_element_type=F32)

    def consume_diag(slot, hd, d):
        off = pl.multiple_of(d * TQ, TQ)
        neg = jnp.finfo(F32).min
        s_tl = jnp.where(half_mask, s_ref[slot, hd, 0:hq, 0:hq], neg)
        s_tr = s_ref[slot, hd, 0:hq, hq:TQ]
        s_br = jnp.where(half_mask, s_ref[slot, hd, hq:TK, hq:TQ], neg)
        m = m_ref[d, hd]
        m_l = jnp.maximum(m[:, 0:hq], col_max(s_tl))
        m_r = jnp.maximum(m[:, hq:TQ], jnp.maximum(col_max(s_tr), col_max(s_br)))
        p_l = jnp.exp2(s_tl - m_l).astype(BF16)
        p_r = jnp.concatenate([jnp.exp2(s_tr - m_r), jnp.exp2(s_br - m_r)], axis=0).astype(BF16)
        acc = acc_ref[d, hd]
        acc_l = (jnp.exp2(m[:, 0:hq] - m_l) * acc[:, 0:hq]
                 + jnp.dot(values(hd, off, hq), p_l, preferred_element_type=F32))
        acc_r = (jnp.exp2(m[:, hq:TQ] - m_r) * acc[:, hq:TQ]
                 + jnp.dot(values(hd, off, TK), p_r, preferred_element_type=F32))
        rows = slice(hd * V_DIM, (hd + 1) * V_DIM)
        o_ref[0, rows, pl.ds(off, hq)] = (
            acc_l[0:V_DIM] * (1.0 / acc_l[V_DIM:V_DIM + 1])).astype(o_ref.dtype)
        o_ref[0, rows, pl.ds(off + hq, hq)] = (
            acc_r[0:V_DIM] * (1.0 / acc_r[V_DIM:V_DIM + 1])).astype(o_ref.dtype)

    heads = range(ATTN_HEADS)

    def next_below(qi, j):
        row_end = j + 1 == qi
        return jnp.where(row_end, qi + 1, qi), jnp.where(row_end, 0, j + 1)

    def below_step(slot, tile, maxes, to_diag=False):
        nxt = next_below(*tile)
        new_maxes = []
        for hd in heads:
            if to_diag:
                produce_diag(1 - slot, hd, 0)
            else:
                new_maxes.append(produce_below(1 - slot, hd, *nxt))
            consume_below(slot, hd, *tile, maxes[hd])
        return nxt, tuple(new_maxes)

    def diag_step(slot, d, produce_next=True):
        for hd in heads:
            if produce_next:
                produce_diag(1 - slot, hd, d + 1)
            consume_diag(slot, hd, d)
        return d + 1

    tile0 = (jnp.int32(1), jnp.int32(0))
    maxes0 = tuple(produce_below(0, hd, *tile0) for hd in heads)

    def below_pair(_, carry):
        tile, maxes = carry
        tile, maxes = below_step(0, tile, maxes)
        return below_step(1, tile, maxes)

    tile, maxes = lax.fori_loop(0, n_below // 2 - 1, below_pair, (tile0, maxes0))
    tile, maxes = below_step(0, tile, maxes)
    below_step(1, tile, maxes, to_diag=True)

    def diag_pair(_, d):
        return diag_step(1, diag_step(0, d))

    d = lax.fori_loop(0, n_q // 2 - 1, diag_pair, jnp.int32(0))
    d = diag_step(0, d)
    diag_step(1, d, produce_next=False)


def _attention(qt, k, vt, f32_weights, layer):
    b, _, s = qt.shape
    hb = ATTN_HEADS
    groups = N_HEADS // hb
    n_steps = b * groups
    n_q = s // TQ
    assert n_q % 2 == 0 and (n_q * (n_q - 1) // 2) % 2 == 0, "both tile streams run two tiles per loop trip"

    def rows_per_step(w):
        rows = w.shape[1] // n_steps
        assert rows * n_steps == w.shape[1] and rows % BF16_SUBLANES == 0, w.shape
        return rows

    cast_in = [pl.BlockSpec((1, rows_per_step(w), w.shape[2]), lambda i, h: (layer, i * groups + h, 0))
               for w in f32_weights]
    cast_out = [pl.BlockSpec((1, rows_per_step(w), w.shape[2]), lambda i, h: (0, i * groups + h, 0))
                for w in f32_weights]
    out = pl.pallas_call(
        functools.partial(_attn_kernel, n_q=n_q, n_cast=len(f32_weights)),
        grid=(b, groups),
        in_specs=[
            pl.BlockSpec((1, hb * HEAD_PAD, s), lambda i, h: (i, h, 0)),
            pl.BlockSpec((1, s, hb * HEAD_PAD), lambda i, h: (i, 0, h)),
            pl.BlockSpec((1, hb * V_DIM, s), lambda i, h: (i, h, 0)),
        ] + cast_in,
        out_specs=[pl.BlockSpec((1, hb * V_DIM, s), lambda i, h: (i, h, 0))] + cast_out,
        out_shape=[jax.ShapeDtypeStruct((b, ATTN_DIM, s), BF16)]
        + [jax.ShapeDtypeStruct((1,) + w.shape[1:], BF16) for w in f32_weights],
        scratch_shapes=[
            pltpu.VMEM((2, hb, TK, TQ), F32),
            pltpu.VMEM((n_q, hb, ACC_ROWS, TQ), F32),
            pltpu.VMEM((n_q, hb, 1, TQ), F32),
        ],
        compiler_params=pltpu.CompilerParams(
            dimension_semantics=("arbitrary", "arbitrary"),
            vmem_limit_bytes=VMEM_LIMIT_BYTES),
        name="mla_attention",
    )(qt, k, vt, *f32_weights)
    return out[0], out[1:]


def _mix_kernel(x_ref, ot_ref, mod_ref, g1_ref, g2_ref, gf_ref, w_in_ref, w_pool_ref, pscale_ref,
                p_pool_ref, p_attn_ref, w_out_ref, w_ff1_ref, w_ff2_ref,
                o_ref, uext_ref, *, final, layer):
    tm = x_ref.shape[1]
    si = pl.program_id(1)

    @pl.when(si == 0)
    def _():
        uext_ref[0:POOL_HALO, :] = jnp.zeros((POOL_HALO, POOL_DIM), F32)

    @pl.when(si > 0)
    def _():
        uext_ref[0:POOL_HALO, :] = uext_ref[tm:tm + POOL_HALO, :]

    shift1, scale1, gate1, shift2, scale2, gate2 = _mod_rows(mod_ref, range(N_MOD))

    for r0 in range(0, tm, MIX_SUB):
        tok = slice(r0, r0 + MIX_SUB)
        x = x_ref[0, tok, :]

        y_b = _tn_dot(ot_ref[0, :, tok], p_attn_ref[0])
        h = ((_rms(x) * g1_ref[layer:layer + 1, :]) * (1.0 + scale1) + shift1).astype(BF16)
        u = jnp.dot(h, w_in_ref[0, :, 0:POOL_DIM], preferred_element_type=F32)
        gz = jnp.dot(h, w_in_ref[0, :, POOL_DIM:], preferred_element_type=F32)
        gz_a = gz[:, 0:D_MODEL]
        gz_b = gz[:, D_MODEL:]

        uext_ref[POOL_HALO + r0:POOL_HALO + r0 + MIX_SUB, :] = u
        head_pos = si * tm + r0 + lax.broadcasted_iota(jnp.int32, (POOL_HALO, 1), 0)
        pooled = []
        for g, w in enumerate(POOL_WINDOWS):
            eg = uext_ref[r0:r0 + POOL_HALO + MIX_SUB,
                          g * POOL_GROUP_DIM:(g + 1) * POOL_GROUP_DIM]
            win = eg
            k = 1
            while k < w:
                win = win + pltpu.roll(win, k, axis=0)
                k *= 2
            win = win[POOL_HALO:]
            ug = eg[POOL_HALO:]
            inv_head = 1.0 / jnp.minimum(head_pos + 1, w).astype(F32)
            mean = jnp.concatenate([win[:POOL_HALO] * inv_head, win[POOL_HALO:] * (1.0 / w)], axis=0)
            yg = jnp.dot((mean - ug).astype(BF16), w_pool_ref[0, g], preferred_element_type=F32)
            pooled.append(yg)
        y_pool = jnp.concatenate(pooled, axis=-1) * pscale_ref[layer:layer + 1, :]
        y_a = jnp.dot(y_pool.astype(BF16), p_pool_ref[0], preferred_element_type=F32)

        merged = _sigmoid(gz_a) * y_a + _sigmoid(gz_b) * y_b
        x1 = x + gate1 * jnp.dot(merged.astype(BF16), w_out_ref[0], preferred_element_type=F32)

        h2 = ((_rms(x1) * g2_ref[layer:layer + 1, :]) * (1.0 + scale2) + shift2).astype(BF16)
        hidden = [jnp.square(jnp.maximum(
            jnp.dot(h2, w_ff1_ref[0, :, c0:c0 + FF_CHUNK], preferred_element_type=F32), 0.0)).astype(BF16)
            for c0 in range(0, D_FF, FF_CHUNK)]
        ff = jnp.dot(jnp.concatenate(hidden, axis=-1), w_ff2_ref[0], preferred_element_type=F32)
        x2 = x1 + gate2 * ff
        if final:
            x2 = _rms(x2) * gf_ref[...]
        o_ref[0, tok, :] = x2


def _mix(x, ot, mod, wts, layer_wts, layer, *, final):
    b, s, d = x.shape
    tm = TM_MIX
    once = dict(pipeline_mode=pl.Buffered(1))
    names = ["ln1_g", "ln2_g", "final_g", "w_in_b", "w_pool", "pool_scale", "p_pool", "p_attn", "w_out",
             "w_ff1", "w_ff2"]
    whole = {"ln1_g", "ln2_g", "final_g", "pool_scale"}
    wts = {**wts, **layer_wts}
    specs = [_whole_spec(wts[n], **once) if n in whole
             else _layer_spec(wts[n], 0 if n in layer_wts else layer, **once)
             for n in names]
    return pl.pallas_call(
        functools.partial(_mix_kernel, final=final, layer=layer),
        grid=(b, s // tm),
        in_specs=[
            pl.BlockSpec((1, tm, d), lambda i, j: (i, j, 0)),
            pl.BlockSpec((1, ATTN_DIM, tm), lambda i, j: (i, 0, j)),
            _layer_spec(mod, layer, **once),
        ] + specs,
        out_specs=pl.BlockSpec((1, tm, d), lambda i, j: (i, j, 0)),
        out_shape=jax.ShapeDtypeStruct((b, s, d), F32),
        scratch_shapes=[pltpu.VMEM((tm + POOL_HALO, POOL_DIM), F32)],
        compiler_params=pltpu.CompilerParams(
            dimension_semantics=("arbitrary", "arbitrary"),
            vmem_limit_bytes=VMEM_LIMIT_BYTES),
        name="mix_mlp",
    )(x, ot, mod, *[wts[n] for n in names])


W_IN_PREP_ROWS = 256


def _w_in_prep_kernel(w_ref, a_ref, b_ref):
    wt = w_ref[0]
    rows = wt.shape[1]
    c0 = POOL_DIM
    c2 = c0 + Q_LORA + KV_LORA
    c3 = c2 + QK_ROPE
    half = QK_ROPE // 2
    kr = wt[c2:c3]
    zl = jnp.zeros((ROPE_LO, rows), F32)
    zr = jnp.zeros((HEAD_PAD - ROPE_HI, rows), F32)
    a_t = jnp.concatenate([wt[c0:c2], zl, kr, zr, zl, -kr[half:], kr[:half], zr], axis=0)
    a_ref[0] = a_t.T.astype(BF16)
    b_ref[0] = jnp.concatenate([wt[:c0], wt[c3:]], axis=0).T.astype(BF16)


def _w_in_prep(w_in):
    depth, d, n = w_in.shape
    na = Q_LORA + KV_LORA + 2 * HEAD_PAD
    nb = n - (Q_LORA + KV_LORA + QK_ROPE)
    rows = W_IN_PREP_ROWS
    w_in = jnp.swapaxes(w_in, 1, 2)
    return pl.pallas_call(
        _w_in_prep_kernel,
        grid=(depth, d // rows),
        in_specs=[pl.BlockSpec((1, n, rows), lambda l, r: (l, 0, r))],
        out_specs=[pl.BlockSpec((1, rows, na), lambda l, r: (l, r, 0)),
                   pl.BlockSpec((1, rows, nb), lambda l, r: (l, r, 0))],
        out_shape=[jax.ShapeDtypeStruct((depth, d, na), BF16), jax.ShapeDtypeStruct((depth, d, nb), BF16)],
        compiler_params=pltpu.CompilerParams(vmem_limit_bytes=VMEM_LIMIT_BYTES),
        name="w_in_prep",
    )(w_in)


def _prep_weights(ln1_g, ln2_g, w_in, q_norm_g, w_uq, kv_norm_g, w_uk, w_uv, w_pool, pool_scale, final_g):
    depth = w_in.shape[0]
    half = QK_ROPE // 2
    w_in_a, w_in_b = _w_in_prep(w_in)

    pad_q = HEAD_PAD - (QK_NOPE + QK_ROPE)
    wuq_pad = jnp.pad(w_uq, ((0, 0), (0, 0), (0, 0), (0, pad_q)))
    wuqt = wuq_pad.reshape(depth, Q_LORA, N_HEADS * HEAD_PAD).transpose(0, 2, 1).astype(BF16)
    r = w_uq[..., QK_NOPE:]
    rot = jnp.concatenate([-r[..., half:], r[..., :half]], axis=-1)
    wuqrt = rot.reshape(depth, Q_LORA, N_HEADS * QK_ROPE).transpose(0, 2, 1).astype(BF16)
    wuk = jnp.pad(w_uk, ((0, 0), (0, 0), (0, 0), (0, HEAD_PAD - QK_NOPE)))
    wuk = wuk.reshape(depth, KV_LORA, N_HEADS * HEAD_PAD).astype(BF16)
    wuvt = w_uv.reshape(depth, KV_LORA, ATTN_DIM).transpose(0, 2, 1).astype(BF16)
    return dict(
        ln1_g=ln1_g, ln2_g=ln2_g, final_g=final_g.reshape(1, D_MODEL), q_norm_g=q_norm_g,
        kv_norm_g=kv_norm_g, pool_scale=pool_scale, w_in_a=w_in_a, w_in_b=w_in_b, wuqt=wuqt, wuqrt=wuqrt,
        wuk=wuk, wuvt=wuvt, w_pool=w_pool.astype(BF16))


def kernel(x, c, positions, ln1_g, ln2_g, w_ada, b_ada, w_in, q_norm_g, w_uq, kv_norm_g, w_uk,
           w_uv, w_pool, pool_scale, p_pool, p_attn, w_out, w_ff1, w_ff2, final_g):
    depth = w_in.shape[0]
    mod = _modulation(c, w_ada, b_ada)
    wts = _prep_weights(ln1_g, ln2_g, w_in, q_norm_g, w_uq, kv_norm_g, w_uk, w_uv, w_pool, pool_scale,
                        final_g)
    cast_names = ("p_pool", "p_attn", "w_out", "w_ff1", "w_ff2")
    f32_weights = (p_pool, p_attn, w_out, w_ff1, w_ff2)
    for layer in range(depth):
        qt, k, vt = _qkv(x, mod, positions, wts, layer)
        ot, cast = _attention(qt, k, vt, f32_weights, layer)
        x = _mix(x, ot, mod, wts, dict(zip(cast_names, cast)), layer, final=(layer == depth - 1))
    return x
```

```python
import functools
import math

import jax
import jax.numpy as jnp
from jax import lax
from jax.experimental import pallas as pl
from jax.experimental.pallas import tpu as pltpu

D_MODEL = 1024
N_HEADS = 8
QK_NOPE = 64
QK_ROPE = 32
V_DIM = 64
Q_LORA = 384
KV_LORA = 256
POOL_WINDOWS = (2, 4, 8, 16)
POOL_GROUP_DIM = 128
POOL_DIM = len(POOL_WINDOWS) * POOL_GROUP_DIM
ATTN_DIM = N_HEADS * V_DIM
D_FF = 4 * D_MODEL
N_MOD = 6
EPS = 1e-6
ROPE_THETA = 10000.0

HEAD_PAD = 128
ROPE_LO = QK_NOPE
ROPE_HI = QK_NOPE + QK_ROPE
POOL_HALO = 16

VMEM_LIMIT_BYTES = 56 * 1024 * 1024

F32 = jnp.float32
BF16 = jnp.bfloat16

TM_QKV = 1024
QKV_SUB = 512
TM_MIX = 1024
MIX_SUB = 512
TQ = 512
TK = 512
ATTN_HEADS = 4
BELOW_PAIRS_PER_TRIP = 2
F32_SUBLANES = 8
BF16_SUBLANES = 16
ACC_ROWS = V_DIM + BF16_SUBLANES
FF_CHUNK = 1024
MOD_TN = 1536


def _nt_dot(a, b):
    return lax.dot_general(a, b, (((1,), (1,)), ((), ())), preferred_element_type=F32)


def _tn_dot(a, b):
    return lax.dot_general(a, b, (((0,), (0,)), ((), ())), preferred_element_type=F32)


def _rms(x):
    return x * lax.rsqrt(jnp.mean(x * x, axis=-1, keepdims=True) + EPS)


def _sigmoid(x):
    return 1.0 / (1.0 + jnp.exp(-x))


def _mod_rows(mod_ref, chunks):
    row = pl.ds(pl.program_id(0), 1)
    return tuple(mod_ref[0, row, c * D_MODEL:(c + 1) * D_MODEL] for c in chunks)


def _layer_spec(arr, layer, **kw):
    tail = (0,) * (arr.ndim - 1)
    return pl.BlockSpec((1,) + arr.shape[1:], lambda i, j: (layer,) + tail, **kw)


def _whole_spec(arr, **kw):
    zeros = (0,) * arr.ndim
    return pl.BlockSpec(arr.shape, lambda i, j: zeros, **kw)


def _mod_kernel(c_ref, w_ref, b_ref, o_ref):
    c = c_ref[...]
    c_act = c * _sigmoid(c)
    o_ref[0] = jnp.dot(c_act.astype(BF16), w_ref[0].astype(BF16),
                       preferred_element_type=F32) + b_ref[0]


def _modulation(c, w_ada, b_ada):
    depth, d, n = w_ada.shape
    b = c.shape[0]
    rows = F32_SUBLANES
    assert b <= rows
    c_pad = jnp.pad(c, ((0, rows - b), (0, 0)))
    out = pl.pallas_call(
        _mod_kernel,
        grid=(depth, n // MOD_TN),
        in_specs=[
            pl.BlockSpec((rows, d), lambda l, j: (0, 0)),
            pl.BlockSpec((1, d, MOD_TN), lambda l, j: (l, 0, j)),
            pl.BlockSpec((1, 1, MOD_TN), lambda l, j: (l, 0, j)),
        ],
        out_specs=pl.BlockSpec((1, rows, MOD_TN), lambda l, j: (l, 0, j)),
        out_shape=jax.ShapeDtypeStruct((depth, rows, n), F32),
        compiler_params=pltpu.CompilerParams(vmem_limit_bytes=VMEM_LIMIT_BYTES),
        name="adaln_mod",
    )(c_pad, w_ada, b_ada.reshape(depth, 1, n))
    return out


def _qkv_kernel(x_ref, mod_ref, g_ref, w_in_ref, gq_ref, gkv_ref, wuqt_ref, wuqrt_ref,
                wuk_ref, wuvt_ref, pos_ref, invf_ref,
                qt_ref, k_ref, vt_ref, *, scale, layer):
    shift, scl = _mod_rows(mod_ref, (0, 1))
    toks = [slice(r0, r0 + QKV_SUB) for r0 in range(0, x_ref.shape[1], QKV_SUB)]

    hs = [((_rms(x_ref[0, tok, :]) * g_ref[layer:layer + 1, :]) * (1.0 + scl) + shift).astype(BF16)
          for tok in toks]
    zs = [jnp.dot(h, w_in_ref[0], preferred_element_type=F32) for h in hs]

    latents = []
    for z in zs:
        c_q = z[:, 0:Q_LORA]
        c_kv = z[:, Q_LORA:Q_LORA + KV_LORA]
        latents.append(((_rms(c_q) * gq_ref[layer:layer + 1, :]).astype(BF16),
                        (_rms(c_kv) * gkv_ref[layer:layer + 1, :]).astype(BF16)))

    prods = [(_nt_dot(wuqt_ref[0], cqn), _nt_dot(wuqrt_ref[0], cqn),
              jnp.dot(ckvn, wuk_ref[0], preferred_element_type=F32), _nt_dot(wuvt_ref[0], ckvn))
             for cqn, ckvn in latents]

    z_lo = jnp.zeros((ROPE_LO, QKV_SUB), F32)
    z_hi = jnp.zeros((HEAD_PAD - ROPE_HI, QKV_SUB), F32)
    for tok, z, (qt, qt_rot, k, vt) in zip(toks, zs, prods):
        ang = invf_ref[...] * pos_ref[0, :, tok].astype(F32)
        cost = jnp.cos(ang)
        sint = jnp.sin(ang)
        for hd in range(N_HEADS):
            base = hd * HEAD_PAD
            qt_ref[0, base:base + ROPE_LO, tok] = (qt[base:base + ROPE_LO] * scale).astype(BF16)
            roped = (qt[base + ROPE_LO:base + ROPE_HI] * cost
                     + qt_rot[hd * QK_ROPE:(hd + 1) * QK_ROPE] * sint)
            qt_ref[0, base + ROPE_LO:base + ROPE_HI, tok] = (roped * scale).astype(BF16)
            qt_ref[0, base + ROPE_HI:base + HEAD_PAD, tok] = (
                qt[base + ROPE_HI:base + HEAD_PAD] * scale).astype(BF16)

        kr = z[:, Q_LORA + KV_LORA:Q_LORA + KV_LORA + HEAD_PAD]
        kr_rot = z[:, Q_LORA + KV_LORA + HEAD_PAD:]
        cos_tok = jnp.concatenate([z_lo, cost, z_hi], axis=0).T
        sin_tok = jnp.concatenate([z_lo, sint, z_hi], axis=0).T
        kr_full = kr * cos_tok + kr_rot * sin_tok
        for hd in range(N_HEADS):
            base = hd * HEAD_PAD
            k_ref[0, tok, base:base + HEAD_PAD] = (k[:, base:base + HEAD_PAD] + kr_full).astype(BF16)

        vt_ref[0, :, tok] = vt.astype(BF16)


def _qkv(x, mod, positions, wts, layer):
    b, s, d = x.shape
    tm = TM_QKV
    inv_freq = ROPE_THETA ** (-jnp.arange(0, QK_ROPE, 2, dtype=F32) / QK_ROPE)
    invf_col = jnp.concatenate([inv_freq, inv_freq]).reshape(QK_ROPE, 1)
    scale = math.log2(math.e) / math.sqrt(QK_NOPE + QK_ROPE)
    stacked = [wts["ln1_g"], wts["w_in_a"], wts["q_norm_g"], wts["kv_norm_g"], wts["wuqt"], wts["wuqrt"],
               wts["wuk"], wts["wuvt"]]
    return pl.pallas_call(
        functools.partial(_qkv_kernel, scale=scale, layer=layer),
        grid=(b, s // tm),
        in_specs=[
            pl.BlockSpec((1, tm, d), lambda i, j: (i, j, 0)),
            _layer_spec(mod, layer),
            _whole_spec(wts["ln1_g"]),
            _layer_spec(wts["w_in_a"], layer),
            _whole_spec(wts["q_norm_g"]),
            _whole_spec(wts["kv_norm_g"]),
            _layer_spec(wts["wuqt"], layer),
            _layer_spec(wts["wuqrt"], layer),
            _layer_spec(wts["wuk"], layer),
            _layer_spec(wts["wuvt"], layer),
            pl.BlockSpec((1, 1, tm), lambda i, j: (i, 0, j)),
            _whole_spec(invf_col),
        ],
        out_specs=[
            pl.BlockSpec((1, N_HEADS * HEAD_PAD, tm), lambda i, j: (i, 0, j)),
            pl.BlockSpec((1, tm, N_HEADS * HEAD_PAD), lambda i, j: (i, j, 0)),
            pl.BlockSpec((1, ATTN_DIM, tm), lambda i, j: (i, 0, j)),
        ],
        out_shape=[
            jax.ShapeDtypeStruct((b, N_HEADS * HEAD_PAD, s), BF16),
            jax.ShapeDtypeStruct((b, s, N_HEADS * HEAD_PAD), BF16),
            jax.ShapeDtypeStruct((b, ATTN_DIM, s), BF16),
        ],
        compiler_params=pltpu.CompilerParams(
            dimension_semantics=("arbitrary", "arbitrary"),
            vmem_limit_bytes=VMEM_LIMIT_BYTES),
        name="qkv_proj",
    )(x, mod, *stacked, positions.reshape(b, 1, s), invf_col)


def _attn_kernel(qt_ref, k_ref, vt_ref, *refs, n_q, n_cast):
    cast_in, o_ref, cast_out = refs[:n_cast], refs[n_cast], refs[n_cast + 1:2 * n_cast + 1]
    s_ref, acc_ref, m_ref = refs[2 * n_cast + 1:]
    for src, dst in zip(cast_in, cast_out):
        dst[...] = src[...].astype(BF16)

    n_below = n_q * (n_q - 1) // 2
    hq = TQ // 2
    ones_rows = jnp.ones((ACC_ROWS - V_DIM, TK), BF16)
    half_mask = (lax.broadcasted_iota(jnp.int32, (hq, hq), 1)
                 >= lax.broadcasted_iota(jnp.int32, (hq, hq), 0))

    acc_ref[...] = jnp.zeros(acc_ref.shape, F32)
    m_ref[...] = jnp.full(m_ref.shape, jnp.finfo(F32).min, F32)

    def keys(hd, off, n):
        return k_ref[0, pl.ds(off, n), hd * HEAD_PAD:(hd + 1) * HEAD_PAD]

    def queries(hd, off, n):
        return qt_ref[0, hd * HEAD_PAD:(hd + 1) * HEAD_PAD, pl.ds(off, n)]

    def values(hd, off, n):
        return jnp.concatenate(
            [vt_ref[0, hd * V_DIM:(hd + 1) * V_DIM, pl.ds(off, n)], ones_rows[:, :n]], axis=0)

    def col_max(s):
        return jnp.max(s, axis=0, keepdims=True)

    def produce_below(slot, hd, qi, j):
        s = jnp.dot(keys(hd, pl.multiple_of(j * TK, TK), TK), queries(hd, pl.multiple_of(qi * TQ, TQ), TQ),
                    preferred_element_type=F32)
        s_ref[slot, hd] = s
        return col_max(s)

    def consume_below(slot, hd, qi, j, tile_max):
        m = m_ref[qi, hd]
        m_new = jnp.maximum(m, tile_max)
        p = jnp.exp2(s_ref[slot, hd] - m_new).astype(BF16)
        acc_ref[qi, hd] = (jnp.exp2(m - m_new) * acc_ref[qi, hd]
                           + jnp.dot(values(hd, pl.multiple_of(j * TK, TK), TK), p, preferred_element_type=F32))
        m_ref[qi, hd] = m_new

    def produce_diag(slot, hd, d):
        off = pl.multiple_of(d * TQ, TQ)
        s_ref[slot, hd, 0:hq, :] = jnp.dot(keys(hd, off, hq), queries(hd, off, TQ),
                                           preferred_element_type=F32)
        s_ref[slot, hd, hq:TK, hq:TQ] = jnp.dot(keys(hd, off + hq, hq), queries(hd, off + hq, hq),
                                                preferred_element_type=F32)

    def consume_diag(slot, hd, d):
        off = pl.multiple_of(d * TQ, TQ)
        neg = jnp.finfo(F32).min
        s_tl = jnp.where(half_mask, s_ref[slot, hd, 0:hq, 0:hq], neg)
        s_tr = s_ref[slot, hd, 0:hq, hq:TQ]
        s_br = jnp.where(half_mask, s_ref[slot, hd, hq:TK, hq:TQ], neg)
        m = m_ref[d, hd]
        m_l = jnp.maximum(m[:, 0:hq], col_max(s_tl))
        m_r = jnp.maximum(m[:, hq:TQ], jnp.maximum(col_max(s_tr), col_max(s_br)))
        p_l = jnp.exp2(s_tl - m_l).astype(BF16)
        p_r = jnp.concatenate([jnp.exp2(s_tr - m_r), jnp.exp2(s_br - m_r)], axis=0).astype(BF16)
        acc = acc_ref[d, hd]
        acc_l = (jnp.exp2(m[:, 0:hq] - m_l) * acc[:, 0:hq]
                 + jnp.dot(values(hd, off, hq), p_l, preferred_element_type=F32))
        acc_r = (jnp.exp2(m[:, hq:TQ] - m_r) * acc[:, hq:TQ]
                 + jnp.dot(values(hd, off, TK), p_r, preferred_element_type=F32))
        rows = slice(hd * V_DIM, (hd + 1) * V_DIM)
        o_ref[0, rows, pl.ds(off, hq)] = (
            acc_l[0:V_DIM] * (1.0 / acc_l[V_DIM:V_DIM + 1])).astype(o_ref.dtype)
        o_ref[0, rows, pl.ds(off + hq, hq)] = (
            acc_r[0:V_DIM] * (1.0 / acc_r[V_DIM:V_DIM + 1])).astype(o_ref.dtype)

    heads = range(ATTN_HEADS)

    def next_below(qi, j):
        row_end = j + 1 == qi
        return jnp.where(row_end, qi + 1, qi), jnp.where(row_end, 0, j + 1)

    def below_step(slot, tile, maxes, to_diag=False):
        nxt = next_below(*tile)
        new_maxes = []
        for hd in heads:
            if to_diag:
                produce_diag(1 - slot, hd, 0)
            else:
                new_maxes.append(produce_below(1 - slot, hd, *nxt))
            consume_below(slot, hd, *tile, maxes[hd])
        return nxt, tuple(new_maxes)

    def diag_step(slot, d, produce_next=True):
        for hd in heads:
            if produce_next:
                produce_diag(1 - slot, hd, d + 1)
            consume_diag(slot, hd, d)
        return d + 1

    tile0 = (jnp.int32(1), jnp.int32(0))
    maxes0 = tuple(produce_below(0, hd, *tile0) for hd in heads)

    def below_pair(carry):
        tile, maxes = carry
        tile, maxes = below_step(0, tile, maxes)
        return below_step(1, tile, maxes)

    def below_trip(_, carry):
        for _ in range(BELOW_PAIRS_PER_TRIP):
            carry = below_pair(carry)
        return carry

    n_pairs = n_below // 2 - 1
    carry = lax.fori_loop(0, n_pairs // BELOW_PAIRS_PER_TRIP, below_trip, (tile0, maxes0))
    for _ in range(n_pairs % BELOW_PAIRS_PER_TRIP):
        carry = below_pair(carry)
    tile, maxes = below_step(0, *carry)
    below_step(1, tile, maxes, to_diag=True)

    def diag_pair(_, d):
        return diag_step(1, diag_step(0, d))

    d = lax.fori_loop(0, n_q // 2 - 1, diag_pair, jnp.int32(0))
    d = diag_step(0, d)
    diag_step(1, d, produce_next=False)


def _attention(qt, k, vt, f32_weights, layer):
    b, _, s = qt.shape
    hb = ATTN_HEADS
    groups = N_HEADS // hb
    n_steps = b * groups
    n_q = s // TQ
    assert n_q % 2 == 0 and (n_q * (n_q - 1) // 2) % 2 == 0, "both tile streams run two tiles per loop trip"

    def rows_per_step(w):
        rows = w.shape[1] // n_steps
        assert rows * n_steps == w.shape[1] and rows % BF16_SUBLANES == 0, w.shape
        return rows

    cast_in = [pl.BlockSpec((1, rows_per_step(w), w.shape[2]), lambda i, h: (layer, i * groups + h, 0))
               for w in f32_weights]
    cast_out = [pl.BlockSpec((1, rows_per_step(w), w.shape[2]), lambda i, h: (0, i * groups + h, 0))
                for w in f32_weights]
    out = pl.pallas_call(
        functools.partial(_attn_kernel, n_q=n_q, n_cast=len(f32_weights)),
        grid=(b, groups),
        in_specs=[
            pl.BlockSpec((1, hb * HEAD_PAD, s), lambda i, h: (i, h, 0)),
            pl.BlockSpec((1, s, hb * HEAD_PAD), lambda i, h: (i, 0, h)),
            pl.BlockSpec((1, hb * V_DIM, s), lambda i, h: (i, h, 0)),
        ] + cast_in,
        out_specs=[pl.BlockSpec((1, hb * V_DIM, s), lambda i, h: (i, h, 0))] + cast_out,
        out_shape=[jax.ShapeDtypeStruct((b, ATTN_DIM, s), BF16)]
        + [jax.ShapeDtypeStruct((1,) + w.shape[1:], BF16) for w in f32_weights],
        scratch_shapes=[
            pltpu.VMEM((2, hb, TK, TQ), F32),
            pltpu.VMEM((n_q, hb, ACC_ROWS, TQ), F32),
            pltpu.VMEM((n_q, hb, 1, TQ), F32),
        ],
        compiler_params=pltpu.CompilerParams(
            dimension_semantics=("arbitrary", "arbitrary"),
            vmem_limit_bytes=VMEM_LIMIT_BYTES),
        name="mla_attention",
    )(qt, k, vt, *f32_weights)
    return out[0], out[1:]


def _mix_kernel(x_ref, ot_ref, mod_ref, g1_ref, g2_ref, gf_ref, w_in_ref, w_pool_ref, pscale_ref,
                p_pool_ref, p_attn_ref, w_out_ref, w_ff1_ref, w_ff2_ref,
                o_ref, uext_ref, *, final, layer):
    tm = x_ref.shape[1]
    si = pl.program_id(1)

    @pl.when(si == 0)
    def _():
        uext_ref[0:POOL_HALO, :] = jnp.zeros((POOL_HALO, POOL_DIM), F32)

    @pl.when(si > 0)
    def _():
        uext_ref[0:POOL_HALO, :] = uext_ref[tm:tm + POOL_HALO, :]

    shift1, scale1, gate1, shift2, scale2, gate2 = _mod_rows(mod_ref, range(N_MOD))

    for r0 in range(0, tm, MIX_SUB):
        tok = slice(r0, r0 + MIX_SUB)
        x = x_ref[0, tok, :]

        y_b = _tn_dot(ot_ref[0, :, tok], p_attn_ref[0])
        h = ((_rms(x) * g1_ref[layer:layer + 1, :]) * (1.0 + scale1) + shift1).astype(BF16)
        u = jnp.dot(h, w_in_ref[0, :, 0:POOL_DIM], preferred_element_type=F32)
        gz = jnp.dot(h, w_in_ref[0, :, POOL_DIM:], preferred_element_type=F32)
        gz_a = gz[:, 0:D_MODEL]
        gz_b = gz[:, D_MODEL:]

        uext_ref[POOL_HALO + r0:POOL_HALO + r0 + MIX_SUB, :] = u
        head_pos = si * tm + r0 + lax.broadcasted_iota(jnp.int32, (POOL_HALO, 1), 0)
        pooled = []
        for g, w in enumerate(POOL_WINDOWS):
            eg = uext_ref[r0:r0 + POOL_HALO + MIX_SUB,
                          g * POOL_GROUP_DIM:(g + 1) * POOL_GROUP_DIM]
            win = eg
            k = 1
            while k < w:
                win = win + pltpu.roll(win, k, axis=0)
                k *= 2
            win = win[POOL_HALO:]
            ug = eg[POOL_HALO:]
            inv_head = 1.0 / jnp.minimum(head_pos + 1, w).astype(F32)
            mean = jnp.concatenate([win[:POOL_HALO] * inv_head, win[POOL_HALO:] * (1.0 / w)], axis=0)
            yg = jnp.dot((mean - ug).astype(BF16), w_pool_ref[0, g], preferred_element_type=F32)
            pooled.append(yg)
        y_pool = jnp.concatenate(pooled, axis=-1) * pscale_ref[layer:layer + 1, :]
        y_a = jnp.dot(y_pool.astype(BF16), p_pool_ref[0], preferred_element_type=F32)

        merged = _sigmoid(gz_a) * y_a + _sigmoid(gz_b) * y_b
        x1 = x + gate1 * jnp.dot(merged.astype(BF16), w_out_ref[0], preferred_element_type=F32)

        h2 = ((_rms(x1) * g2_ref[layer:layer + 1, :]) * (1.0 + scale2) + shift2).astype(BF16)
        hidden = [jnp.square(jnp.maximum(
            jnp.dot(h2, w_ff1_ref[0, :, c0:c0 + FF_CHUNK], preferred_element_type=F32), 0.0)).astype(BF16)
            for c0 in range(0, D_FF, FF_CHUNK)]
        ff = jnp.dot(jnp.concatenate(hidden, axis=-1), w_ff2_ref[0], preferred_element_type=F32)
        x2 = x1 + gate2 * ff
        if final:
            x2 = _rms(x2) * gf_ref[...]
        o_ref[0, tok, :] = x2


def _mix(x, ot, mod, wts, layer_wts, layer, *, final):
    b, s, d = x.shape
    tm = TM_MIX
    once = dict(pipeline_mode=pl.Buffered(1))
    names = ["ln1_g", "ln2_g", "final_g", "w_in_b", "w_pool", "pool_scale", "p_pool", "p_attn", "w_out",
             "w_ff1", "w_ff2"]
    whole = {"ln1_g", "ln2_g", "final_g", "pool_scale"}
    wts = {**wts, **layer_wts}
    specs = [_whole_spec(wts[n], **once) if n in whole
             else _layer_spec(wts[n], 0 if n in layer_wts else layer, **once)
             for n in names]
    return pl.pallas_call(
        functools.partial(_mix_kernel, final=final, layer=layer),
        grid=(b, s // tm),
        in_specs=[
            pl.BlockSpec((1, tm, d), lambda i, j: (i, j, 0)),
            pl.BlockSpec((1, ATTN_DIM, tm), lambda i, j: (i, 0, j)),
            _layer_spec(mod, layer, **once),
        ] + specs,
        out_specs=pl.BlockSpec((1, tm, d), lambda i, j: (i, j, 0)),
        out_shape=jax.ShapeDtypeStruct((b, s, d), F32),
        scratch_shapes=[pltpu.VMEM((tm + POOL_HALO, POOL_DIM), F32)],
        compiler_params=pltpu.CompilerParams(
            dimension_semantics=("arbitrary", "arbitrary"),
            vmem_limit_bytes=VMEM_LIMIT_BYTES),
        name="mix_mlp",
    )(x, ot, mod, *[wts[n] for n in names])


W_IN_PREP_ROWS = 256


def _w_in_prep_kernel(w_ref, a_ref, b_ref):
    wt = w_ref[0]
    rows = wt.shape[1]
    c0 = POOL_DIM
    c2 = c0 + Q_LORA + KV_LORA
    c3 = c2 + QK_ROPE
    half = QK_ROPE // 2
    kr = wt[c2:c3]
    zl = jnp.zeros((ROPE_LO, rows), F32)
    zr = jnp.zeros((HEAD_PAD - ROPE_HI, rows), F32)
    a_t = jnp.concatenate([wt[c0:c2], zl, kr, zr, zl, -kr[half:], kr[:half], zr], axis=0)
    a_ref[0] = a_t.T.astype(BF16)
    b_ref[0] = jnp.concatenate([wt[:c0], wt[c3:]], axis=0).T.astype(BF16)


def _w_in_prep(w_in):
    depth, d, n = w_in.shape
    na = Q_LORA + KV_LORA + 2 * HEAD_PAD
    nb = n - (Q_LORA + KV_LORA + QK_ROPE)
    rows = W_IN_PREP_ROWS
    w_in = jnp.swapaxes(w_in, 1, 2)
    return pl.pallas_call(
        _w_in_prep_kernel,
        grid=(depth, d // rows),
        in_specs=[pl.BlockSpec((1, n, rows), lambda l, r: (l, 0, r))],
        out_specs=[pl.BlockSpec((1, rows, na), lambda l, r: (l, r, 0)),
                   pl.BlockSpec((1, rows, nb), lambda l, r: (l, r, 0))],
        out_shape=[jax.ShapeDtypeStruct((depth, d, na), BF16), jax.ShapeDtypeStruct((depth, d, nb), BF16)],
        compiler_params=pltpu.CompilerParams(vmem_limit_bytes=VMEM_LIMIT_BYTES),
        name="w_in_prep",
    )(w_in)


def _prep_weights(ln1_g, ln2_g, w_in, q_norm_g, w_uq, kv_norm_g, w_uk, w_uv, w_pool, pool_scale, final_g):
    depth = w_in.shape[0]
    half = QK_ROPE // 2
    w_in_a, w_in_b = _w_in_prep(w_in)

    pad_q = HEAD_PAD - (QK_NOPE + QK_ROPE)
    wuq_pad = jnp.pad(w_uq, ((0, 0), (0, 0), (0, 0), (0, pad_q)))
    wuqt = wuq_pad.reshape(depth, Q_LORA, N_HEADS * HEAD_PAD).transpose(0, 2, 1).astype(BF16)
    r = w_uq[..., QK_NOPE:]
    rot = jnp.concatenate([-r[..., half:], r[..., :half]], axis=-1)
    wuqrt = rot.reshape(depth, Q_LORA, N_HEADS * QK_ROPE).transpose(0, 2, 1).astype(BF16)
    wuk = jnp.pad(w_uk, ((0, 0), (0, 0), (0, 0), (0, HEAD_PAD - QK_NOPE)))
    wuk = wuk.reshape(depth, KV_LORA, N_HEADS * HEAD_PAD).astype(BF16)
    wuvt = w_uv.reshape(depth, KV_LORA, ATTN_DIM).transpose(0, 2, 1).astype(BF16)
    return dict(
        ln1_g=ln1_g, ln2_g=ln2_g, final_g=final_g.reshape(1, D_MODEL), q_norm_g=q_norm_g,
        kv_norm_g=kv_norm_g, pool_scale=pool_scale, w_in_a=w_in_a, w_in_b=w_in_b, wuqt=wuqt, wuqrt=wuqrt,
        wuk=wuk, wuvt=wuvt, w_pool=w_pool.astype(BF16))


def kernel(x, c, positions, ln1_g, ln2_g, w_ada, b_ada, w_in, q_norm_g, w_uq, kv_norm_g, w_uk,
           w_uv, w_pool, pool_scale, p_pool, p_attn, w_out, w_ff1, w_ff2, final_g):
    depth = w_in.shape[0]
    mod = _modulation(c, w_ada, b_ada)
    wts = _prep_weights(ln1_g, ln2_g, w_in, q_norm_g, w_uq, kv_norm_g, w_uk, w_uv, w_pool, pool_scale,
                        final_g)
    cast_names = ("p_pool", "p_attn", "w_out", "w_ff1", "w_ff2")
    f32_weights = (p_pool, p_attn, w_out, w_ff1, w_ff2)
    for layer in range(depth):
        qt, k, vt = _qkv(x, mod, positions, wts, layer)
        ot, cast = _attention(qt, k, vt, f32_weights, layer)
        x = _mix(x, ot, mod, wts, dict(zip(cast_names, cast)), layer, final=(layer == depth - 1))
    return x
```

```python
import functools
import math

import jax
import jax.numpy as jnp
from jax import lax
from jax.experimental import pallas as pl
from jax.experimental.pallas import tpu as pltpu

D_MODEL = 1024
N_HEADS = 8
QK_NOPE = 64
QK_ROPE = 32
V_DIM = 64
Q_LORA = 384
KV_LORA = 256
POOL_WINDOWS = (2, 4, 8, 16)
POOL_GROUP_DIM = 128
POOL_DIM = len(POOL_WINDOWS) * POOL_GROUP_DIM
ATTN_DIM = N_HEADS * V_DIM
D_FF = 4 * D_MODEL
N_MOD = 6
EPS = 1e-6
ROPE_THETA = 10000.0

HEAD_PAD = 128
ROPE_LO = QK_NOPE
ROPE_HI = QK_NOPE + QK_ROPE
POOL_HALO = 16

VMEM_LIMIT_BYTES = 56 * 1024 * 1024

F32 = jnp.float32
BF16 = jnp.bfloat16

TM_QKV = 1024
QKV_SUB = 512
TM_MIX = 1024
MIX_SUB = 512
TQ = 512
TK = 512
ATTN_HEADS = 4
BELOW_PAIRS_PER_TRIP = 2
F32_SUBLANES = 8
BF16_SUBLANES = 16
ACC_ROWS = V_DIM + BF16_SUBLANES
FF_CHUNK = 1024
MOD_TN = 1536


def _nt_dot(a, b):
    return lax.dot_general(a, b, (((1,), (1,)), ((), ())), preferred_element_type=F32)


def _tn_dot(a, b):
    return lax.dot_general(a, b, (((0,), (0,)), ((), ())), preferred_element_type=F32)


def _rms(x):
    return x * lax.rsqrt(jnp.mean(x * x, axis=-1, keepdims=True) + EPS)


def _sigmoid(x):
    return 1.0 / (1.0 + jnp.exp(-x))


def _mod_rows(mod_ref, chunks):
    row = pl.ds(pl.program_id(0), 1)
    return tuple(mod_ref[0, row, c * D_MODEL:(c + 1) * D_MODEL] for c in chunks)


def _layer_spec(arr, layer, **kw):
    tail = (0,) * (arr.ndim - 1)
    return pl.BlockSpec((1,) + arr.shape[1:], lambda i, j: (layer,) + tail, **kw)


def _whole_spec(arr, **kw):
    zeros = (0,) * arr.ndim
    return pl.BlockSpec(arr.shape, lambda i, j: zeros, **kw)


def _mod_kernel(c_ref, w_ref, b_ref, o_ref):
    c = c_ref[...]
    c_act = c * _sigmoid(c)
    o_ref[0] = jnp.dot(c_act.astype(BF16), w_ref[0].astype(BF16),
                       preferred_element_type=F32) + b_ref[0]


def _modulation(c, w_ada, b_ada):
    depth, d, n = w_ada.shape
    b = c.shape[0]
    rows = F32_SUBLANES
    assert b <= rows
    c_pad = jnp.pad(c, ((0, rows - b), (0, 0)))
    out = pl.pallas_call(
        _mod_kernel,
        grid=(depth, n // MOD_TN),
        in_specs=[
            pl.BlockSpec((rows, d), lambda l, j: (0, 0)),
            pl.BlockSpec((1, d, MOD_TN), lambda l, j: (l, 0, j)),
            pl.BlockSpec((1, 1, MOD_TN), lambda l, j: (l, 0, j)),
        ],
        out_specs=pl.BlockSpec((1, rows, MOD_TN), lambda l, j: (l, 0, j)),
        out_shape=jax.ShapeDtypeStruct((depth, rows, n), F32),
        compiler_params=pltpu.CompilerParams(vmem_limit_bytes=VMEM_LIMIT_BYTES),
        name="adaln_mod",
    )(c_pad, w_ada, b_ada.reshape(depth, 1, n))
    return out


def _qkv_kernel(x_ref, mod_ref, g_ref, w_in_ref, gq_ref, gkv_ref, wuqt_ref, wuqrt_ref,
                wuk_ref, wuvt_ref, pos_ref, invf_ref,
                qt_ref, k_ref, vt_ref, *, scale, layer):
    shift, scl = _mod_rows(mod_ref, (0, 1))
    toks = [slice(r0, r0 + QKV_SUB) for r0 in range(0, x_ref.shape[1], QKV_SUB)]

    hs = [((_rms(x_ref[0, tok, :]) * g_ref[layer:layer + 1, :]) * (1.0 + scl) + shift).astype(BF16)
          for tok in toks]
    zs = [jnp.dot(h, w_in_ref[0], preferred_element_type=F32) for h in hs]

    latents = []
    for z in zs:
        c_q = z[:, 0:Q_LORA]
        c_kv = z[:, Q_LORA:Q_LORA + KV_LORA]
        latents.append(((_rms(c_q) * gq_ref[layer:layer + 1, :]).astype(BF16),
                        (_rms(c_kv) * gkv_ref[layer:layer + 1, :]).astype(BF16)))

    prods = [(_nt_dot(wuqt_ref[0], cqn), _nt_dot(wuqrt_ref[0], cqn),
              jnp.dot(ckvn, wuk_ref[0], preferred_element_type=F32), _nt_dot(wuvt_ref[0], ckvn))
             for cqn, ckvn in latents]

    z_lo = jnp.zeros((ROPE_LO, QKV_SUB), F32)
    z_hi = jnp.zeros((HEAD_PAD - ROPE_HI, QKV_SUB), F32)
    for tok, z, (qt, qt_rot, k, vt) in zip(toks, zs, prods):
        ang = invf_ref[...] * pos_ref[0, :, tok].astype(F32)
        cost = jnp.cos(ang)
        sint = jnp.sin(ang)
        for hd in range(N_HEADS):
            base = hd * HEAD_PAD
            qt_ref[0, base:base + ROPE_LO, tok] = (qt[base:base + ROPE_LO] * scale).astype(BF16)
            roped = (qt[base + ROPE_LO:base + ROPE_HI] * cost
                     + qt_rot[hd * QK_ROPE:(hd + 1) * QK_ROPE] * sint)
            qt_ref[0, base + ROPE_LO:base + ROPE_HI, tok] = (roped * scale).astype(BF16)
            qt_ref[0, base + ROPE_HI:base + HEAD_PAD, tok] = (
                qt[base + ROPE_HI:base + HEAD_PAD] * scale).astype(BF16)

        kr = z[:, Q_LORA + KV_LORA:Q_LORA + KV_LORA + HEAD_PAD]
        kr_rot = z[:, Q_LORA + KV_LORA + HEAD_PAD:]
        cos_tok = jnp.concatenate([z_lo, cost, z_hi], axis=0).T
        sin_tok = jnp.concatenate([z_lo, sint, z_hi], axis=0).T
        kr_full = kr * cos_tok + kr_rot * sin_tok
        for hd in range(N_HEADS):
            base = hd * HEAD_PAD
            k_ref[0, tok, base:base + HEAD_PAD] = (k[:, base:base + HEAD_PAD] + kr_full).astype(BF16)

        vt_ref[0, :, tok] = vt.astype(BF16)


def _qkv(x, mod, positions, wts, layer):
    b, s, d = x.shape
    tm = TM_QKV
    inv_freq = ROPE_THETA ** (-jnp.arange(0, QK_ROPE, 2, dtype=F32) / QK_ROPE)
    invf_col = jnp.concatenate([inv_freq, inv_freq]).reshape(QK_ROPE, 1)
    scale = math.log2(math.e) / math.sqrt(QK_NOPE + QK_ROPE)
    stacked = [wts["ln1_g"], wts["w_in_a"], wts["q_norm_g"], wts["kv_norm_g"], wts["wuqt"], wts["wuqrt"],
               wts["wuk"], wts["wuvt"]]
    return pl.pallas_call(
        functools.partial(_qkv_kernel, scale=scale, layer=layer),
        grid=(b, s // tm),
        in_specs=[
            pl.BlockSpec((1, tm, d), lambda i, j: (i, j, 0)),
            _layer_spec(mod, layer),
            _whole_spec(wts["ln1_g"]),
            _layer_spec(wts["w_in_a"], layer),
            _whole_spec(wts["q_norm_g"]),
            _whole_spec(wts["kv_norm_g"]),
            _layer_spec(wts["wuqt"], layer),
            _layer_spec(wts["wuqrt"], layer),
            _layer_spec(wts["wuk"], layer),
            _layer_spec(wts["wuvt"], layer),
            pl.BlockSpec((1, 1, tm), lambda i, j: (i, 0, j)),
            _whole_spec(invf_col),
        ],
        out_specs=[
            pl.BlockSpec((1, N_HEADS * HEAD_PAD, tm), lambda i, j: (i, 0, j)),
            pl.BlockSpec((1, tm, N_HEADS * HEAD_PAD), lambda i, j: (i, j, 0)),
            pl.BlockSpec((1, ATTN_DIM, tm), lambda i, j: (i, 0, j)),
        ],
        out_shape=[
            jax.ShapeDtypeStruct((b, N_HEADS * HEAD_PAD, s), BF16),
            jax.ShapeDtypeStruct((b, s, N_HEADS * HEAD_PAD), BF16),
            jax.ShapeDtypeStruct((b, ATTN_DIM, s), BF16),
        ],
        compiler_params=pltpu.CompilerParams(
            dimension_semantics=("arbitrary", "arbitrary"),
            vmem_limit_bytes=VMEM_LIMIT_BYTES),
        name="qkv_proj",
    )(x, mod, *stacked, positions.reshape(b, 1, s), invf_col)


def _attn_kernel(qt_ref, k_ref, vt_ref, *refs, n_q, n_cast):
    cast_in, o_ref, cast_out = refs[:n_cast], refs[n_cast], refs[n_cast + 1:2 * n_cast + 1]
    s_ref, acc_ref, m_ref = refs[2 * n_cast + 1:]

    n_below = n_q * (n_q - 1) // 2
    hq = TQ // 2
    ones_rows = jnp.ones((ACC_ROWS - V_DIM, TK), BF16)
    half_mask = (lax.broadcasted_iota(jnp.int32, (hq, hq), 1)
                 >= lax.broadcasted_iota(jnp.int32, (hq, hq), 0))

    def keys(hd, off, n):
        return k_ref[0, pl.ds(off, n), hd * HEAD_PAD:(hd + 1) * HEAD_PAD]

    def queries(hd, off, n):
        return qt_ref[0, hd * HEAD_PAD:(hd + 1) * HEAD_PAD, pl.ds(off, n)]

    def values(hd, off, n):
        return jnp.concatenate(
            [vt_ref[0, hd * V_DIM:(hd + 1) * V_DIM, pl.ds(off, n)], ones_rows[:, :n]], axis=0)

    def col_max(s):
        return jnp.max(s, axis=0, keepdims=True)

    def produce_below(slot, hd, qi, j):
        s = jnp.dot(keys(hd, pl.multiple_of(j * TK, TK), TK), queries(hd, pl.multiple_of(qi * TQ, TQ), TQ),
                    preferred_element_type=F32)
        s_ref[slot, hd] = s
        return col_max(s)

    def consume_below(slot, hd, qi, j, tile_max):
        m = m_ref[qi, hd]
        m_new = jnp.maximum(m, tile_max)
        p = jnp.exp2(s_ref[slot, hd] - m_new).astype(BF16)
        acc_ref[qi, hd] = (jnp.exp2(m - m_new) * acc_ref[qi, hd]
                           + jnp.dot(values(hd, pl.multiple_of(j * TK, TK), TK), p, preferred_element_type=F32))
        m_ref[qi, hd] = m_new

    def produce_diag(slot, hd, d):
        off = d * TQ
        s_ref[slot, hd, 0:hq, :] = jnp.dot(keys(hd, off, hq), queries(hd, off, TQ),
                                           preferred_element_type=F32)
        s_ref[slot, hd, hq:TK, hq:TQ] = jnp.dot(keys(hd, off + hq, hq), queries(hd, off + hq, hq),
                                                preferred_element_type=F32)

    def consume_diag(slot, hd, d):
        off = d * TQ
        neg = jnp.finfo(F32).min
        s_tl = jnp.where(half_mask, s_ref[slot, hd, 0:hq, 0:hq], neg)
        s_tr = s_ref[slot, hd, 0:hq, hq:TQ]
        s_br = jnp.where(half_mask, s_ref[slot, hd, hq:TK, hq:TQ], neg)
        m = m_ref[d, hd]
        m_l = jnp.maximum(m[:, 0:hq], col_max(s_tl))
        m_r = jnp.maximum(m[:, hq:TQ], jnp.maximum(col_max(s_tr), col_max(s_br)))
        p_l = jnp.exp2(s_tl - m_l).astype(BF16)
        p_r = jnp.concatenate([jnp.exp2(s_tr - m_r), jnp.exp2(s_br - m_r)], axis=0).astype(BF16)
        acc = acc_ref[d, hd]
        acc_l = (jnp.exp2(m[:, 0:hq] - m_l) * acc[:, 0:hq]
                 + jnp.dot(values(hd, off, hq), p_l, preferred_element_type=F32))
        acc_r = (jnp.exp2(m[:, hq:TQ] - m_r) * acc[:, hq:TQ]
                 + jnp.dot(values(hd, off, TK), p_r, preferred_element_type=F32))
        rows = slice(hd * V_DIM, (hd + 1) * V_DIM)
        o_ref[0, rows, pl.ds(off, hq)] = (
            acc_l[0:V_DIM] * (1.0 / acc_l[V_DIM:V_DIM + 1])).astype(o_ref.dtype)
        o_ref[0, rows, pl.ds(off + hq, hq)] = (
            acc_r[0:V_DIM] * (1.0 / acc_r[V_DIM:V_DIM + 1])).astype(o_ref.dtype)

    heads = range(ATTN_HEADS)

    def next_below(qi, j):
        row_end = j + 1 == qi
        return jnp.where(row_end, qi + 1, qi), jnp.where(row_end, 0, j + 1)

    def below_step(slot, tile, maxes, to_diag=False):
        nxt = next_below(*tile)
        new_maxes = []
        for hd in heads:
            if to_diag:
                produce_diag(1 - slot, hd, 0)
            else:
                new_maxes.append(produce_below(1 - slot, hd, *nxt))
            consume_below(slot, hd, *tile, maxes[hd])
        return nxt, tuple(new_maxes)

    def diag_step(slot, d, produce_next=True):
        for hd in heads:
            if produce_next:
                produce_diag(1 - slot, hd, d + 1)
            consume_diag(slot, hd, d)
        return d + 1

    tile0 = (jnp.int32(1), jnp.int32(0))
    maxes0 = tuple(produce_below(0, hd, *tile0) for hd in heads)

    acc_ref[...] = jnp.zeros(acc_ref.shape, F32)
    m_ref[...] = jnp.full(m_ref.shape, jnp.finfo(F32).min, F32)
    for src, dst in zip(cast_in, cast_out):
        dst[...] = src[...].astype(BF16)

    def below_pair(carry):
        tile, maxes = carry
        tile, maxes = below_step(0, tile, maxes)
        return below_step(1, tile, maxes)

    def below_trip(_, carry):
        for _ in range(BELOW_PAIRS_PER_TRIP):
            carry = below_pair(carry)
        return carry

    n_pairs = n_below // 2 - 1
    carry = lax.fori_loop(0, n_pairs // BELOW_PAIRS_PER_TRIP, below_trip, (tile0, maxes0))
    for _ in range(n_pairs % BELOW_PAIRS_PER_TRIP):
        carry = below_pair(carry)
    tile, maxes = below_step(0, *carry)
    below_step(1, tile, maxes, to_diag=True)

    for d in range(n_q):
        diag_step(d % 2, d, produce_next=d + 1 < n_q)


def _attention(qt, k, vt, f32_weights, layer):
    b, _, s = qt.shape
    hb = ATTN_HEADS
    groups = N_HEADS // hb
    n_steps = b * groups
    n_q = s // TQ
    assert n_q % 2 == 0 and (n_q * (n_q - 1) // 2) % 2 == 0, "both tile streams run two tiles per loop trip"

    def rows_per_step(w):
        rows = w.shape[1] // n_steps
        assert rows * n_steps == w.shape[1] and rows % BF16_SUBLANES == 0, w.shape
        return rows

    cast_in = [pl.BlockSpec((1, rows_per_step(w), w.shape[2]), lambda i, h: (layer, i * groups + h, 0))
               for w in f32_weights]
    cast_out = [pl.BlockSpec((1, rows_per_step(w), w.shape[2]), lambda i, h: (0, i * groups + h, 0))
                for w in f32_weights]
    out = pl.pallas_call(
        functools.partial(_attn_kernel, n_q=n_q, n_cast=len(f32_weights)),
        grid=(b, groups),
        in_specs=[
            pl.BlockSpec((1, hb * HEAD_PAD, s), lambda i, h: (i, h, 0)),
            pl.BlockSpec((1, s, hb * HEAD_PAD), lambda i, h: (i, 0, h)),
            pl.BlockSpec((1, hb * V_DIM, s), lambda i, h: (i, h, 0)),
        ] + cast_in,
        out_specs=[pl.BlockSpec((1, hb * V_DIM, s), lambda i, h: (i, h, 0))] + cast_out,
        out_shape=[jax.ShapeDtypeStruct((b, ATTN_DIM, s), BF16)]
        + [jax.ShapeDtypeStruct((1,) + w.shape[1:], BF16) for w in f32_weights],
        scratch_shapes=[
            pltpu.VMEM((2, hb, TK, TQ), F32),
            pltpu.VMEM((n_q, hb, ACC_ROWS, TQ), F32),
            pltpu.VMEM((n_q, hb, 1, TQ), F32),
        ],
        compiler_params=pltpu.CompilerParams(
            dimension_semantics=("arbitrary", "arbitrary"),
            vmem_limit_bytes=VMEM_LIMIT_BYTES),
        name="mla_attention",
    )(qt, k, vt, *f32_weights)
    return out[0], out[1:]


def _mix_kernel(x_ref, ot_ref, mod_ref, g1_ref, g2_ref, gf_ref, w_in_ref, w_pool_ref, pscale_ref,
                p_pool_ref, p_attn_ref, w_out_ref, w_ff1_ref, w_ff2_ref,
                o_ref, uext_ref, *, final, layer):
    tm = x_ref.shape[1]
    si = pl.program_id(1)

    @pl.when(si == 0)
    def _():
        uext_ref[0:POOL_HALO, :] = jnp.zeros((POOL_HALO, POOL_DIM), F32)

    @pl.when(si > 0)
    def _():
        uext_ref[0:POOL_HALO, :] = uext_ref[tm:tm + POOL_HALO, :]

    shift1, scale1, gate1, shift2, scale2, gate2 = _mod_rows(mod_ref, range(N_MOD))

    for r0 in range(0, tm, MIX_SUB):
        tok = slice(r0, r0 + MIX_SUB)
        x = x_ref[0, tok, :]

        y_b = _tn_dot(ot_ref[0, :, tok], p_attn_ref[0])
        h = ((_rms(x) * g1_ref[layer:layer + 1, :]) * (1.0 + scale1) + shift1).astype(BF16)
        u = jnp.dot(h, w_in_ref[0, :, 0:POOL_DIM], preferred_element_type=F32)
        gz = jnp.dot(h, w_in_ref[0, :, POOL_DIM:], preferred_element_type=F32)
        gz_a = gz[:, 0:D_MODEL]
        gz_b = gz[:, D_MODEL:]

        uext_ref[POOL_HALO + r0:POOL_HALO + r0 + MIX_SUB, :] = u
        head_pos = si * tm + r0 + lax.broadcasted_iota(jnp.int32, (POOL_HALO, 1), 0)
        pooled = []
        for g, w in enumerate(POOL_WINDOWS):
            eg = uext_ref[r0:r0 + POOL_HALO + MIX_SUB,
                          g * POOL_GROUP_DIM:(g + 1) * POOL_GROUP_DIM]
            win = eg
            k = 1
            while k < w:
                win = win + pltpu.roll(win, k, axis=0)
                k *= 2
            win = win[POOL_HALO:]
            ug = eg[POOL_HALO:]
            inv_head = 1.0 / jnp.minimum(head_pos + 1, w).astype(F32)
            mean = jnp.concatenate([win[:POOL_HALO] * inv_head, win[POOL_HALO:] * (1.0 / w)], axis=0)
            yg = jnp.dot((mean - ug).astype(BF16), w_pool_ref[0, g], preferred_element_type=F32)
            pooled.append(yg)
        y_pool = jnp.concatenate(pooled, axis=-1) * pscale_ref[layer:layer + 1, :]
        y_a = jnp.dot(y_pool.astype(BF16), p_pool_ref[0], preferred_element_type=F32)

        merged = _sigmoid(gz_a) * y_a + _sigmoid(gz_b) * y_b
        x1 = x + gate1 * jnp.dot(merged.astype(BF16), w_out_ref[0], preferred_element_type=F32)

        h2 = ((_rms(x1) * g2_ref[layer:layer + 1, :]) * (1.0 + scale2) + shift2).astype(BF16)
        hidden = [jnp.square(jnp.maximum(
            jnp.dot(h2, w_ff1_ref[0, :, c0:c0 + FF_CHUNK], preferred_element_type=F32), 0.0)).astype(BF16)
            for c0 in range(0, D_FF, FF_CHUNK)]
        ff = jnp.dot(jnp.concatenate(hidden, axis=-1), w_ff2_ref[0], preferred_element_type=F32)
        x2 = x1 + gate2 * ff
        if final:
            x2 = _rms(x2) * gf_ref[...]
        o_ref[0, tok, :] = x2


def _mix(x, ot, mod, wts, layer_wts, layer, *, final):
    b, s, d = x.shape
    tm = TM_MIX
    once = dict(pipeline_mode=pl.Buffered(1))
    names = ["ln1_g", "ln2_g", "final_g", "w_in_b", "w_pool", "pool_scale", "p_pool", "p_attn", "w_out",
             "w_ff1", "w_ff2"]
    whole = {"ln1_g", "ln2_g", "final_g", "pool_scale"}
    wts = {**wts, **layer_wts}
    specs = [_whole_spec(wts[n], **once) if n in whole
             else _layer_spec(wts[n], 0 if n in layer_wts else layer, **once)
             for n in names]
    return pl.pallas_call(
        functools.partial(_mix_kernel, final=final, layer=layer),
        grid=(b, s // tm),
        in_specs=[
            pl.BlockSpec((1, tm, d), lambda i, j: (i, j, 0)),
            pl.BlockSpec((1, ATTN_DIM, tm), lambda i, j: (i, 0, j)),
            _layer_spec(mod, layer, **once),
        ] + specs,
        out_specs=pl.BlockSpec((1, tm, d), lambda i, j: (i, j, 0)),
        out_shape=jax.ShapeDtypeStruct((b, s, d), F32),
        scratch_shapes=[pltpu.VMEM((tm + POOL_HALO, POOL_DIM), F32)],
        compiler_params=pltpu.CompilerParams(
            dimension_semantics=("arbitrary", "arbitrary"),
            vmem_limit_bytes=VMEM_LIMIT_BYTES),
        name="mix_mlp",
    )(x, ot, mod, *[wts[n] for n in names])


W_IN_PREP_ROWS = 256


def _w_in_prep_kernel(w_ref, a_ref, b_ref):
    wt = w_ref[0]
    rows = wt.shape[1]
    c0 = POOL_DIM
    c2 = c0 + Q_LORA + KV_LORA
    c3 = c2 + QK_ROPE
    half = QK_ROPE // 2
    kr = wt[c2:c3]
    zl = jnp.zeros((ROPE_LO, rows), F32)
    zr = jnp.zeros((HEAD_PAD - ROPE_HI, rows), F32)
    a_t = jnp.concatenate([wt[c0:c2], zl, kr, zr, zl, -kr[half:], kr[:half], zr], axis=0)
    a_ref[0] = a_t.T.astype(BF16)
    b_ref[0] = jnp.concatenate([wt[:c0], wt[c3:]], axis=0).T.astype(BF16)


def _w_in_prep(w_in):
    depth, d, n = w_in.shape
    na = Q_LORA + KV_LORA + 2 * HEAD_PAD
    nb = n - (Q_LORA + KV_LORA + QK_ROPE)
    rows = W_IN_PREP_ROWS
    w_in = jnp.swapaxes(w_in, 1, 2)
    return pl.pallas_call(
        _w_in_prep_kernel,
        grid=(depth, d // rows),
        in_specs=[pl.BlockSpec((1, n, rows), lambda l, r: (l, 0, r))],
        out_specs=[pl.BlockSpec((1, rows, na), lambda l, r: (l, r, 0)),
                   pl.BlockSpec((1, rows, nb), lambda l, r: (l, r, 0))],
        out_shape=[jax.ShapeDtypeStruct((depth, d, na), BF16), jax.ShapeDtypeStruct((depth, d, nb), BF16)],
        compiler_params=pltpu.CompilerParams(vmem_limit_bytes=VMEM_LIMIT_BYTES),
        name="w_in_prep",
    )(w_in)


def _prep_weights(ln1_g, ln2_g, w_in, q_norm_g, w_uq, kv_norm_g, w_uk, w_uv, w_pool, pool_scale, final_g):
    depth = w_in.shape[0]
    half = QK_ROPE // 2
    w_in_a, w_in_b = _w_in_prep(w_in)

    pad_q = HEAD_PAD - (QK_NOPE + QK_ROPE)
    wuq_pad = jnp.pad(w_uq, ((0, 0), (0, 0), (0, 0), (0, pad_q)))
    wuqt = wuq_pad.reshape(depth, Q_LORA, N_HEADS * HEAD_PAD).transpose(0, 2, 1).astype(BF16)
    r = w_uq[..., QK_NOPE:]
    rot = jnp.concatenate([-r[..., half:], r[..., :half]], axis=-1)
    wuqrt = rot.reshape(depth, Q_LORA, N_HEADS * QK_ROPE).transpose(0, 2, 1).astype(BF16)
    wuk = jnp.pad(w_uk, ((0, 0), (0, 0), (0, 0), (0, HEAD_PAD - QK_NOPE)))
    wuk = wuk.reshape(depth, KV_LORA, N_HEADS * HEAD_PAD).astype(BF16)
    wuvt = w_uv.reshape(depth, KV_LORA, ATTN_DIM).transpose(0, 2, 1).astype(BF16)
    return dict(
        ln1_g=ln1_g, ln2_g=ln2_g, final_g=final_g.reshape(1, D_MODEL), q_norm_g=q_norm_g,
        kv_norm_g=kv_norm_g, pool_scale=pool_scale, w_in_a=w_in_a, w_in_b=w_in_b, wuqt=wuqt, wuqrt=wuqrt,
        wuk=wuk, wuvt=wuvt, w_pool=w_pool.astype(BF16))


def kernel(x, c, positions, ln1_g, ln2_g, w_ada, b_ada, w_in, q_norm_g, w_uq, kv_norm_g, w_uk,
           w_uv, w_pool, pool_scale, p_pool, p_attn, w_out, w_ff1, w_ff2, final_g):
    depth = w_in.shape[0]
    mod = _modulation(c, w_ada, b_ada)
    wts = _prep_weights(ln1_g, ln2_g, w_in, q_norm_g, w_uq, kv_norm_g, w_uk, w_uv, w_pool, pool_scale,
                        final_g)
    cast_names = ("p_pool", "p_attn", "w_out", "w_ff1", "w_ff2")
    f32_weights = (p_pool, p_attn, w_out, w_ff1, w_ff2)
    for layer in range(depth):
        qt, k, vt = _qkv(x, mod, positions, wts, layer)
        ot, cast = _attention(qt, k, vt, f32_weights, layer)
        x = _mix(x, ot, mod, wts, dict(zip(cast_names, cast)), layer, final=(layer == depth - 1))
    return x
```

```python
import functools
import math

import jax
import jax.numpy as jnp
from jax import lax
from jax.experimental import pallas as pl
from jax.experimental.pallas import tpu as pltpu

D_MODEL = 1024
N_HEADS = 8
QK_NOPE = 64
QK_ROPE = 32
V_DIM = 64
Q_LORA = 384
KV_LORA = 256
POOL_WINDOWS = (2, 4, 8, 16)
POOL_GROUP_DIM = 128
POOL_DIM = len(POOL_WINDOWS) * POOL_GROUP_DIM
ATTN_DIM = N_HEADS * V_DIM
D_FF = 4 * D_MODEL
N_MOD = 6
EPS = 1e-6
ROPE_THETA = 10000.0

HEAD_PAD = 128
ROPE_LO = QK_NOPE
ROPE_HI = QK_NOPE + QK_ROPE
POOL_HALO = 16

VMEM_LIMIT_BYTES = 56 * 1024 * 1024

F32 = jnp.float32
BF16 = jnp.bfloat16

TM_QKV = 1024
QKV_SUB = 512
TM_MIX = 1024
MIX_SUB = 512
TQ = 512
TK = 512
ATTN_HEADS = 4
BELOW_PAIRS_PER_TRIP = 4
F32_SUBLANES = 8
BF16_SUBLANES = 16
ACC_ROWS = V_DIM + BF16_SUBLANES
FF_CHUNK = 1024
MOD_TN = 1536


def _nt_dot(a, b):
    return lax.dot_general(a, b, (((1,), (1,)), ((), ())), preferred_element_type=F32)


def _tn_dot(a, b):
    return lax.dot_general(a, b, (((0,), (0,)), ((), ())), preferred_element_type=F32)


def _rms(x):
    return x * lax.rsqrt(jnp.mean(x * x, axis=-1, keepdims=True) + EPS)


def _sigmoid(x):
    return 1.0 / (1.0 + jnp.exp(-x))


def _mod_rows(mod_ref, chunks):
    row = pl.ds(pl.program_id(0), 1)
    return tuple(mod_ref[0, row, c * D_MODEL:(c + 1) * D_MODEL] for c in chunks)


def _layer_spec(arr, layer, **kw):
    tail = (0,) * (arr.ndim - 1)
    return pl.BlockSpec((1,) + arr.shape[1:], lambda i, j: (layer,) + tail, **kw)


def _whole_spec(arr, **kw):
    zeros = (0,) * arr.ndim
    return pl.BlockSpec(arr.shape, lambda i, j: zeros, **kw)


def _mod_kernel(c_ref, w_ref, b_ref, o_ref):
    c = c_ref[...]
    c_act = c * _sigmoid(c)
    o_ref[0] = jnp.dot(c_act.astype(BF16), w_ref[0].astype(BF16),
                       preferred_element_type=F32) + b_ref[0]


def _modulation(c, w_ada, b_ada):
    depth, d, n = w_ada.shape
    b = c.shape[0]
    rows = F32_SUBLANES
    assert b <= rows
    c_pad = jnp.pad(c, ((0, rows - b), (0, 0)))
    out = pl.pallas_call(
        _mod_kernel,
        grid=(depth, n // MOD_TN),
        in_specs=[
            pl.BlockSpec((rows, d), lambda l, j: (0, 0)),
            pl.BlockSpec((1, d, MOD_TN), lambda l, j: (l, 0, j)),
            pl.BlockSpec((1, 1, MOD_TN), lambda l, j: (l, 0, j)),
        ],
        out_specs=pl.BlockSpec((1, rows, MOD_TN), lambda l, j: (l, 0, j)),
        out_shape=jax.ShapeDtypeStruct((depth, rows, n), F32),
        compiler_params=pltpu.CompilerParams(vmem_limit_bytes=VMEM_LIMIT_BYTES),
        name="adaln_mod",
    )(c_pad, w_ada, b_ada.reshape(depth, 1, n))
    return out


def _qkv_kernel(x_ref, mod_ref, g_ref, w_in_ref, gq_ref, gkv_ref, wuqt_ref, wuqrt_ref,
                wuk_ref, wuvt_ref, pos_ref, invf_ref,
                qt_ref, k_ref, vt_ref, *, scale, layer):
    shift, scl = _mod_rows(mod_ref, (0, 1))
    toks = [slice(r0, r0 + QKV_SUB) for r0 in range(0, x_ref.shape[1], QKV_SUB)]

    hs = [((_rms(x_ref[0, tok, :]) * g_ref[layer:layer + 1, :]) * (1.0 + scl) + shift).astype(BF16)
          for tok in toks]
    zs = [jnp.dot(h, w_in_ref[0], preferred_element_type=F32) for h in hs]

    latents = []
    for z in zs:
        c_q = z[:, 0:Q_LORA]
        c_kv = z[:, Q_LORA:Q_LORA + KV_LORA]
        latents.append(((_rms(c_q) * gq_ref[layer:layer + 1, :]).astype(BF16),
                        (_rms(c_kv) * gkv_ref[layer:layer + 1, :]).astype(BF16)))

    prods = [(_nt_dot(wuqt_ref[0], cqn), _nt_dot(wuqrt_ref[0], cqn),
              jnp.dot(ckvn, wuk_ref[0], preferred_element_type=F32), _nt_dot(wuvt_ref[0], ckvn))
             for cqn, ckvn in latents]

    z_lo = jnp.zeros((ROPE_LO, QKV_SUB), F32)
    z_hi = jnp.zeros((HEAD_PAD - ROPE_HI, QKV_SUB), F32)
    for tok, z, (qt, qt_rot, k, vt) in zip(toks, zs, prods):
        ang = invf_ref[...] * pos_ref[0, :, tok].astype(F32)
        cost = jnp.cos(ang)
        sint = jnp.sin(ang)
        for hd in range(N_HEADS):
            base = hd * HEAD_PAD
            qt_ref[0, base:base + ROPE_LO, tok] = (qt[base:base + ROPE_LO] * scale).astype(BF16)
            roped = (qt[base + ROPE_LO:base + ROPE_HI] * cost
                     + qt_rot[hd * QK_ROPE:(hd + 1) * QK_ROPE] * sint)
            qt_ref[0, base + ROPE_LO:base + ROPE_HI, tok] = (roped * scale).astype(BF16)
            qt_ref[0, base + ROPE_HI:base + HEAD_PAD, tok] = (
                qt[base + ROPE_HI:base + HEAD_PAD] * scale).astype(BF16)

        kr = z[:, Q_LORA + KV_LORA:Q_LORA + KV_LORA + HEAD_PAD]
        kr_rot = z[:, Q_LORA + KV_LORA + HEAD_PAD:]
        cos_tok = jnp.concatenate([z_lo, cost, z_hi], axis=0).T
        sin_tok = jnp.concatenate([z_lo, sint, z_hi], axis=0).T
        kr_full = kr * cos_tok + kr_rot * sin_tok
        for hd in range(N_HEADS):
            base = hd * HEAD_PAD
            k_ref[0, tok, base:base + HEAD_PAD] = (k[:, base:base + HEAD_PAD] + kr_full).astype(BF16)

        vt_ref[0, :, tok] = vt.astype(BF16)


def _qkv(x, mod, positions, wts, layer):
    b, s, d = x.shape
    tm = TM_QKV
    inv_freq = ROPE_THETA ** (-jnp.arange(0, QK_ROPE, 2, dtype=F32) / QK_ROPE)
    invf_col = jnp.concatenate([inv_freq, inv_freq]).reshape(QK_ROPE, 1)
    scale = math.log2(math.e) / math.sqrt(QK_NOPE + QK_ROPE)
    stacked = [wts["ln1_g"], wts["w_in_a"], wts["q_norm_g"], wts["kv_norm_g"], wts["wuqt"], wts["wuqrt"],
               wts["wuk"], wts["wuvt"]]
    return pl.pallas_call(
        functools.partial(_qkv_kernel, scale=scale, layer=layer),
        grid=(b, s // tm),
        in_specs=[
            pl.BlockSpec((1, tm, d), lambda i, j: (i, j, 0)),
            _layer_spec(mod, layer),
            _whole_spec(wts["ln1_g"]),
            _layer_spec(wts["w_in_a"], layer),
            _whole_spec(wts["q_norm_g"]),
            _whole_spec(wts["kv_norm_g"]),
            _layer_spec(wts["wuqt"], layer),
            _layer_spec(wts["wuqrt"], layer),
            _layer_spec(wts["wuk"], layer),
            _layer_spec(wts["wuvt"], layer),
            pl.BlockSpec((1, 1, tm), lambda i, j: (i, 0, j)),
            _whole_spec(invf_col),
        ],
        out_specs=[
            pl.BlockSpec((1, N_HEADS * HEAD_PAD, tm), lambda i, j: (i, 0, j)),
            pl.BlockSpec((1, tm, N_HEADS * HEAD_PAD), lambda i, j: (i, j, 0)),
            pl.BlockSpec((1, ATTN_DIM, tm), lambda i, j: (i, 0, j)),
        ],
        out_shape=[
            jax.ShapeDtypeStruct((b, N_HEADS * HEAD_PAD, s), BF16),
            jax.ShapeDtypeStruct((b, s, N_HEADS * HEAD_PAD), BF16),
            jax.ShapeDtypeStruct((b, ATTN_DIM, s), BF16),
        ],
        compiler_params=pltpu.CompilerParams(
            dimension_semantics=("arbitrary", "arbitrary"),
            vmem_limit_bytes=VMEM_LIMIT_BYTES),
        name="qkv_proj",
    )(x, mod, *stacked, positions.reshape(b, 1, s), invf_col)


def _attn_kernel(qt_ref, k_ref, vt_ref, *refs, n_q, n_cast):
    cast_in, o_ref, cast_out = refs[:n_cast], refs[n_cast], refs[n_cast + 1:2 * n_cast + 1]
    s_ref, acc_ref, m_ref = refs[2 * n_cast + 1:]

    n_below = n_q * (n_q - 1) // 2
    hq = TQ // 2
    ones_rows = jnp.ones((ACC_ROWS - V_DIM, TK), BF16)
    half_mask = (lax.broadcasted_iota(jnp.int32, (hq, hq), 1)
                 >= lax.broadcasted_iota(jnp.int32, (hq, hq), 0))

    def keys(hd, off, n):
        return k_ref[0, pl.ds(off, n), hd * HEAD_PAD:(hd + 1) * HEAD_PAD]

    def queries(hd, off, n):
        return qt_ref[0, hd * HEAD_PAD:(hd + 1) * HEAD_PAD, pl.ds(off, n)]

    def values(hd, off, n):
        return jnp.concatenate(
            [vt_ref[0, hd * V_DIM:(hd + 1) * V_DIM, pl.ds(off, n)], ones_rows[:, :n]], axis=0)

    def col_max(s):
        return jnp.max(s, axis=0, keepdims=True)

    def produce_below(slot, hd, qi, j):
        s = jnp.dot(keys(hd, pl.multiple_of(j * TK, TK), TK), queries(hd, pl.multiple_of(qi * TQ, TQ), TQ),
                    preferred_element_type=F32)
        s_ref[slot, hd] = s
        return col_max(s)

    def consume_below(slot, hd, qi, j, tile_max):
        m = m_ref[qi, hd]
        m_new = jnp.maximum(m, tile_max)
        p = jnp.exp2(s_ref[slot, hd] - m_new).astype(BF16)
        acc_ref[qi, hd] = (jnp.exp2(m - m_new) * acc_ref[qi, hd]
                           + jnp.dot(values(hd, pl.multiple_of(j * TK, TK), TK), p, preferred_element_type=F32))
        m_ref[qi, hd] = m_new

    def produce_diag(slot, hd, d):
        off = d * TQ
        s_ref[slot, hd, 0:hq, :] = jnp.dot(keys(hd, off, hq), queries(hd, off, TQ),
                                           preferred_element_type=F32)
        s_ref[slot, hd, hq:TK, hq:TQ] = jnp.dot(keys(hd, off + hq, hq), queries(hd, off + hq, hq),
                                                preferred_element_type=F32)

    def consume_diag(slot, hd, d):
        off = d * TQ
        neg = jnp.finfo(F32).min
        s_tl = jnp.where(half_mask, s_ref[slot, hd, 0:hq, 0:hq], neg)
        s_tr = s_ref[slot, hd, 0:hq, hq:TQ]
        s_br = jnp.where(half_mask, s_ref[slot, hd, hq:TK, hq:TQ], neg)
        m = m_ref[d, hd]
        m_l = jnp.maximum(m[:, 0:hq], col_max(s_tl))
        m_r = jnp.maximum(m[:, hq:TQ], jnp.maximum(col_max(s_tr), col_max(s_br)))
        p_l = jnp.exp2(s_tl - m_l).astype(BF16)
        p_r = jnp.concatenate([jnp.exp2(s_tr - m_r), jnp.exp2(s_br - m_r)], axis=0).astype(BF16)
        acc = acc_ref[d, hd]
        acc_l = (jnp.exp2(m[:, 0:hq] - m_l) * acc[:, 0:hq]
                 + jnp.dot(values(hd, off, hq), p_l, preferred_element_type=F32))
        acc_r = (jnp.exp2(m[:, hq:TQ] - m_r) * acc[:, hq:TQ]
                 + jnp.dot(values(hd, off, TK), p_r, preferred_element_type=F32))
        rows = slice(hd * V_DIM, (hd + 1) * V_DIM)
        o_ref[0, rows, pl.ds(off, hq)] = (
            acc_l[0:V_DIM] * (1.0 / acc_l[V_DIM:V_DIM + 1])).astype(o_ref.dtype)
        o_ref[0, rows, pl.ds(off + hq, hq)] = (
            acc_r[0:V_DIM] * (1.0 / acc_r[V_DIM:V_DIM + 1])).astype(o_ref.dtype)

    heads = range(ATTN_HEADS)

    def next_below(qi, j):
        row_end = j + 1 == qi
        return jnp.where(row_end, qi + 1, qi), jnp.where(row_end, 0, j + 1)

    def below_step(slot, tile, maxes, to_diag=False):
        nxt = next_below(*tile)
        new_maxes = []
        for hd in heads:
            if to_diag:
                produce_diag(1 - slot, hd, 0)
            else:
                new_maxes.append(produce_below(1 - slot, hd, *nxt))
            consume_below(slot, hd, *tile, maxes[hd])
        return nxt, tuple(new_maxes)

    def diag_step(slot, d, produce_next=True):
        for hd in heads:
            if produce_next:
                produce_diag(1 - slot, hd, d + 1)
            consume_diag(slot, hd, d)
        return d + 1

    tile0 = (jnp.int32(1), jnp.int32(0))
    maxes0 = tuple(produce_below(0, hd, *tile0) for hd in heads)

    acc_ref[...] = jnp.zeros(acc_ref.shape, F32)
    m_ref[...] = jnp.full(m_ref.shape, jnp.finfo(F32).min, F32)
    for src, dst in zip(cast_in, cast_out):
        dst[...] = src[...].astype(BF16)

    def below_pair(carry):
        tile, maxes = carry
        tile, maxes = below_step(0, tile, maxes)
        return below_step(1, tile, maxes)

    def below_trip(_, carry):
        for _ in range(BELOW_PAIRS_PER_TRIP):
            carry = below_pair(carry)
        return carry

    n_pairs = n_below // 2 - 1
    carry = lax.fori_loop(0, n_pairs // BELOW_PAIRS_PER_TRIP, below_trip, (tile0, maxes0))
    for _ in range(n_pairs % BELOW_PAIRS_PER_TRIP):
        carry = below_pair(carry)
    tile, maxes = below_step(0, *carry)
    below_step(1, tile, maxes, to_diag=True)

    for d in range(n_q):
        diag_step(d % 2, d, produce_next=d + 1 < n_q)


def _attention(qt, k, vt, f32_weights, layer):
    b, _, s = qt.shape
    hb = ATTN_HEADS
    groups = N_HEADS // hb
    n_steps = b * groups
    n_q = s // TQ
    assert n_q % 2 == 0 and (n_q * (n_q - 1) // 2) % 2 == 0, "both tile streams run two tiles per loop trip"

    def rows_per_step(w):
        rows = w.shape[1] // n_steps
        assert rows * n_steps == w.shape[1] and rows % BF16_SUBLANES == 0, w.shape
        return rows

    cast_in = [pl.BlockSpec((1, rows_per_step(w), w.shape[2]), lambda i, h: (layer, i * groups + h, 0))
               for w in f32_weights]
    cast_out = [pl.BlockSpec((1, rows_per_step(w), w.shape[2]), lambda i, h: (0, i * groups + h, 0))
                for w in f32_weights]
    out = pl.pallas_call(
        functools.partial(_attn_kernel, n_q=n_q, n_cast=len(f32_weights)),
        grid=(b, groups),
        in_specs=[
            pl.BlockSpec((1, hb * HEAD_PAD, s), lambda i, h: (i, h, 0)),
            pl.BlockSpec((1, s, hb * HEAD_PAD), lambda i, h: (i, 0, h)),
            pl.BlockSpec((1, hb * V_DIM, s), lambda i, h: (i, h, 0)),
        ] + cast_in,
        out_specs=[pl.BlockSpec((1, hb * V_DIM, s), lambda i, h: (i, h, 0))] + cast_out,
        out_shape=[jax.ShapeDtypeStruct((b, ATTN_DIM, s), BF16)]
        + [jax.ShapeDtypeStruct((1,) + w.shape[1:], BF16) for w in f32_weights],
        scratch_shapes=[
            pltpu.VMEM((2, hb, TK, TQ), F32),
            pltpu.VMEM((n_q, hb, ACC_ROWS, TQ), F32),
            pltpu.VMEM((n_q, hb, 1, TQ), F32),
        ],
        compiler_params=pltpu.CompilerParams(
            dimension_semantics=("arbitrary", "arbitrary"),
            vmem_limit_bytes=VMEM_LIMIT_BYTES),
        name="mla_attention",
    )(qt, k, vt, *f32_weights)
    return out[0], out[1:]


def _mix_kernel(x_ref, ot_ref, mod_ref, g1_ref, g2_ref, gf_ref, w_in_ref, w_pool_ref, pscale_ref,
                p_pool_ref, p_attn_ref, w_out_ref, w_ff1_ref, w_ff2_ref,
                o_ref, uext_ref, *, final, layer):
    tm = x_ref.shape[1]
    si = pl.program_id(1)

    @pl.when(si == 0)
    def _():
        uext_ref[0:POOL_HALO, :] = jnp.zeros((POOL_HALO, POOL_DIM), F32)

    @pl.when(si > 0)
    def _():
        uext_ref[0:POOL_HALO, :] = uext_ref[tm:tm + POOL_HALO, :]

    shift1, scale1, gate1, shift2, scale2, gate2 = _mod_rows(mod_ref, range(N_MOD))

    for r0 in range(0, tm, MIX_SUB):
        tok = slice(r0, r0 + MIX_SUB)
        x = x_ref[0, tok, :]

        y_b = _tn_dot(ot_ref[0, :, tok], p_attn_ref[0])
        h = ((_rms(x) * g1_ref[layer:layer + 1, :]) * (1.0 + scale1) + shift1).astype(BF16)
        u = jnp.dot(h, w_in_ref[0, :, 0:POOL_DIM], preferred_element_type=F32)
        gz = jnp.dot(h, w_in_ref[0, :, POOL_DIM:], preferred_element_type=F32)
        gz_a = gz[:, 0:D_MODEL]
        gz_b = gz[:, D_MODEL:]

        uext_ref[POOL_HALO + r0:POOL_HALO + r0 + MIX_SUB, :] = u
        head_pos = si * tm + r0 + lax.broadcasted_iota(jnp.int32, (POOL_HALO, 1), 0)
        pooled = []
        for g, w in enumerate(POOL_WINDOWS):
            eg = uext_ref[r0:r0 + POOL_HALO + MIX_SUB,
                          g * POOL_GROUP_DIM:(g + 1) * POOL_GROUP_DIM]
            win = eg
            k = 1
            while k < w:
                win = win + pltpu.roll(win, k, axis=0)
                k *= 2
            win = win[POOL_HALO:]
            ug = eg[POOL_HALO:]
            inv_head = 1.0 / jnp.minimum(head_pos + 1, w).astype(F32)
            mean = jnp.concatenate([win[:POOL_HALO] * inv_head, win[POOL_HALO:] * (1.0 / w)], axis=0)
            yg = jnp.dot((mean - ug).astype(BF16), w_pool_ref[0, g], preferred_element_type=F32)
            pooled.append(yg)
        y_pool = jnp.concatenate(pooled, axis=-1) * pscale_ref[layer:layer + 1, :]
        y_a = jnp.dot(y_pool.astype(BF16), p_pool_ref[0], preferred_element_type=F32)

        merged = _sigmoid(gz_a) * y_a + _sigmoid(gz_b) * y_b
        x1 = x + gate1 * jnp.dot(merged.astype(BF16), w_out_ref[0], preferred_element_type=F32)

        h2 = ((_rms(x1) * g2_ref[layer:layer + 1, :]) * (1.0 + scale2) + shift2).astype(BF16)
        hidden = [jnp.square(jnp.maximum(
            jnp.dot(h2, w_ff1_ref[0, :, c0:c0 + FF_CHUNK], preferred_element_type=F32), 0.0)).astype(BF16)
            for c0 in range(0, D_FF, FF_CHUNK)]
        ff = jnp.dot(jnp.concatenate(hidden, axis=-1), w_ff2_ref[0], preferred_element_type=F32)
        x2 = x1 + gate2 * ff
        if final:
            x2 = _rms(x2) * gf_ref[...]
        o_ref[0, tok, :] = x2


def _mix(x, ot, mod, wts, layer_wts, layer, *, final):
    b, s, d = x.shape
    tm = TM_MIX
    once = dict(pipeline_mode=pl.Buffered(1))
    names = ["ln1_g", "ln2_g", "final_g", "w_in_b", "w_pool", "pool_scale", "p_pool", "p_attn", "w_out",
             "w_ff1", "w_ff2"]
    whole = {"ln1_g", "ln2_g", "final_g", "pool_scale"}
    wts = {**wts, **layer_wts}
    specs = [_whole_spec(wts[n], **once) if n in whole
             else _layer_spec(wts[n], 0 if n in layer_wts else layer, **once)
             for n in names]
    return pl.pallas_call(
        functools.partial(_mix_kernel, final=final, layer=layer),
        grid=(b, s // tm),
        in_specs=[
            pl.BlockSpec((1, tm, d), lambda i, j: (i, j, 0)),
            pl.BlockSpec((1, ATTN_DIM, tm), lambda i, j: (i, 0, j)),
            _layer_spec(mod, layer, **once),
        ] + specs,
        out_specs=pl.BlockSpec((1, tm, d), lambda i, j: (i, j, 0)),
        out_shape=jax.ShapeDtypeStruct((b, s, d), F32),
        scratch_shapes=[pltpu.VMEM((tm + POOL_HALO, POOL_DIM), F32)],
        compiler_params=pltpu.CompilerParams(
            dimension_semantics=("arbitrary", "arbitrary"),
            vmem_limit_bytes=VMEM_LIMIT_BYTES),
        name="mix_mlp",
    )(x, ot, mod, *[wts[n] for n in names])


W_IN_PREP_ROWS = 256


def _w_in_prep_kernel(w_ref, a_ref, b_ref):
    wt = w_ref[0]
    rows = wt.shape[1]
    c0 = POOL_DIM
    c2 = c0 + Q_LORA + KV_LORA
    c3 = c2 + QK_ROPE
    half = QK_ROPE // 2
    kr = wt[c2:c3]
    zl = jnp.zeros((ROPE_LO, rows), F32)
    zr = jnp.zeros((HEAD_PAD - ROPE_HI, rows), F32)
    a_t = jnp.concatenate([wt[c0:c2], zl, kr, zr, zl, -kr[half:], kr[:half], zr], axis=0)
    a_ref[0] = a_t.T.astype(BF16)
    b_ref[0] = jnp.concatenate([wt[:c0], wt[c3:]], axis=0).T.astype(BF16)


def _w_in_prep(w_in):
    depth, d, n = w_in.shape
    na = Q_LORA + KV_LORA + 2 * HEAD_PAD
    nb = n - (Q_LORA + KV_LORA + QK_ROPE)
    rows = W_IN_PREP_ROWS
    w_in = jnp.swapaxes(w_in, 1, 2)
    return pl.pallas_call(
        _w_in_prep_kernel,
        grid=(depth, d // rows),
        in_specs=[pl.BlockSpec((1, n, rows), lambda l, r: (l, 0, r))],
        out_specs=[pl.BlockSpec((1, rows, na), lambda l, r: (l, r, 0)),
                   pl.BlockSpec((1, rows, nb), lambda l, r: (l, r, 0))],
        out_shape=[jax.ShapeDtypeStruct((depth, d, na), BF16), jax.ShapeDtypeStruct((depth, d, nb), BF16)],
        compiler_params=pltpu.CompilerParams(vmem_limit_bytes=VMEM_LIMIT_BYTES),
        name="w_in_prep",
    )(w_in)


def _prep_weights(ln1_g, ln2_g, w_in, q_norm_g, w_uq, kv_norm_g, w_uk, w_uv, w_pool, pool_scale, final_g):
    depth = w_in.shape[0]
    half = QK_ROPE // 2
    w_in_a, w_in_b = _w_in_prep(w_in)

    pad_q = HEAD_PAD - (QK_NOPE + QK_ROPE)
    wuq_pad = jnp.pad(w_uq, ((0, 0), (0, 0), (0, 0), (0, pad_q)))
    wuqt = wuq_pad.reshape(depth, Q_LORA, N_HEADS * HEAD_PAD).transpose(0, 2, 1).astype(BF16)
    r = w_uq[..., QK_NOPE:]
    rot = jnp.concatenate([-r[..., half:], r[..., :half]], axis=-1)
    wuqrt = rot.reshape(depth, Q_LORA, N_HEADS * QK_ROPE).transpose(0, 2, 1).astype(BF16)
    wuk = jnp.pad(w_uk, ((0, 0), (0, 0), (0, 0), (0, HEAD_PAD - QK_NOPE)))
    wuk = wuk.reshape(depth, KV_LORA, N_HEADS * HEAD_PAD).astype(BF16)
    wuvt = w_uv.reshape(depth, KV_LORA, ATTN_DIM).transpose(0, 2, 1).astype(BF16)
    return dict(
        ln1_g=ln1_g, ln2_g=ln2_g, final_g=final_g.reshape(1, D_MODEL), q_norm_g=q_norm_g,
        kv_norm_g=kv_norm_g, pool_scale=pool_scale, w_in_a=w_in_a, w_in_b=w_in_b, wuqt=wuqt, wuqrt=wuqrt,
        wuk=wuk, wuvt=wuvt, w_pool=w_pool.astype(BF16))


def kernel(x, c, positions, ln1_g, ln2_g, w_ada, b_ada, w_in, q_norm_g, w_uq, kv_norm_g, w_uk,
           w_uv, w_pool, pool_scale, p_pool, p_attn, w_out, w_ff1, w_ff2, final_g):
    depth = w_in.shape[0]
    mod = _modulation(c, w_ada, b_ada)
    wts = _prep_weights(ln1_g, ln2_g, w_in, q_norm_g, w_uq, kv_norm_g, w_uk, w_uv, w_pool, pool_scale,
                        final_g)
    cast_names = ("p_pool", "p_attn", "w_out", "w_ff1", "w_ff2")
    f32_weights = (p_pool, p_attn, w_out, w_ff1, w_ff2)
    for layer in range(depth):
        qt, k, vt = _qkv(x, mod, positions, wts, layer)
        ot, cast = _attention(qt, k, vt, f32_weights, layer)
        x = _mix(x, ot, mod, wts, dict(zip(cast_names, cast)), layer, final=(layer == depth - 1))
    return x
```

```python
import functools
import math

import jax
import jax.numpy as jnp
from jax import lax
from jax.experimental import pallas as pl
from jax.experimental.pallas import tpu as pltpu

D_MODEL = 1024
N_HEADS = 8
QK_NOPE = 64
QK_ROPE = 32
V_DIM = 64
Q_LORA = 384
KV_LORA = 256
POOL_WINDOWS = (2, 4, 8, 16)
POOL_GROUP_DIM = 128
POOL_DIM = len(POOL_WINDOWS) * POOL_GROUP_DIM
ATTN_DIM = N_HEADS * V_DIM
D_FF = 4 * D_MODEL
N_MOD = 6
EPS = 1e-6
ROPE_THETA = 10000.0

HEAD_PAD = 128
ROPE_LO = QK_NOPE
ROPE_HI = QK_NOPE + QK_ROPE
POOL_HALO = 16

VMEM_LIMIT_BYTES = 56 * 1024 * 1024

F32 = jnp.float32
BF16 = jnp.bfloat16

TM_QKV = 1024
QKV_SUB = 512
TM_MIX = 1024
MIX_SUB = 512
TQ = 512
TK = 512
ATTN_HEADS = 4
BELOW_PAIRS_PER_TRIP = 4
F32_SUBLANES = 8
BF16_SUBLANES = 16
ACC_ROWS = V_DIM + BF16_SUBLANES
FF_CHUNK = 1024
MOD_TN = 1536


def _nt_dot(a, b):
    return lax.dot_general(a, b, (((1,), (1,)), ((), ())), preferred_element_type=F32)


def _tn_dot(a, b):
    return lax.dot_general(a, b, (((0,), (0,)), ((), ())), preferred_element_type=F32)


def _rms(x):
    return x * lax.rsqrt(jnp.mean(x * x, axis=-1, keepdims=True) + EPS)


def _sigmoid(x):
    return 1.0 / (1.0 + jnp.exp(-x))


def _mod_rows(mod_ref, chunks):
    row = pl.ds(pl.program_id(0), 1)
    return tuple(mod_ref[0, row, c * D_MODEL:(c + 1) * D_MODEL] for c in chunks)


def _layer_spec(arr, layer, **kw):
    tail = (0,) * (arr.ndim - 1)
    return pl.BlockSpec((1,) + arr.shape[1:], lambda i, j: (layer,) + tail, **kw)


def _whole_spec(arr, **kw):
    zeros = (0,) * arr.ndim
    return pl.BlockSpec(arr.shape, lambda i, j: zeros, **kw)


def _mod_kernel(c_ref, w_ref, b_ref, o_ref):
    c = c_ref[...]
    c_act = c * _sigmoid(c)
    o_ref[0] = jnp.dot(c_act.astype(BF16), w_ref[0].astype(BF16),
                       preferred_element_type=F32) + b_ref[0]


def _modulation(c, w_ada, b_ada):
    depth, d, n = w_ada.shape
    b = c.shape[0]
    rows = F32_SUBLANES
    assert b <= rows
    c_pad = jnp.pad(c, ((0, rows - b), (0, 0)))
    out = pl.pallas_call(
        _mod_kernel,
        grid=(depth, n // MOD_TN),
        in_specs=[
            pl.BlockSpec((rows, d), lambda l, j: (0, 0)),
            pl.BlockSpec((1, d, MOD_TN), lambda l, j: (l, 0, j)),
            pl.BlockSpec((1, 1, MOD_TN), lambda l, j: (l, 0, j)),
        ],
        out_specs=pl.BlockSpec((1, rows, MOD_TN), lambda l, j: (l, 0, j)),
        out_shape=jax.ShapeDtypeStruct((depth, rows, n), F32),
        compiler_params=pltpu.CompilerParams(vmem_limit_bytes=VMEM_LIMIT_BYTES),
        name="adaln_mod",
    )(c_pad, w_ada, b_ada.reshape(depth, 1, n))
    return out


def _qkv_kernel(x_ref, mod_ref, g_ref, w_in_ref, gq_ref, gkv_ref, wuqt_ref, wuqrt_ref,
                wuk_ref, wuvt_ref, pos_ref, invf_ref,
                qt_ref, k_ref, vt_ref, *, scale, layer):
    shift, scl = _mod_rows(mod_ref, (0, 1))
    toks = [slice(r0, r0 + QKV_SUB) for r0 in range(0, x_ref.shape[1], QKV_SUB)]

    hs = [((_rms(x_ref[0, tok, :]) * g_ref[layer:layer + 1, :]) * (1.0 + scl) + shift).astype(BF16)
          for tok in toks]
    zs = [jnp.dot(h, w_in_ref[0], preferred_element_type=F32) for h in hs]

    latents = []
    for z in zs:
        c_q = z[:, 0:Q_LORA]
        c_kv = z[:, Q_LORA:Q_LORA + KV_LORA]
        latents.append(((_rms(c_q) * gq_ref[layer:layer + 1, :]).astype(BF16),
                        (_rms(c_kv) * gkv_ref[layer:layer + 1, :]).astype(BF16)))

    prods = [(_nt_dot(wuqt_ref[0], cqn), _nt_dot(wuqrt_ref[0], cqn),
              jnp.dot(ckvn, wuk_ref[0], preferred_element_type=F32), _nt_dot(wuvt_ref[0], ckvn))
             for cqn, ckvn in latents]

    z_lo = jnp.zeros((ROPE_LO, QKV_SUB), F32)
    z_hi = jnp.zeros((HEAD_PAD - ROPE_HI, QKV_SUB), F32)
    for tok, z, (qt, qt_rot, k, vt) in zip(toks, zs, prods):
        ang = invf_ref[...] * pos_ref[0, :, tok].astype(F32)
        cost = jnp.cos(ang)
        sint = jnp.sin(ang)
        for hd in range(N_HEADS):
            base = hd * HEAD_PAD
            qt_ref[0, base:base + ROPE_LO, tok] = (qt[base:base + ROPE_LO] * scale).astype(BF16)
            roped = (qt[base + ROPE_LO:base + ROPE_HI] * cost
                     + qt_rot[hd * QK_ROPE:(hd + 1) * QK_ROPE] * sint)
            qt_ref[0, base + ROPE_LO:base + ROPE_HI, tok] = (roped * scale).astype(BF16)
            qt_ref[0, base + ROPE_HI:base + HEAD_PAD, tok] = (
                qt[base + ROPE_HI:base + HEAD_PAD] * scale).astype(BF16)

        krx = z[:, Q_LORA + KV_LORA:]
        kr_rot = pltpu.roll(krx, HEAD_PAD - QK_ROPE, axis=1)
        cos_tok = jnp.concatenate([z_lo, cost, z_hi], axis=0).T
        sin_tok = jnp.concatenate([z_lo, sint, z_hi], axis=0).T
        kr_full = krx * cos_tok + kr_rot * sin_tok
        nope_lane = lax.broadcasted_iota(jnp.int32, (QKV_SUB, HEAD_PAD), 1) < QK_NOPE
        for pair in range(N_HEADS // 2):
            two = k[:, pair * HEAD_PAD:(pair + 1) * HEAD_PAD]
            for odd, src in ((0, two), (1, pltpu.roll(two, QK_NOPE, axis=1))):
                base = (2 * pair + odd) * HEAD_PAD
                k_ref[0, tok, base:base + HEAD_PAD] = jnp.where(nope_lane, src, kr_full).astype(BF16)

        vt_ref[0, :, tok] = vt.astype(BF16)


def _qkv(x, mod, positions, wts, layer):
    b, s, d = x.shape
    tm = TM_QKV
    inv_freq = ROPE_THETA ** (-jnp.arange(0, QK_ROPE, 2, dtype=F32) / QK_ROPE)
    invf_col = jnp.concatenate([inv_freq, inv_freq]).reshape(QK_ROPE, 1)
    scale = math.log2(math.e) / math.sqrt(QK_NOPE + QK_ROPE)
    stacked = [wts["ln1_g"], wts["w_in_a"], wts["q_norm_g"], wts["kv_norm_g"], wts["wuqt"], wts["wuqrt"],
               wts["wuk"], wts["wuvt"]]
    return pl.pallas_call(
        functools.partial(_qkv_kernel, scale=scale, layer=layer),
        grid=(b, s // tm),
        in_specs=[
            pl.BlockSpec((1, tm, d), lambda i, j: (i, j, 0)),
            _layer_spec(mod, layer),
            _whole_spec(wts["ln1_g"]),
            _layer_spec(wts["w_in_a"], layer),
            _whole_spec(wts["q_norm_g"]),
            _whole_spec(wts["kv_norm_g"]),
            _layer_spec(wts["wuqt"], layer),
            _layer_spec(wts["wuqrt"], layer),
            _layer_spec(wts["wuk"], layer),
            _layer_spec(wts["wuvt"], layer),
            pl.BlockSpec((1, 1, tm), lambda i, j: (i, 0, j)),
            _whole_spec(invf_col),
        ],
        out_specs=[
            pl.BlockSpec((1, N_HEADS * HEAD_PAD, tm), lambda i, j: (i, 0, j)),
            pl.BlockSpec((1, tm, N_HEADS * HEAD_PAD), lambda i, j: (i, j, 0)),
            pl.BlockSpec((1, ATTN_DIM, tm), lambda i, j: (i, 0, j)),
        ],
        out_shape=[
            jax.ShapeDtypeStruct((b, N_HEADS * HEAD_PAD, s), BF16),
            jax.ShapeDtypeStruct((b, s, N_HEADS * HEAD_PAD), BF16),
            jax.ShapeDtypeStruct((b, ATTN_DIM, s), BF16),
        ],
        compiler_params=pltpu.CompilerParams(
            dimension_semantics=("arbitrary", "arbitrary"),
            vmem_limit_bytes=VMEM_LIMIT_BYTES),
        name="qkv_proj",
    )(x, mod, *stacked, positions.reshape(b, 1, s), invf_col)


def _attn_kernel(qt_ref, k_ref, vt_ref, *refs, n_q, n_cast):
    cast_in, o_ref, cast_out = refs[:n_cast], refs[n_cast], refs[n_cast + 1:2 * n_cast + 1]
    s_ref, acc_ref, m_ref = refs[2 * n_cast + 1:]

    n_below = n_q * (n_q - 1) // 2
    hq = TQ // 2
    ones_rows = jnp.ones((ACC_ROWS - V_DIM, TK), BF16)
    half_mask = (lax.broadcasted_iota(jnp.int32, (hq, hq), 1)
                 >= lax.broadcasted_iota(jnp.int32, (hq, hq), 0))

    def keys(hd, off, n):
        return k_ref[0, pl.ds(off, n), hd * HEAD_PAD:(hd + 1) * HEAD_PAD]

    def queries(hd, off, n):
        return qt_ref[0, hd * HEAD_PAD:(hd + 1) * HEAD_PAD, pl.ds(off, n)]

    def values(hd, off, n):
        return jnp.concatenate(
            [vt_ref[0, hd * V_DIM:(hd + 1) * V_DIM, pl.ds(off, n)], ones_rows[:, :n]], axis=0)

    def col_max(s):
        return jnp.max(s, axis=0, keepdims=True)

    def produce_below(slot, hd, qi, j):
        s = jnp.dot(keys(hd, pl.multiple_of(j * TK, TK), TK), queries(hd, pl.multiple_of(qi * TQ, TQ), TQ),
                    preferred_element_type=F32)
        s_ref[slot, hd] = s
        return col_max(s)

    def consume_below(slot, hd, qi, j, tile_max):
        m = m_ref[qi, hd]
        m_new = jnp.maximum(m, tile_max)
        p = jnp.exp2(s_ref[slot, hd] - m_new).astype(BF16)
        acc_ref[qi, hd] = (jnp.exp2(m - m_new) * acc_ref[qi, hd]
                           + jnp.dot(values(hd, pl.multiple_of(j * TK, TK), TK), p, preferred_element_type=F32))
        m_ref[qi, hd] = m_new

    def produce_diag(slot, hd, d):
        off = d * TQ
        s_ref[slot, hd, 0:hq, :] = jnp.dot(keys(hd, off, hq), queries(hd, off, TQ),
                                           preferred_element_type=F32)
        s_ref[slot, hd, hq:TK, hq:TQ] = jnp.dot(keys(hd, off + hq, hq), queries(hd, off + hq, hq),
                                                preferred_element_type=F32)

    def consume_diag(slot, hd, d):
        off = d * TQ
        neg = jnp.finfo(F32).min
        s_tl = jnp.where(half_mask, s_ref[slot, hd, 0:hq, 0:hq], neg)
        s_tr = s_ref[slot, hd, 0:hq, hq:TQ]
        s_br = jnp.where(half_mask, s_ref[slot, hd, hq:TK, hq:TQ], neg)
        m = m_ref[d, hd]
        m_l = jnp.maximum(m[:, 0:hq], col_max(s_tl))
        m_r = jnp.maximum(m[:, hq:TQ], jnp.maximum(col_max(s_tr), col_max(s_br)))
        p_l = jnp.exp2(s_tl - m_l).astype(BF16)
        p_r = jnp.concatenate([jnp.exp2(s_tr - m_r), jnp.exp2(s_br - m_r)], axis=0).astype(BF16)
        acc = acc_ref[d, hd]
        acc_l = (jnp.exp2(m[:, 0:hq] - m_l) * acc[:, 0:hq]
                 + jnp.dot(values(hd, off, hq), p_l, preferred_element_type=F32))
        acc_r = (jnp.exp2(m[:, hq:TQ] - m_r) * acc[:, hq:TQ]
                 + jnp.dot(values(hd, off, TK), p_r, preferred_element_type=F32))
        rows = slice(hd * V_DIM, (hd + 1) * V_DIM)
        o_ref[0, rows, pl.ds(off, hq)] = (
            acc_l[0:V_DIM] * (1.0 / acc_l[V_DIM:V_DIM + 1])).astype(o_ref.dtype)
        o_ref[0, rows, pl.ds(off + hq, hq)] = (
            acc_r[0:V_DIM] * (1.0 / acc_r[V_DIM:V_DIM + 1])).astype(o_ref.dtype)

    heads = range(ATTN_HEADS)

    def next_below(qi, j):
        row_end = j + 1 == qi
        return jnp.where(row_end, qi + 1, qi), jnp.where(row_end, 0, j + 1)

    def below_step(slot, tile, maxes, to_diag=False):
        nxt = next_below(*tile)
        new_maxes = []
        for hd in heads:
            if to_diag:
                produce_diag(1 - slot, hd, 0)
            else:
                new_maxes.append(produce_below(1 - slot, hd, *nxt))
            consume_below(slot, hd, *tile, maxes[hd])
        return nxt, tuple(new_maxes)

    def diag_step(slot, d, produce_next=True):
        for hd in heads:
            if produce_next:
                produce_diag(1 - slot, hd, d + 1)
            consume_diag(slot, hd, d)
        return d + 1

    tile0 = (jnp.int32(1), jnp.int32(0))
    maxes0 = tuple(produce_below(0, hd, *tile0) for hd in heads)

    acc_ref[...] = jnp.zeros(acc_ref.shape, F32)
    m_ref[...] = jnp.full(m_ref.shape, jnp.finfo(F32).min, F32)
    for src, dst in zip(cast_in, cast_out):
        dst[...] = src[...].astype(BF16)

    def below_pair(carry):
        tile, maxes = carry
        tile, maxes = below_step(0, tile, maxes)
        return below_step(1, tile, maxes)

    def below_trip(_, carry):
        for _ in range(BELOW_PAIRS_PER_TRIP):
            carry = below_pair(carry)
        return carry

    n_pairs = n_below // 2 - 1
    carry = lax.fori_loop(0, n_pairs // BELOW_PAIRS_PER_TRIP, below_trip, (tile0, maxes0))
    for _ in range(n_pairs % BELOW_PAIRS_PER_TRIP):
        carry = below_pair(carry)
    tile, maxes = below_step(0, *carry)
    below_step(1, tile, maxes, to_diag=True)

    for d in range(n_q):
        diag_step(d % 2, d, produce_next=d + 1 < n_q)


def _attention(qt, k, vt, f32_weights, layer):
    b, _, s = qt.shape
    hb = ATTN_HEADS
    groups = N_HEADS // hb
    n_steps = b * groups
    n_q = s // TQ
    assert n_q % 2 == 0 and (n_q * (n_q - 1) // 2) % 2 == 0, "both tile streams run two tiles per loop trip"

    def rows_per_step(w):
        rows = w.shape[1] // n_steps
        assert rows * n_steps == w.shape[1] and rows % BF16_SUBLANES == 0, w.shape
        return rows

    cast_in = [pl.BlockSpec((1, rows_per_step(w), w.shape[2]), lambda i, h: (layer, i * groups + h, 0))
               for w in f32_weights]
    cast_out = [pl.BlockSpec((1, rows_per_step(w), w.shape[2]), lambda i, h: (0, i * groups + h, 0))
                for w in f32_weights]
    out = pl.pallas_call(
        functools.partial(_attn_kernel, n_q=n_q, n_cast=len(f32_weights)),
        grid=(b, groups),
        in_specs=[
            pl.BlockSpec((1, hb * HEAD_PAD, s), lambda i, h: (i, h, 0)),
            pl.BlockSpec((1, s, hb * HEAD_PAD), lambda i, h: (i, 0, h)),
            pl.BlockSpec((1, hb * V_DIM, s), lambda i, h: (i, h, 0)),
        ] + cast_in,
        out_specs=[pl.BlockSpec((1, hb * V_DIM, s), lambda i, h: (i, h, 0))] + cast_out,
        out_shape=[jax.ShapeDtypeStruct((b, ATTN_DIM, s), BF16)]
        + [jax.ShapeDtypeStruct((1,) + w.shape[1:], BF16) for w in f32_weights],
        scratch_shapes=[
            pltpu.VMEM((2, hb, TK, TQ), F32),
            pltpu.VMEM((n_q, hb, ACC_ROWS, TQ), F32),
            pltpu.VMEM((n_q, hb, 1, TQ), F32),
        ],
        compiler_params=pltpu.CompilerParams(
            dimension_semantics=("arbitrary", "arbitrary"),
            vmem_limit_bytes=VMEM_LIMIT_BYTES),
        name="mla_attention",
    )(qt, k, vt, *f32_weights)
    return out[0], out[1:]


def _mix_kernel(x_ref, ot_ref, mod_ref, g1_ref, g2_ref, gf_ref, w_in_ref, w_pool_ref, pscale_ref,
                p_pool_ref, p_attn_ref, w_out_ref, w_ff1_ref, w_ff2_ref,
                o_ref, uext_ref, *, final, layer):
    tm = x_ref.shape[1]
    si = pl.program_id(1)

    @pl.when(si == 0)
    def _():
        uext_ref[0:POOL_HALO, :] = jnp.zeros((POOL_HALO, POOL_DIM), F32)

    @pl.when(si > 0)
    def _():
        uext_ref[0:POOL_HALO, :] = uext_ref[tm:tm + POOL_HALO, :]

    shift1, scale1, gate1, shift2, scale2, gate2 = _mod_rows(mod_ref, range(N_MOD))

    for r0 in range(0, tm, MIX_SUB):
        tok = slice(r0, r0 + MIX_SUB)
        x = x_ref[0, tok, :]

        y_b = _tn_dot(ot_ref[0, :, tok], p_attn_ref[0])
        h = ((_rms(x) * g1_ref[layer:layer + 1, :]) * (1.0 + scale1) + shift1).astype(BF16)
        u = jnp.dot(h, w_in_ref[0, :, 0:POOL_DIM], preferred_element_type=F32)
        gz = jnp.dot(h, w_in_ref[0, :, POOL_DIM:], preferred_element_type=F32)
        gz_a = gz[:, 0:D_MODEL]
        gz_b = gz[:, D_MODEL:]

        uext_ref[POOL_HALO + r0:POOL_HALO + r0 + MIX_SUB, :] = u
        head_pos = si * tm + r0 + lax.broadcasted_iota(jnp.int32, (POOL_HALO, 1), 0)
        pooled = []
        for g, w in enumerate(POOL_WINDOWS):
            eg = uext_ref[r0:r0 + POOL_HALO + MIX_SUB,
                          g * POOL_GROUP_DIM:(g + 1) * POOL_GROUP_DIM]
            win = eg
            k = 1
            while k < w:
                win = win + pltpu.roll(win, k, axis=0)
                k *= 2
            win = win[POOL_HALO:]
            ug = eg[POOL_HALO:]
            inv_head = 1.0 / jnp.minimum(head_pos + 1, w).astype(F32)
            mean = jnp.concatenate([win[:POOL_HALO] * inv_head, win[POOL_HALO:] * (1.0 / w)], axis=0)
            yg = jnp.dot((mean - ug).astype(BF16), w_pool_ref[0, g], preferred_element_type=F32)
            pooled.append(yg)
        y_pool = jnp.concatenate(pooled, axis=-1) * pscale_ref[layer:layer + 1, :]
        y_a = jnp.dot(y_pool.astype(BF16), p_pool_ref[0], preferred_element_type=F32)

        merged = _sigmoid(gz_a) * y_a + _sigmoid(gz_b) * y_b
        x1 = x + gate1 * jnp.dot(merged.astype(BF16), w_out_ref[0], preferred_element_type=F32)

        h2 = ((_rms(x1) * g2_ref[layer:layer + 1, :]) * (1.0 + scale2) + shift2).astype(BF16)
        hidden = [jnp.square(jnp.maximum(
            jnp.dot(h2, w_ff1_ref[0, :, c0:c0 + FF_CHUNK], preferred_element_type=F32), 0.0)).astype(BF16)
            for c0 in range(0, D_FF, FF_CHUNK)]
        ff = jnp.dot(jnp.concatenate(hidden, axis=-1), w_ff2_ref[0], preferred_element_type=F32)
        x2 = x1 + gate2 * ff
        if final:
            x2 = _rms(x2) * gf_ref[...]
        o_ref[0, tok, :] = x2


def _mix(x, ot, mod, wts, layer_wts, layer, *, final):
    b, s, d = x.shape
    tm = TM_MIX
    once = dict(pipeline_mode=pl.Buffered(1))
    names = ["ln1_g", "ln2_g", "final_g", "w_in_b", "w_pool", "pool_scale", "p_pool", "p_attn", "w_out",
             "w_ff1", "w_ff2"]
    whole = {"ln1_g", "ln2_g", "final_g", "pool_scale"}
    wts = {**wts, **layer_wts}
    specs = [_whole_spec(wts[n], **once) if n in whole
             else _layer_spec(wts[n], 0 if n in layer_wts else layer, **once)
             for n in names]
    return pl.pallas_call(
        functools.partial(_mix_kernel, final=final, layer=layer),
        grid=(b, s // tm),
        in_specs=[
            pl.BlockSpec((1, tm, d), lambda i, j: (i, j, 0)),
            pl.BlockSpec((1, ATTN_DIM, tm), lambda i, j: (i, 0, j)),
            _layer_spec(mod, layer, **once),
        ] + specs,
        out_specs=pl.BlockSpec((1, tm, d), lambda i, j: (i, j, 0)),
        out_shape=jax.ShapeDtypeStruct((b, s, d), F32),
        scratch_shapes=[pltpu.VMEM((tm + POOL_HALO, POOL_DIM), F32)],
        compiler_params=pltpu.CompilerParams(
            dimension_semantics=("arbitrary", "arbitrary"),
            vmem_limit_bytes=VMEM_LIMIT_BYTES),
        name="mix_mlp",
    )(x, ot, mod, *[wts[n] for n in names])


W_IN_PREP_ROWS = 256


def _w_in_prep_kernel(w_ref, a_ref, b_ref):
    wt = w_ref[0]
    rows = wt.shape[1]
    c0 = POOL_DIM
    c2 = c0 + Q_LORA + KV_LORA
    c3 = c2 + QK_ROPE
    half = QK_ROPE // 2
    kr = wt[c2:c3]
    zl = jnp.zeros((ROPE_LO, rows), F32)
    a_t = jnp.concatenate([wt[c0:c2], zl, kr, -kr[half:], kr[:half]], axis=0)
    a_ref[0] = a_t.T.astype(BF16)
    b_ref[0] = jnp.concatenate([wt[:c0], wt[c3:]], axis=0).T.astype(BF16)


def _w_in_prep(w_in):
    depth, d, n = w_in.shape
    na = Q_LORA + KV_LORA + HEAD_PAD
    nb = n - (Q_LORA + KV_LORA + QK_ROPE)
    rows = W_IN_PREP_ROWS
    w_in = jnp.swapaxes(w_in, 1, 2)
    return pl.pallas_call(
        _w_in_prep_kernel,
        grid=(depth, d // rows),
        in_specs=[pl.BlockSpec((1, n, rows), lambda l, r: (l, 0, r))],
        out_specs=[pl.BlockSpec((1, rows, na), lambda l, r: (l, r, 0)),
                   pl.BlockSpec((1, rows, nb), lambda l, r: (l, r, 0))],
        out_shape=[jax.ShapeDtypeStruct((depth, d, na), BF16), jax.ShapeDtypeStruct((depth, d, nb), BF16)],
        compiler_params=pltpu.CompilerParams(vmem_limit_bytes=VMEM_LIMIT_BYTES),
        name="w_in_prep",
    )(w_in)


def _prep_weights(ln1_g, ln2_g, w_in, q_norm_g, w_uq, kv_norm_g, w_uk, w_uv, w_pool, pool_scale, final_g):
    depth = w_in.shape[0]
    half = QK_ROPE // 2
    w_in_a, w_in_b = _w_in_prep(w_in)

    pad_q = HEAD_PAD - (QK_NOPE + QK_ROPE)
    wuq_pad = jnp.pad(w_uq, ((0, 0), (0, 0), (0, 0), (0, pad_q)))
    wuqt = wuq_pad.reshape(depth, Q_LORA, N_HEADS * HEAD_PAD).transpose(0, 2, 1).astype(BF16)
    r = w_uq[..., QK_NOPE:]
    rot = jnp.concatenate([-r[..., half:], r[..., :half]], axis=-1)
    wuqrt = rot.reshape(depth, Q_LORA, N_HEADS * QK_ROPE).transpose(0, 2, 1).astype(BF16)
    wuk = w_uk.reshape(depth, KV_LORA, N_HEADS * QK_NOPE).astype(BF16)
    wuvt = w_uv.reshape(depth, KV_LORA, ATTN_DIM).transpose(0, 2, 1).astype(BF16)
    return dict(
        ln1_g=ln1_g, ln2_g=ln2_g, final_g=final_g.reshape(1, D_MODEL), q_norm_g=q_norm_g,
        kv_norm_g=kv_norm_g, pool_scale=pool_scale, w_in_a=w_in_a, w_in_b=w_in_b, wuqt=wuqt, wuqrt=wuqrt,
        wuk=wuk, wuvt=wuvt, w_pool=w_pool.astype(BF16))


def kernel(x, c, positions, ln1_g, ln2_g, w_ada, b_ada, w_in, q_norm_g, w_uq, kv_norm_g, w_uk,
           w_uv, w_pool, pool_scale, p_pool, p_attn, w_out, w_ff1, w_ff2, final_g):
    depth = w_in.shape[0]
    mod = _modulation(c, w_ada, b_ada)
    wts = _prep_weights(ln1_g, ln2_g, w_in, q_norm_g, w_uq, kv_norm_g, w_uk, w_uv, w_pool, pool_scale,
                        final_g)
    cast_names = ("p_pool", "p_attn", "w_out", "w_ff1", "w_ff2")
    f32_weights = (p_pool, p_attn, w_out, w_ff1, w_ff2)
    for layer in range(depth):
        qt, k, vt = _qkv(x, mod, positions, wts, layer)
        ot, cast = _attention(qt, k, vt, f32_weights, layer)
        x = _mix(x, ot, mod, wts, dict(zip(cast_names, cast)), layer, final=(layer == depth - 1))
    return x
```

```python
import functools
import math

import jax
import jax.numpy as jnp
from jax import lax
from jax.experimental import pallas as pl
from jax.experimental.pallas import tpu as pltpu

D_MODEL = 1024
N_HEADS = 8
QK_NOPE = 64
QK_ROPE = 32
V_DIM = 64
Q_LORA = 384
KV_LORA = 256
POOL_WINDOWS = (2, 4, 8, 16)
POOL_GROUP_DIM = 128
POOL_DIM = len(POOL_WINDOWS) * POOL_GROUP_DIM
ATTN_DIM = N_HEADS * V_DIM
D_FF = 4 * D_MODEL
N_MOD = 6
EPS = 1e-6
ROPE_THETA = 10000.0

HEAD_PAD = 128
ROPE_LO = QK_NOPE
ROPE_HI = QK_NOPE + QK_ROPE
POOL_HALO = 16

VMEM_LIMIT_BYTES = 56 * 1024 * 1024

F32 = jnp.float32
BF16 = jnp.bfloat16

TM_QKV = 1024
QKV_SUB = 512
TM_MIX = 1024
MIX_SUB = 512
TQ = 512
TK = 512
ATTN_HEADS = 4
BELOW_PAIRS_PER_TRIP = 4
F32_SUBLANES = 8
BF16_SUBLANES = 16
ACC_ROWS = V_DIM + BF16_SUBLANES
FF_CHUNK = 1024
MOD_TN = 1536


def _nt_dot(a, b):
    return lax.dot_general(a, b, (((1,), (1,)), ((), ())), preferred_element_type=F32)


def _tn_dot(a, b):
    return lax.dot_general(a, b, (((0,), (0,)), ((), ())), preferred_element_type=F32)


def _rms(x):
    return x * lax.rsqrt(jnp.mean(x * x, axis=-1, keepdims=True) + EPS)


def _sigmoid(x):
    return 1.0 / (1.0 + jnp.exp(-x))


def _mod_rows(mod_ref, chunks):
    row = pl.ds(pl.program_id(0), 1)
    return tuple(mod_ref[0, row, c * D_MODEL:(c + 1) * D_MODEL] for c in chunks)


def _layer_spec(arr, layer, **kw):
    tail = (0,) * (arr.ndim - 1)
    return pl.BlockSpec((1,) + arr.shape[1:], lambda i, j: (layer,) + tail, **kw)


def _whole_spec(arr, **kw):
    zeros = (0,) * arr.ndim
    return pl.BlockSpec(arr.shape, lambda i, j: zeros, **kw)


def _mod_kernel(c_ref, w_ref, b_ref, o_ref):
    c = c_ref[...]
    c_act = c * _sigmoid(c)
    o_ref[0] = jnp.dot(c_act.astype(BF16), w_ref[0].astype(BF16),
                       preferred_element_type=F32) + b_ref[0]


def _modulation(c, w_ada, b_ada):
    depth, d, n = w_ada.shape
    b = c.shape[0]
    rows = F32_SUBLANES
    assert b <= rows
    c_pad = jnp.pad(c, ((0, rows - b), (0, 0)))
    out = pl.pallas_call(
        _mod_kernel,
        grid=(depth, n // MOD_TN),
        in_specs=[
            pl.BlockSpec((rows, d), lambda l, j: (0, 0)),
            pl.BlockSpec((1, d, MOD_TN), lambda l, j: (l, 0, j)),
            pl.BlockSpec((1, 1, MOD_TN), lambda l, j: (l, 0, j)),
        ],
        out_specs=pl.BlockSpec((1, rows, MOD_TN), lambda l, j: (l, 0, j)),
        out_shape=jax.ShapeDtypeStruct((depth, rows, n), F32),
        compiler_params=pltpu.CompilerParams(vmem_limit_bytes=VMEM_LIMIT_BYTES),
        name="adaln_mod",
    )(c_pad, w_ada, b_ada.reshape(depth, 1, n))
    return out


def _qkv_kernel(x_ref, mod_ref, g_ref, w_in_ref, gq_ref, gkv_ref, wuqt_ref,
                wuk_ref, wuvt_ref, pos_ref, invf_ref,
                qt_ref, k_ref, vt_ref, *, scale, layer):
    shift, scl = _mod_rows(mod_ref, (0, 1))
    toks = [slice(r0, r0 + QKV_SUB) for r0 in range(0, x_ref.shape[1], QKV_SUB)]

    hs = [((_rms(x_ref[0, tok, :]) * g_ref[layer:layer + 1, :]) * (1.0 + scl) + shift).astype(BF16)
          for tok in toks]
    zs = [jnp.dot(h, w_in_ref[0], preferred_element_type=F32) for h in hs]

    latents = []
    for z in zs:
        c_q = z[:, 0:Q_LORA]
        c_kv = z[:, Q_LORA:Q_LORA + KV_LORA]
        latents.append(((_rms(c_q) * gq_ref[layer:layer + 1, :]).astype(BF16),
                        (_rms(c_kv) * gkv_ref[layer:layer + 1, :]).astype(BF16)))

    prods = [(_nt_dot(wuqt_ref[0], cqn),
              jnp.dot(ckvn, wuk_ref[0], preferred_element_type=F32), _nt_dot(wuvt_ref[0], ckvn))
             for cqn, ckvn in latents]

    half = QK_ROPE // 2
    z_lo = jnp.zeros((ROPE_LO, QKV_SUB), F32)
    z_hi = jnp.zeros((HEAD_PAD - ROPE_HI, QKV_SUB), F32)
    for tok, z, (qt, k, vt) in zip(toks, zs, prods):
        ang = invf_ref[...] * pos_ref[0, :, tok].astype(F32)
        cost = jnp.cos(ang)
        sint = jnp.sin(ang)
        for hd in range(N_HEADS):
            base = hd * HEAD_PAD
            qt_ref[0, base:base + ROPE_LO, tok] = (qt[base:base + ROPE_LO] * scale).astype(BF16)
            x1 = qt[base + ROPE_LO:base + ROPE_LO + half]
            x2 = qt[base + ROPE_LO + half:base + ROPE_HI]
            roped = jnp.concatenate([x1 * cost[:half] - x2 * sint[:half],
                                     x2 * cost[half:] + x1 * sint[half:]], axis=0)
            qt_ref[0, base + ROPE_LO:base + ROPE_HI, tok] = (roped * scale).astype(BF16)
            qt_ref[0, base + ROPE_HI:base + HEAD_PAD, tok] = (
                qt[base + ROPE_HI:base + HEAD_PAD] * scale).astype(BF16)

        krx = z[:, Q_LORA + KV_LORA:]
        kr_rot = pltpu.roll(krx, HEAD_PAD - QK_ROPE, axis=1)
        cos_tok = jnp.concatenate([z_lo, cost, z_hi], axis=0).T
        sin_tok = jnp.concatenate([z_lo, sint, z_hi], axis=0).T
        kr_full = krx * cos_tok + kr_rot * sin_tok
        nope_lane = lax.broadcasted_iota(jnp.int32, (QKV_SUB, HEAD_PAD), 1) < QK_NOPE
        for pair in range(N_HEADS // 2):
            two = k[:, pair * HEAD_PAD:(pair + 1) * HEAD_PAD]
            for odd, src in ((0, two), (1, pltpu.roll(two, QK_NOPE, axis=1))):
                base = (2 * pair + odd) * HEAD_PAD
                k_ref[0, tok, base:base + HEAD_PAD] = jnp.where(nope_lane, src, kr_full).astype(BF16)

        vt_ref[0, :, tok] = vt.astype(BF16)


def _qkv(x, mod, positions, wts, layer):
    b, s, d = x.shape
    tm = TM_QKV
    inv_freq = ROPE_THETA ** (-jnp.arange(0, QK_ROPE, 2, dtype=F32) / QK_ROPE)
    invf_col = jnp.concatenate([inv_freq, inv_freq]).reshape(QK_ROPE, 1)
    scale = math.log2(math.e) / math.sqrt(QK_NOPE + QK_ROPE)
    stacked = [wts["ln1_g"], wts["w_in_a"], wts["q_norm_g"], wts["kv_norm_g"], wts["wuqt"],
               wts["wuk"], wts["wuvt"]]
    return pl.pallas_call(
        functools.partial(_qkv_kernel, scale=scale, layer=layer),
        grid=(b, s // tm),
        in_specs=[
            pl.BlockSpec((1, tm, d), lambda i, j: (i, j, 0)),
            _layer_spec(mod, layer),
            _whole_spec(wts["ln1_g"]),
            _layer_spec(wts["w_in_a"], layer),
            _whole_spec(wts["q_norm_g"]),
            _whole_spec(wts["kv_norm_g"]),
            _layer_spec(wts["wuqt"], layer),
            _layer_spec(wts["wuk"], layer),
            _layer_spec(wts["wuvt"], layer),
            pl.BlockSpec((1, 1, tm), lambda i, j: (i, 0, j)),
            _whole_spec(invf_col),
        ],
        out_specs=[
            pl.BlockSpec((1, N_HEADS * HEAD_PAD, tm), lambda i, j: (i, 0, j)),
            pl.BlockSpec((1, tm, N_HEADS * HEAD_PAD), lambda i, j: (i, j, 0)),
            pl.BlockSpec((1, ATTN_DIM, tm), lambda i, j: (i, 0, j)),
        ],
        out_shape=[
            jax.ShapeDtypeStruct((b, N_HEADS * HEAD_PAD, s), BF16),
            jax.ShapeDtypeStruct((b, s, N_HEADS * HEAD_PAD), BF16),
            jax.ShapeDtypeStruct((b, ATTN_DIM, s), BF16),
        ],
        compiler_params=pltpu.CompilerParams(
            dimension_semantics=("arbitrary", "arbitrary"),
            vmem_limit_bytes=VMEM_LIMIT_BYTES),
        name="qkv_proj",
    )(x, mod, *stacked, positions.reshape(b, 1, s), invf_col)


def _attn_kernel(qt_ref, k_ref, vt_ref, *refs, n_q, n_cast):
    cast_in, o_ref, cast_out = refs[:n_cast], refs[n_cast], refs[n_cast + 1:2 * n_cast + 1]
    s_ref, acc_ref, m_ref = refs[2 * n_cast + 1:]

    n_below = n_q * (n_q - 1) // 2
    hq = TQ // 2
    ones_rows = jnp.ones((ACC_ROWS - V_DIM, TK), BF16)
    half_mask = (lax.broadcasted_iota(jnp.int32, (hq, hq), 1)
                 >= lax.broadcasted_iota(jnp.int32, (hq, hq), 0))

    def keys(hd, off, n):
        return k_ref[0, pl.ds(off, n), hd * HEAD_PAD:(hd + 1) * HEAD_PAD]

    def queries(hd, off, n):
        return qt_ref[0, hd * HEAD_PAD:(hd + 1) * HEAD_PAD, pl.ds(off, n)]

    def values(hd, off, n):
        return jnp.concatenate(
            [vt_ref[0, hd * V_DIM:(hd + 1) * V_DIM, pl.ds(off, n)], ones_rows[:, :n]], axis=0)

    def col_max(s):
        return jnp.max(s, axis=0, keepdims=True)

    def produce_below(slot, hd, qi, j):
        s = jnp.dot(keys(hd, pl.multiple_of(j * TK, TK), TK), queries(hd, pl.multiple_of(qi * TQ, TQ), TQ),
                    preferred_element_type=F32)
        s_ref[slot, hd] = s
        return col_max(s)

    def consume_below(slot, hd, qi, j, tile_max):
        m = m_ref[qi, hd]
        m_new = jnp.maximum(m, tile_max)
        p = jnp.exp2(s_ref[slot, hd] - m_new).astype(BF16)
        acc_ref[qi, hd] = (jnp.exp2(m - m_new) * acc_ref[qi, hd]
                           + jnp.dot(values(hd, pl.multiple_of(j * TK, TK), TK), p, preferred_element_type=F32))
        m_ref[qi, hd] = m_new

    def produce_diag(slot, hd, d):
        off = d * TQ
        s_ref[slot, hd, 0:hq, :] = jnp.dot(keys(hd, off, hq), queries(hd, off, TQ),
                                           preferred_element_type=F32)
        s_ref[slot, hd, hq:TK, hq:TQ] = jnp.dot(keys(hd, off + hq, hq), queries(hd, off + hq, hq),
                                                preferred_element_type=F32)

    def consume_diag(slot, hd, d):
        off = d * TQ
        neg = jnp.finfo(F32).min
        s_tl = jnp.where(half_mask, s_ref[slot, hd, 0:hq, 0:hq], neg)
        s_tr = s_ref[slot, hd, 0:hq, hq:TQ]
        s_br = jnp.where(half_mask, s_ref[slot, hd, hq:TK, hq:TQ], neg)
        m = m_ref[d, hd]
        m_l = jnp.maximum(m[:, 0:hq], col_max(s_tl))
        m_r = jnp.maximum(m[:, hq:TQ], jnp.maximum(col_max(s_tr), col_max(s_br)))
        p_l = jnp.exp2(s_tl - m_l).astype(BF16)
        p_r = jnp.concatenate([jnp.exp2(s_tr - m_r), jnp.exp2(s_br - m_r)], axis=0).astype(BF16)
        acc = acc_ref[d, hd]
        acc_l = (jnp.exp2(m[:, 0:hq] - m_l) * acc[:, 0:hq]
                 + jnp.dot(values(hd, off, hq), p_l, preferred_element_type=F32))
        acc_r = (jnp.exp2(m[:, hq:TQ] - m_r) * acc[:, hq:TQ]
                 + jnp.dot(values(hd, off, TK), p_r, preferred_element_type=F32))
        rows = slice(hd * V_DIM, (hd + 1) * V_DIM)
        o_ref[0, rows, pl.ds(off, hq)] = (
            acc_l[0:V_DIM] * (1.0 / acc_l[V_DIM:V_DIM + 1])).astype(o_ref.dtype)
        o_ref[0, rows, pl.ds(off + hq, hq)] = (
            acc_r[0:V_DIM] * (1.0 / acc_r[V_DIM:V_DIM + 1])).astype(o_ref.dtype)

    heads = range(ATTN_HEADS)

    def next_below(qi, j):
        row_end = j + 1 == qi
        return jnp.where(row_end, qi + 1, qi), jnp.where(row_end, 0, j + 1)

    def below_step(slot, tile, maxes, to_diag=False):
        nxt = next_below(*tile)
        new_maxes = []
        for hd in heads:
            if to_diag:
                produce_diag(1 - slot, hd, 0)
            else:
                new_maxes.append(produce_below(1 - slot, hd, *nxt))
            consume_below(slot, hd, *tile, maxes[hd])
        return nxt, tuple(new_maxes)

    def diag_step(slot, d, produce_next=True):
        for hd in heads:
            if produce_next:
                produce_diag(1 - slot, hd, d + 1)
            consume_diag(slot, hd, d)
        return d + 1

    tile0 = (jnp.int32(1), jnp.int32(0))
    maxes0 = tuple(produce_below(0, hd, *tile0) for hd in heads)

    acc_ref[...] = jnp.zeros(acc_ref.shape, F32)
    m_ref[...] = jnp.full(m_ref.shape, jnp.finfo(F32).min, F32)
    for src, dst in zip(cast_in, cast_out):
        dst[...] = src[...].astype(BF16)

    def below_pair(carry):
        tile, maxes = carry
        tile, maxes = below_step(0, tile, maxes)
        return below_step(1, tile, maxes)

    def below_trip(_, carry):
        for _ in range(BELOW_PAIRS_PER_TRIP):
            carry = below_pair(carry)
        return carry

    n_pairs = n_below // 2 - 1
    carry = lax.fori_loop(0, n_pairs // BELOW_PAIRS_PER_TRIP, below_trip, (tile0, maxes0))
    for _ in range(n_pairs % BELOW_PAIRS_PER_TRIP):
        carry = below_pair(carry)
    tile, maxes = below_step(0, *carry)
    below_step(1, tile, maxes, to_diag=True)

    for d in range(n_q):
        diag_step(d % 2, d, produce_next=d + 1 < n_q)


def _attention(qt, k, vt, f32_weights, layer):
    b, _, s = qt.shape
    hb = ATTN_HEADS
    groups = N_HEADS // hb
    n_steps = b * groups
    n_q = s // TQ
    assert n_q % 2 == 0 and (n_q * (n_q - 1) // 2) % 2 == 0, "both tile streams run two tiles per loop trip"

    def rows_per_step(w):
        rows = w.shape[1] // n_steps
        assert rows * n_steps == w.shape[1] and rows % BF16_SUBLANES == 0, w.shape
        return rows

    cast_in = [pl.BlockSpec((1, rows_per_step(w), w.shape[2]), lambda i, h: (layer, i * groups + h, 0))
               for w in f32_weights]
    cast_out = [pl.BlockSpec((1, rows_per_step(w), w.shape[2]), lambda i, h: (0, i * groups + h, 0))
                for w in f32_weights]
    out = pl.pallas_call(
        functools.partial(_attn_kernel, n_q=n_q, n_cast=len(f32_weights)),
        grid=(b, groups),
        in_specs=[
            pl.BlockSpec((1, hb * HEAD_PAD, s), lambda i, h: (i, h, 0)),
            pl.BlockSpec((1, s, hb * HEAD_PAD), lambda i, h: (i, 0, h)),
            pl.BlockSpec((1, hb * V_DIM, s), lambda i, h: (i, h, 0)),
        ] + cast_in,
        out_specs=[pl.BlockSpec((1, hb * V_DIM, s), lambda i, h: (i, h, 0))] + cast_out,
        out_shape=[jax.ShapeDtypeStruct((b, ATTN_DIM, s), BF16)]
        + [jax.ShapeDtypeStruct((1,) + w.shape[1:], BF16) for w in f32_weights],
        scratch_shapes=[
            pltpu.VMEM((2, hb, TK, TQ), F32),
            pltpu.VMEM((n_q, hb, ACC_ROWS, TQ), F32),
            pltpu.VMEM((n_q, hb, 1, TQ), F32),
        ],
        compiler_params=pltpu.CompilerParams(
            dimension_semantics=("arbitrary", "arbitrary"),
            vmem_limit_bytes=VMEM_LIMIT_BYTES),
        name="mla_attention",
    )(qt, k, vt, *f32_weights)
    return out[0], out[1:]


def _mix_kernel(x_ref, ot_ref, mod_ref, g1_ref, g2_ref, gf_ref, w_in_ref, w_pool_ref, pscale_ref,
                p_pool_ref, p_attn_ref, w_out_ref, w_ff1_ref, w_ff2_ref,
                o_ref, uext_ref, *, final, layer):
    tm = x_ref.shape[1]
    si = pl.program_id(1)

    @pl.when(si == 0)
    def _():
        uext_ref[0:POOL_HALO, :] = jnp.zeros((POOL_HALO, POOL_DIM), F32)

    @pl.when(si > 0)
    def _():
        uext_ref[0:POOL_HALO, :] = uext_ref[tm:tm + POOL_HALO, :]

    shift1, scale1, gate1, shift2, scale2, gate2 = _mod_rows(mod_ref, range(N_MOD))

    for r0 in range(0, tm, MIX_SUB):
        tok = slice(r0, r0 + MIX_SUB)
        x = x_ref[0, tok, :]

        y_b = _tn_dot(ot_ref[0, :, tok], p_attn_ref[0])
        h = ((_rms(x) * g1_ref[layer:layer + 1, :]) * (1.0 + scale1) + shift1).astype(BF16)
        u = jnp.dot(h, w_in_ref[0, :, 0:POOL_DIM], preferred_element_type=F32)
        gz = jnp.dot(h, w_in_ref[0, :, POOL_DIM:], preferred_element_type=F32)
        gz_a = gz[:, 0:D_MODEL]
        gz_b = gz[:, D_MODEL:]

        uext_ref[POOL_HALO + r0:POOL_HALO + r0 + MIX_SUB, :] = u
        head_pos = si * tm + r0 + lax.broadcasted_iota(jnp.int32, (POOL_HALO, 1), 0)
        pooled = []
        for g, w in enumerate(POOL_WINDOWS):
            eg = uext_ref[r0:r0 + POOL_HALO + MIX_SUB,
                          g * POOL_GROUP_DIM:(g + 1) * POOL_GROUP_DIM]
            win = eg
            k = 1
            while k < w:
                win = win + pltpu.roll(win, k, axis=0)
                k *= 2
            win = win[POOL_HALO:]
            ug = eg[POOL_HALO:]
            inv_head = 1.0 / jnp.minimum(head_pos + 1, w).astype(F32)
            mean = jnp.concatenate([win[:POOL_HALO] * inv_head, win[POOL_HALO:] * (1.0 / w)], axis=0)
            yg = jnp.dot((mean - ug).astype(BF16), w_pool_ref[0, g], preferred_element_type=F32)
            pooled.append(yg)
        y_pool = jnp.concatenate(pooled, axis=-1) * pscale_ref[layer:layer + 1, :]
        y_a = jnp.dot(y_pool.astype(BF16), p_pool_ref[0], preferred_element_type=F32)

        merged = _sigmoid(gz_a) * y_a + _sigmoid(gz_b) * y_b
        x1 = x + gate1 * jnp.dot(merged.astype(BF16), w_out_ref[0], preferred_element_type=F32)

        h2 = ((_rms(x1) * g2_ref[layer:layer + 1, :]) * (1.0 + scale2) + shift2).astype(BF16)
        hidden = [jnp.square(jnp.maximum(
            jnp.dot(h2, w_ff1_ref[0, :, c0:c0 + FF_CHUNK], preferred_element_type=F32), 0.0)).astype(BF16)
            for c0 in range(0, D_FF, FF_CHUNK)]
        ff = jnp.dot(jnp.concatenate(hidden, axis=-1), w_ff2_ref[0], preferred_element_type=F32)
        x2 = x1 + gate2 * ff
        if final:
            x2 = _rms(x2) * gf_ref[...]
        o_ref[0, tok, :] = x2


def _mix(x, ot, mod, wts, layer_wts, layer, *, final):
    b, s, d = x.shape
    tm = TM_MIX
    once = dict(pipeline_mode=pl.Buffered(1))
    names = ["ln1_g", "ln2_g", "final_g", "w_in_b", "w_pool", "pool_scale", "p_pool", "p_attn", "w_out",
             "w_ff1", "w_ff2"]
    whole = {"ln1_g", "ln2_g", "final_g", "pool_scale"}
    wts = {**wts, **layer_wts}
    specs = [_whole_spec(wts[n], **once) if n in whole
             else _layer_spec(wts[n], 0 if n in layer_wts else layer, **once)
             for n in names]
    return pl.pallas_call(
        functools.partial(_mix_kernel, final=final, layer=layer),
        grid=(b, s // tm),
        in_specs=[
            pl.BlockSpec((1, tm, d), lambda i, j: (i, j, 0)),
            pl.BlockSpec((1, ATTN_DIM, tm), lambda i, j: (i, 0, j)),
            _layer_spec(mod, layer, **once),
        ] + specs,
        out_specs=pl.BlockSpec((1, tm, d), lambda i, j: (i, j, 0)),
        out_shape=jax.ShapeDtypeStruct((b, s, d), F32),
        scratch_shapes=[pltpu.VMEM((tm + POOL_HALO, POOL_DIM), F32)],
        compiler_params=pltpu.CompilerParams(
            dimension_semantics=("arbitrary", "arbitrary"),
            vmem_limit_bytes=VMEM_LIMIT_BYTES),
        name="mix_mlp",
    )(x, ot, mod, *[wts[n] for n in names])


W_IN_PREP_ROWS = 256


def _w_in_prep_kernel(w_ref, a_ref, b_ref):
    wt = w_ref[0]
    rows = wt.shape[1]
    c0 = POOL_DIM
    c2 = c0 + Q_LORA + KV_LORA
    c3 = c2 + QK_ROPE
    half = QK_ROPE // 2
    kr = wt[c2:c3]
    zl = jnp.zeros((ROPE_LO, rows), F32)
    a_t = jnp.concatenate([wt[c0:c2], zl, kr, -kr[half:], kr[:half]], axis=0)
    a_ref[0] = a_t.T.astype(BF16)
    b_ref[0] = jnp.concatenate([wt[:c0], wt[c3:]], axis=0).T.astype(BF16)


def _w_in_prep(w_in):
    depth, d, n = w_in.shape
    na = Q_LORA + KV_LORA + HEAD_PAD
    nb = n - (Q_LORA + KV_LORA + QK_ROPE)
    rows = W_IN_PREP_ROWS
    w_in = jnp.swapaxes(w_in, 1, 2)
    return pl.pallas_call(
        _w_in_prep_kernel,
        grid=(depth, d // rows),
        in_specs=[pl.BlockSpec((1, n, rows), lambda l, r: (l, 0, r))],
        out_specs=[pl.BlockSpec((1, rows, na), lambda l, r: (l, r, 0)),
                   pl.BlockSpec((1, rows, nb), lambda l, r: (l, r, 0))],
        out_shape=[jax.ShapeDtypeStruct((depth, d, na), BF16), jax.ShapeDtypeStruct((depth, d, nb), BF16)],
        compiler_params=pltpu.CompilerParams(vmem_limit_bytes=VMEM_LIMIT_BYTES),
        name="w_in_prep",
    )(w_in)


def _prep_weights(ln1_g, ln2_g, w_in, q_norm_g, w_uq, kv_norm_g, w_uk, w_uv, w_pool, pool_scale, final_g):
    depth = w_in.shape[0]
    w_in_a, w_in_b = _w_in_prep(w_in)

    pad_q = HEAD_PAD - (QK_NOPE + QK_ROPE)
    wuq_pad = jnp.pad(w_uq, ((0, 0), (0, 0), (0, 0), (0, pad_q)))
    wuqt = wuq_pad.reshape(depth, Q_LORA, N_HEADS * HEAD_PAD).transpose(0, 2, 1).astype(BF16)
    wuk = w_uk.reshape(depth, KV_LORA, N_HEADS * QK_NOPE).astype(BF16)
    wuvt = w_uv.reshape(depth, KV_LORA, ATTN_DIM).transpose(0, 2, 1).astype(BF16)
    return dict(
        ln1_g=ln1_g, ln2_g=ln2_g, final_g=final_g.reshape(1, D_MODEL), q_norm_g=q_norm_g,
        kv_norm_g=kv_norm_g, pool_scale=pool_scale, w_in_a=w_in_a, w_in_b=w_in_b, wuqt=wuqt,
        wuk=wuk, wuvt=wuvt, w_pool=w_pool.astype(BF16))


def kernel(x, c, positions, ln1_g, ln2_g, w_ada, b_ada, w_in, q_norm_g, w_uq, kv_norm_g, w_uk,
           w_uv, w_pool, pool_scale, p_pool, p_attn, w_out, w_ff1, w_ff2, final_g):
    depth = w_in.shape[0]
    mod = _modulation(c, w_ada, b_ada)
    wts = _prep_weights(ln1_g, ln2_g, w_in, q_norm_g, w_uq, kv_norm_g, w_uk, w_uv, w_pool, pool_scale,
                        final_g)
    cast_names = ("p_pool", "p_attn", "w_out", "w_ff1", "w_ff2")
    f32_weights = (p_pool, p_attn, w_out, w_ff1, w_ff2)
    for layer in range(depth):
        qt, k, vt = _qkv(x, mod, positions, wts, layer)
        ot, cast = _attention(qt, k, vt, f32_weights, layer)
        x = _mix(x, ot, mod, wts, dict(zip(cast_names, cast)), layer, final=(layer == depth - 1))
    return x
```

```python
import functools
import math

import jax
import jax.numpy as jnp
from jax import lax
from jax.experimental import pallas as pl
from jax.experimental.pallas import tpu as pltpu

D_MODEL = 1024
N_HEADS = 8
QK_NOPE = 64
QK_ROPE = 32
V_DIM = 64
Q_LORA = 384
KV_LORA = 256
POOL_WINDOWS = (2, 4, 8, 16)
POOL_GROUP_DIM = 128
POOL_DIM = len(POOL_WINDOWS) * POOL_GROUP_DIM
ATTN_DIM = N_HEADS * V_DIM
D_FF = 4 * D_MODEL
N_MOD = 6
EPS = 1e-6
ROPE_THETA = 10000.0

HEAD_PAD = 128
ROPE_LO = QK_NOPE
ROPE_HI = QK_NOPE + QK_ROPE
POOL_HALO = 16

VMEM_LIMIT_BYTES = 56 * 1024 * 1024

F32 = jnp.float32
BF16 = jnp.bfloat16

TM_QKV = 1024
QKV_SUB = 512
TM_MIX = 1024
MIX_SUB = 512
TQ = 512
TK = 512
ATTN_HEADS = 4
BELOW_PAIRS_PER_TRIP = 4
F32_SUBLANES = 8
BF16_SUBLANES = 16
ACC_ROWS = V_DIM + BF16_SUBLANES
FF_CHUNK = 1024
MOD_TN = 1536


def _nt_dot(a, b):
    return lax.dot_general(a, b, (((1,), (1,)), ((), ())), preferred_element_type=F32)


def _tn_dot(a, b):
    return lax.dot_general(a, b, (((0,), (0,)), ((), ())), preferred_element_type=F32)


def _rms(x):
    return x * lax.rsqrt(jnp.mean(x * x, axis=-1, keepdims=True) + EPS)


def _sigmoid(x):
    return 1.0 / (1.0 + jnp.exp(-x))


def _mod_rows(mod_ref, chunks):
    row = pl.ds(pl.program_id(0), 1)
    return tuple(mod_ref[0, row, c * D_MODEL:(c + 1) * D_MODEL] for c in chunks)


def _layer_spec(arr, layer, **kw):
    tail = (0,) * (arr.ndim - 1)
    return pl.BlockSpec((1,) + arr.shape[1:], lambda i, j: (layer,) + tail, **kw)


def _whole_spec(arr, **kw):
    zeros = (0,) * arr.ndim
    return pl.BlockSpec(arr.shape, lambda i, j: zeros, **kw)


def _mod_kernel(c_ref, w_ref, b_ref, o_ref):
    c = c_ref[...]
    c_act = c * _sigmoid(c)
    o_ref[0] = jnp.dot(c_act.astype(BF16), w_ref[0].astype(BF16),
                       preferred_element_type=F32) + b_ref[0]


def _modulation(c, w_ada, b_ada):
    depth, d, n = w_ada.shape
    b = c.shape[0]
    rows = F32_SUBLANES
    assert b <= rows
    c_pad = jnp.pad(c, ((0, rows - b), (0, 0)))
    out = pl.pallas_call(
        _mod_kernel,
        grid=(depth, n // MOD_TN),
        in_specs=[
            pl.BlockSpec((rows, d), lambda l, j: (0, 0)),
            pl.BlockSpec((1, d, MOD_TN), lambda l, j: (l, 0, j)),
            pl.BlockSpec((1, 1, MOD_TN), lambda l, j: (l, 0, j)),
        ],
        out_specs=pl.BlockSpec((1, rows, MOD_TN), lambda l, j: (l, 0, j)),
        out_shape=jax.ShapeDtypeStruct((depth, rows, n), F32),
        compiler_params=pltpu.CompilerParams(vmem_limit_bytes=VMEM_LIMIT_BYTES),
        name="adaln_mod",
    )(c_pad, w_ada, b_ada.reshape(depth, 1, n))
    return out


def _qkv_kernel(x_ref, mod_ref, g_ref, w_in_ref, gq_ref, gkv_ref, wuqt_ref,
                wuk_ref, wuvt_ref, pos_ref, invf_ref,
                qt_ref, k_ref, vt_ref, *, scale, layer):
    shift, scl = _mod_rows(mod_ref, (0, 1))
    toks = [slice(r0, r0 + QKV_SUB) for r0 in range(0, x_ref.shape[1], QKV_SUB)]

    hs = [((_rms(x_ref[0, tok, :]) * g_ref[layer:layer + 1, :]) * (1.0 + scl) + shift).astype(BF16)
          for tok in toks]
    zs = [jnp.dot(h, w_in_ref[0], preferred_element_type=F32) for h in hs]

    latents = []
    for z in zs:
        c_q = z[:, 0:Q_LORA]
        c_kv = z[:, Q_LORA:Q_LORA + KV_LORA]
        latents.append(((_rms(c_q) * gq_ref[layer:layer + 1, :]).astype(BF16),
                        (_rms(c_kv) * gkv_ref[layer:layer + 1, :]).astype(BF16)))

    prods = [(_nt_dot(wuqt_ref[0], cqn),
              jnp.dot(ckvn, wuk_ref[0], preferred_element_type=F32), _nt_dot(wuvt_ref[0], ckvn))
             for cqn, ckvn in latents]

    half = QK_ROPE // 2
    z_lo = jnp.zeros((ROPE_LO, QKV_SUB), F32)
    z_hi = jnp.zeros((HEAD_PAD - ROPE_HI, QKV_SUB), F32)
    for tok, z, (qt, k, vt) in zip(toks, zs, prods):
        ang = invf_ref[...] * pos_ref[0, :, tok].astype(F32)
        cost = jnp.cos(ang)
        sint = jnp.sin(ang)
        for hd in range(N_HEADS):
            src = hd * ROPE_HI
            base = hd * HEAD_PAD
            qt_ref[0, base:base + ROPE_LO, tok] = (qt[src:src + ROPE_LO] * scale).astype(BF16)
            x1 = qt[src + ROPE_LO:src + ROPE_LO + half]
            x2 = qt[src + ROPE_LO + half:src + ROPE_HI]
            roped = jnp.concatenate([x1 * cost[:half] - x2 * sint[:half],
                                     x2 * cost[half:] + x1 * sint[half:]], axis=0)
            qt_ref[0, base + ROPE_LO:base + ROPE_HI, tok] = (roped * scale).astype(BF16)
            qt_ref[0, base + ROPE_HI:base + HEAD_PAD, tok] = jnp.zeros((HEAD_PAD - ROPE_HI, QKV_SUB), BF16)

        krx = z[:, Q_LORA + KV_LORA:]
        kr_rot = pltpu.roll(krx, HEAD_PAD - QK_ROPE, axis=1)
        cos_tok = jnp.concatenate([z_lo, cost, z_hi], axis=0).T
        sin_tok = jnp.concatenate([z_lo, sint, z_hi], axis=0).T
        kr_full = krx * cos_tok + kr_rot * sin_tok
        nope_lane = lax.broadcasted_iota(jnp.int32, (QKV_SUB, HEAD_PAD), 1) < QK_NOPE
        for pair in range(N_HEADS // 2):
            two = k[:, pair * HEAD_PAD:(pair + 1) * HEAD_PAD]
            for odd, src in ((0, two), (1, pltpu.roll(two, QK_NOPE, axis=1))):
                base = (2 * pair + odd) * HEAD_PAD
                k_ref[0, tok, base:base + HEAD_PAD] = jnp.where(nope_lane, src, kr_full).astype(BF16)

        vt_ref[0, :, tok] = vt.astype(BF16)


def _qkv(x, mod, positions, wts, layer):
    b, s, d = x.shape
    tm = TM_QKV
    inv_freq = ROPE_THETA ** (-jnp.arange(0, QK_ROPE, 2, dtype=F32) / QK_ROPE)
    invf_col = jnp.concatenate([inv_freq, inv_freq]).reshape(QK_ROPE, 1)
    scale = math.log2(math.e) / math.sqrt(QK_NOPE + QK_ROPE)
    stacked = [wts["ln1_g"], wts["w_in_a"], wts["q_norm_g"], wts["kv_norm_g"], wts["wuqt"],
               wts["wuk"], wts["wuvt"]]
    return pl.pallas_call(
        functools.partial(_qkv_kernel, scale=scale, layer=layer),
        grid=(b, s // tm),
        in_specs=[
            pl.BlockSpec((1, tm, d), lambda i, j: (i, j, 0)),
            _layer_spec(mod, layer),
            _whole_spec(wts["ln1_g"]),
            _layer_spec(wts["w_in_a"], layer),
            _whole_spec(wts["q_norm_g"]),
            _whole_spec(wts["kv_norm_g"]),
            _layer_spec(wts["wuqt"], layer),
            _layer_spec(wts["wuk"], layer),
            _layer_spec(wts["wuvt"], layer),
            pl.BlockSpec((1, 1, tm), lambda i, j: (i, 0, j)),
            _whole_spec(invf_col),
        ],
        out_specs=[
            pl.BlockSpec((1, N_HEADS * HEAD_PAD, tm), lambda i, j: (i, 0, j)),
            pl.BlockSpec((1, tm, N_HEADS * HEAD_PAD), lambda i, j: (i, j, 0)),
            pl.BlockSpec((1, ATTN_DIM, tm), lambda i, j: (i, 0, j)),
        ],
        out_shape=[
            jax.ShapeDtypeStruct((b, N_HEADS * HEAD_PAD, s), BF16),
            jax.ShapeDtypeStruct((b, s, N_HEADS * HEAD_PAD), BF16),
            jax.ShapeDtypeStruct((b, ATTN_DIM, s), BF16),
        ],
        compiler_params=pltpu.CompilerParams(
            dimension_semantics=("arbitrary", "arbitrary"),
            vmem_limit_bytes=VMEM_LIMIT_BYTES),
        name="qkv_proj",
    )(x, mod, *stacked, positions.reshape(b, 1, s), invf_col)


def _attn_kernel(qt_ref, k_ref, vt_ref, *refs, n_q, n_cast):
    cast_in, o_ref, cast_out = refs[:n_cast], refs[n_cast], refs[n_cast + 1:2 * n_cast + 1]
    s_ref, acc_ref, m_ref = refs[2 * n_cast + 1:]

    n_below = n_q * (n_q - 1) // 2
    hq = TQ // 2
    ones_rows = jnp.ones((ACC_ROWS - V_DIM, TK), BF16)
    half_mask = (lax.broadcasted_iota(jnp.int32, (hq, hq), 1)
                 >= lax.broadcasted_iota(jnp.int32, (hq, hq), 0))

    def keys(hd, off, n):
        return k_ref[0, pl.ds(off, n), hd * HEAD_PAD:(hd + 1) * HEAD_PAD]

    def queries(hd, off, n):
        return qt_ref[0, hd * HEAD_PAD:(hd + 1) * HEAD_PAD, pl.ds(off, n)]

    def values(hd, off, n):
        return jnp.concatenate(
            [vt_ref[0, hd * V_DIM:(hd + 1) * V_DIM, pl.ds(off, n)], ones_rows[:, :n]], axis=0)

    def col_max(s):
        return jnp.max(s, axis=0, keepdims=True)

    def produce_below(slot, hd, qi, j):
        s = jnp.dot(keys(hd, pl.multiple_of(j * TK, TK), TK), queries(hd, pl.multiple_of(qi * TQ, TQ), TQ),
                    preferred_element_type=F32)
        s_ref[slot, hd] = s
        return col_max(s)

    def consume_below(slot, hd, qi, j, tile_max):
        m = m_ref[qi, hd]
        m_new = jnp.maximum(m, tile_max)
        p = jnp.exp2(s_ref[slot, hd] - m_new).astype(BF16)
        acc_ref[qi, hd] = (jnp.exp2(m - m_new) * acc_ref[qi, hd]
                           + jnp.dot(values(hd, pl.multiple_of(j * TK, TK), TK), p, preferred_element_type=F32))
        m_ref[qi, hd] = m_new

    def produce_diag(slot, hd, d):
        off = d * TQ
        s_ref[slot, hd, 0:hq, :] = jnp.dot(keys(hd, off, hq), queries(hd, off, TQ),
                                           preferred_element_type=F32)
        s_ref[slot, hd, hq:TK, hq:TQ] = jnp.dot(keys(hd, off + hq, hq), queries(hd, off + hq, hq),
                                                preferred_element_type=F32)

    def consume_diag(slot, hd, d):
        off = d * TQ
        neg = jnp.finfo(F32).min
        s_tl = jnp.where(half_mask, s_ref[slot, hd, 0:hq, 0:hq], neg)
        s_tr = s_ref[slot, hd, 0:hq, hq:TQ]
        s_br = jnp.where(half_mask, s_ref[slot, hd, hq:TK, hq:TQ], neg)
        m = m_ref[d, hd]
        m_l = jnp.maximum(m[:, 0:hq], col_max(s_tl))
        m_r = jnp.maximum(m[:, hq:TQ], jnp.maximum(col_max(s_tr), col_max(s_br)))
        p_l = jnp.exp2(s_tl - m_l).astype(BF16)
        p_r = jnp.concatenate([jnp.exp2(s_tr - m_r), jnp.exp2(s_br - m_r)], axis=0).astype(BF16)
        acc = acc_ref[d, hd]
        acc_l = (jnp.exp2(m[:, 0:hq] - m_l) * acc[:, 0:hq]
                 + jnp.dot(values(hd, off, hq), p_l, preferred_element_type=F32))
        acc_r = (jnp.exp2(m[:, hq:TQ] - m_r) * acc[:, hq:TQ]
                 + jnp.dot(values(hd, off, TK), p_r, preferred_element_type=F32))
        rows = slice(hd * V_DIM, (hd + 1) * V_DIM)
        o_ref[0, rows, pl.ds(off, hq)] = (
            acc_l[0:V_DIM] * (1.0 / acc_l[V_DIM:V_DIM + 1])).astype(o_ref.dtype)
        o_ref[0, rows, pl.ds(off + hq, hq)] = (
            acc_r[0:V_DIM] * (1.0 / acc_r[V_DIM:V_DIM + 1])).astype(o_ref.dtype)

    heads = range(ATTN_HEADS)

    def next_below(qi, j):
        row_end = j + 1 == qi
        return jnp.where(row_end, qi + 1, qi), jnp.where(row_end, 0, j + 1)

    def below_step(slot, tile, maxes, to_diag=False):
        nxt = next_below(*tile)
        new_maxes = []
        for hd in heads:
            if to_diag:
                produce_diag(1 - slot, hd, 0)
            else:
                new_maxes.append(produce_below(1 - slot, hd, *nxt))
            consume_below(slot, hd, *tile, maxes[hd])
        return nxt, tuple(new_maxes)

    def diag_step(slot, d, produce_next=True):
        for hd in heads:
            if produce_next:
                produce_diag(1 - slot, hd, d + 1)
            consume_diag(slot, hd, d)
        return d + 1

    tile0 = (jnp.int32(1), jnp.int32(0))
    maxes0 = tuple(produce_below(0, hd, *tile0) for hd in heads)

    acc_ref[...] = jnp.zeros(acc_ref.shape, F32)
    m_ref[...] = jnp.full(m_ref.shape, jnp.finfo(F32).min, F32)
    for src, dst in zip(cast_in, cast_out):
        dst[...] = src[...].astype(BF16)

    def below_pair(carry):
        tile, maxes = carry
        tile, maxes = below_step(0, tile, maxes)
        return below_step(1, tile, maxes)

    def below_trip(_, carry):
        for _ in range(BELOW_PAIRS_PER_TRIP):
            carry = below_pair(carry)
        return carry

    n_pairs = n_below // 2 - 1
    carry = lax.fori_loop(0, n_pairs // BELOW_PAIRS_PER_TRIP, below_trip, (tile0, maxes0))
    for _ in range(n_pairs % BELOW_PAIRS_PER_TRIP):
        carry = below_pair(carry)
    tile, maxes = below_step(0, *carry)
    below_step(1, tile, maxes, to_diag=True)

    for d in range(n_q):
        diag_step(d % 2, d, produce_next=d + 1 < n_q)


def _attention(qt, k, vt, f32_weights, layer):
    b, _, s = qt.shape
    hb = ATTN_HEADS
    groups = N_HEADS // hb
    n_steps = b * groups
    n_q = s // TQ
    assert n_q % 2 == 0 and (n_q * (n_q - 1) // 2) % 2 == 0, "both tile streams run two tiles per loop trip"

    def rows_per_step(w):
        rows = w.shape[1] // n_steps
        assert rows * n_steps == w.shape[1] and rows % BF16_SUBLANES == 0, w.shape
        return rows

    cast_in = [pl.BlockSpec((1, rows_per_step(w), w.shape[2]), lambda i, h: (layer, i * groups + h, 0))
               for w in f32_weights]
    cast_out = [pl.BlockSpec((1, rows_per_step(w), w.shape[2]), lambda i, h: (0, i * groups + h, 0))
                for w in f32_weights]
    out = pl.pallas_call(
        functools.partial(_attn_kernel, n_q=n_q, n_cast=len(f32_weights)),
        grid=(b, groups),
        in_specs=[
            pl.BlockSpec((1, hb * HEAD_PAD, s), lambda i, h: (i, h, 0)),
            pl.BlockSpec((1, s, hb * HEAD_PAD), lambda i, h: (i, 0, h)),
            pl.BlockSpec((1, hb * V_DIM, s), lambda i, h: (i, h, 0)),
        ] + cast_in,
        out_specs=[pl.BlockSpec((1, hb * V_DIM, s), lambda i, h: (i, h, 0))] + cast_out,
        out_shape=[jax.ShapeDtypeStruct((b, ATTN_DIM, s), BF16)]
        + [jax.ShapeDtypeStruct((1,) + w.shape[1:], BF16) for w in f32_weights],
        scratch_shapes=[
            pltpu.VMEM((2, hb, TK, TQ), F32),
            pltpu.VMEM((n_q, hb, ACC_ROWS, TQ), F32),
            pltpu.VMEM((n_q, hb, 1, TQ), F32),
        ],
        compiler_params=pltpu.CompilerParams(
            dimension_semantics=("arbitrary", "arbitrary"),
            vmem_limit_bytes=VMEM_LIMIT_BYTES),
        name="mla_attention",
    )(qt, k, vt, *f32_weights)
    return out[0], out[1:]


def _mix_kernel(x_ref, ot_ref, mod_ref, g1_ref, g2_ref, gf_ref, w_in_ref, w_pool_ref, pscale_ref,
                p_pool_ref, p_attn_ref, w_out_ref, w_ff1_ref, w_ff2_ref,
                o_ref, uext_ref, *, final, layer):
    tm = x_ref.shape[1]
    si = pl.program_id(1)

    @pl.when(si == 0)
    def _():
        uext_ref[0:POOL_HALO, :] = jnp.zeros((POOL_HALO, POOL_DIM), F32)

    @pl.when(si > 0)
    def _():
        uext_ref[0:POOL_HALO, :] = uext_ref[tm:tm + POOL_HALO, :]

    shift1, scale1, gate1, shift2, scale2, gate2 = _mod_rows(mod_ref, range(N_MOD))

    for r0 in range(0, tm, MIX_SUB):
        tok = slice(r0, r0 + MIX_SUB)
        x = x_ref[0, tok, :]

        y_b = _tn_dot(ot_ref[0, :, tok], p_attn_ref[0])
        h = ((_rms(x) * g1_ref[layer:layer + 1, :]) * (1.0 + scale1) + shift1).astype(BF16)
        u = jnp.dot(h, w_in_ref[0, :, 0:POOL_DIM], preferred_element_type=F32)
        gz = jnp.dot(h, w_in_ref[0, :, POOL_DIM:], preferred_element_type=F32)
        gz_a = gz[:, 0:D_MODEL]
        gz_b = gz[:, D_MODEL:]

        uext_ref[POOL_HALO + r0:POOL_HALO + r0 + MIX_SUB, :] = u
        head_pos = si * tm + r0 + lax.broadcasted_iota(jnp.int32, (POOL_HALO, 1), 0)
        pooled = []
        for g, w in enumerate(POOL_WINDOWS):
            eg = uext_ref[r0:r0 + POOL_HALO + MIX_SUB,
                          g * POOL_GROUP_DIM:(g + 1) * POOL_GROUP_DIM]
            win = eg
            k = 1
            while k < w:
                win = win + pltpu.roll(win, k, axis=0)
                k *= 2
            win = win[POOL_HALO:]
            ug = eg[POOL_HALO:]
            inv_head = 1.0 / jnp.minimum(head_pos + 1, w).astype(F32)
            mean = jnp.concatenate([win[:POOL_HALO] * inv_head, win[POOL_HALO:] * (1.0 / w)], axis=0)
            yg = jnp.dot((mean - ug).astype(BF16), w_pool_ref[0, g], preferred_element_type=F32)
            pooled.append(yg)
        y_pool = jnp.concatenate(pooled, axis=-1) * pscale_ref[layer:layer + 1, :]
        y_a = jnp.dot(y_pool.astype(BF16), p_pool_ref[0], preferred_element_type=F32)

        merged = _sigmoid(gz_a) * y_a + _sigmoid(gz_b) * y_b
        x1 = x + gate1 * jnp.dot(merged.astype(BF16), w_out_ref[0], preferred_element_type=F32)

        h2 = ((_rms(x1) * g2_ref[layer:layer + 1, :]) * (1.0 + scale2) + shift2).astype(BF16)
        hidden = [jnp.square(jnp.maximum(
            jnp.dot(h2, w_ff1_ref[0, :, c0:c0 + FF_CHUNK], preferred_element_type=F32), 0.0)).astype(BF16)
            for c0 in range(0, D_FF, FF_CHUNK)]
        ff = jnp.dot(jnp.concatenate(hidden, axis=-1), w_ff2_ref[0], preferred_element_type=F32)
        x2 = x1 + gate2 * ff
        if final:
            x2 = _rms(x2) * gf_ref[...]
        o_ref[0, tok, :] = x2


def _mix(x, ot, mod, wts, layer_wts, layer, *, final):
    b, s, d = x.shape
    tm = TM_MIX
    once = dict(pipeline_mode=pl.Buffered(1))
    names = ["ln1_g", "ln2_g", "final_g", "w_in_b", "w_pool", "pool_scale", "p_pool", "p_attn", "w_out",
             "w_ff1", "w_ff2"]
    whole = {"ln1_g", "ln2_g", "final_g", "pool_scale"}
    wts = {**wts, **layer_wts}
    specs = [_whole_spec(wts[n], **once) if n in whole
             else _layer_spec(wts[n], 0 if n in layer_wts else layer, **once)
             for n in names]
    return pl.pallas_call(
        functools.partial(_mix_kernel, final=final, layer=layer),
        grid=(b, s // tm),
        in_specs=[
            pl.BlockSpec((1, tm, d), lambda i, j: (i, j, 0)),
            pl.BlockSpec((1, ATTN_DIM, tm), lambda i, j: (i, 0, j)),
            _layer_spec(mod, layer, **once),
        ] + specs,
        out_specs=pl.BlockSpec((1, tm, d), lambda i, j: (i, j, 0)),
        out_shape=jax.ShapeDtypeStruct((b, s, d), F32),
        scratch_shapes=[pltpu.VMEM((tm + POOL_HALO, POOL_DIM), F32)],
        compiler_params=pltpu.CompilerParams(
            dimension_semantics=("arbitrary", "arbitrary"),
            vmem_limit_bytes=VMEM_LIMIT_BYTES),
        name="mix_mlp",
    )(x, ot, mod, *[wts[n] for n in names])


W_IN_PREP_ROWS = 256


def _w_in_prep_kernel(w_ref, a_ref, b_ref):
    wt = w_ref[0]
    rows = wt.shape[1]
    c0 = POOL_DIM
    c2 = c0 + Q_LORA + KV_LORA
    c3 = c2 + QK_ROPE
    half = QK_ROPE // 2
    kr = wt[c2:c3]
    zl = jnp.zeros((ROPE_LO, rows), F32)
    a_t = jnp.concatenate([wt[c0:c2], zl, kr, -kr[half:], kr[:half]], axis=0)
    a_ref[0] = a_t.T.astype(BF16)
    b_ref[0] = jnp.concatenate([wt[:c0], wt[c3:]], axis=0).T.astype(BF16)


def _w_in_prep(w_in):
    depth, d, n = w_in.shape
    na = Q_LORA + KV_LORA + HEAD_PAD
    nb = n - (Q_LORA + KV_LORA + QK_ROPE)
    rows = W_IN_PREP_ROWS
    w_in = jnp.swapaxes(w_in, 1, 2)
    return pl.pallas_call(
        _w_in_prep_kernel,
        grid=(depth, d // rows),
        in_specs=[pl.BlockSpec((1, n, rows), lambda l, r: (l, 0, r))],
        out_specs=[pl.BlockSpec((1, rows, na), lambda l, r: (l, r, 0)),
                   pl.BlockSpec((1, rows, nb), lambda l, r: (l, r, 0))],
        out_shape=[jax.ShapeDtypeStruct((depth, d, na), BF16), jax.ShapeDtypeStruct((depth, d, nb), BF16)],
        compiler_params=pltpu.CompilerParams(vmem_limit_bytes=VMEM_LIMIT_BYTES),
        name="w_in_prep",
    )(w_in)


def _prep_weights(ln1_g, ln2_g, w_in, q_norm_g, w_uq, kv_norm_g, w_uk, w_uv, w_pool, pool_scale, final_g):
    depth = w_in.shape[0]
    w_in_a, w_in_b = _w_in_prep(w_in)

    wuqt = w_uq.reshape(depth, Q_LORA, N_HEADS * ROPE_HI).transpose(0, 2, 1).astype(BF16)
    wuk = w_uk.reshape(depth, KV_LORA, N_HEADS * QK_NOPE).astype(BF16)
    wuvt = w_uv.reshape(depth, KV_LORA, ATTN_DIM).transpose(0, 2, 1).astype(BF16)
    return dict(
        ln1_g=ln1_g, ln2_g=ln2_g, final_g=final_g.reshape(1, D_MODEL), q_norm_g=q_norm_g,
        kv_norm_g=kv_norm_g, pool_scale=pool_scale, w_in_a=w_in_a, w_in_b=w_in_b, wuqt=wuqt,
        wuk=wuk, wuvt=wuvt, w_pool=w_pool.astype(BF16))


def kernel(x, c, positions, ln1_g, ln2_g, w_ada, b_ada, w_in, q_norm_g, w_uq, kv_norm_g, w_uk,
           w_uv, w_pool, pool_scale, p_pool, p_attn, w_out, w_ff1, w_ff2, final_g):
    depth = w_in.shape[0]
    mod = _modulation(c, w_ada, b_ada)
    wts = _prep_weights(ln1_g, ln2_g, w_in, q_norm_g, w_uq, kv_norm_g, w_uk, w_uv, w_pool, pool_scale,
                        final_g)
    cast_names = ("p_pool", "p_attn", "w_out", "w_ff1", "w_ff2")
    f32_weights = (p_pool, p_attn, w_out, w_ff1, w_ff2)
    for layer in range(depth):
        qt, k, vt = _qkv(x, mod, positions, wts, layer)
        ot, cast = _attention(qt, k, vt, f32_weights, layer)
        x = _mix(x, ot, mod, wts, dict(zip(cast_names, cast)), layer, final=(layer == depth - 1))
    return x
```

```python
import functools
import math

import jax
import jax.numpy as jnp
from jax import lax
from jax.experimental import pallas as pl
from jax.experimental.pallas import tpu as pltpu

D_MODEL = 1024
N_HEADS = 8
QK_NOPE = 64
QK_ROPE = 32
V_DIM = 64
Q_LORA = 384
KV_LORA = 256
POOL_WINDOWS = (2, 4, 8, 16)
POOL_GROUP_DIM = 128
POOL_DIM = len(POOL_WINDOWS) * POOL_GROUP_DIM
ATTN_DIM = N_HEADS * V_DIM
D_FF = 4 * D_MODEL
N_MOD = 6
EPS = 1e-6
ROPE_THETA = 10000.0

HEAD_PAD = 128
ROPE_LO = QK_NOPE
ROPE_HI = QK_NOPE + QK_ROPE
POOL_HALO = 16

VMEM_LIMIT_BYTES = 56 * 1024 * 1024

F32 = jnp.float32
BF16 = jnp.bfloat16

TM_QKV = 1024
QKV_SUB = 512
TM_MIX = 1024
MIX_SUB = 512
TQ = 512
TK = 512
ATTN_HEADS = 4
BELOW_PAIRS_PER_TRIP = 4
F32_SUBLANES = 8
BF16_SUBLANES = 16
ACC_ROWS = V_DIM + BF16_SUBLANES
FF_CHUNK = 1024
MOD_TN = 3072


def _nt_dot(a, b):
    return lax.dot_general(a, b, (((1,), (1,)), ((), ())), preferred_element_type=F32)


def _tn_dot(a, b):
    return lax.dot_general(a, b, (((0,), (0,)), ((), ())), preferred_element_type=F32)


def _rms(x):
    return x * lax.rsqrt(jnp.mean(x * x, axis=-1, keepdims=True) + EPS)


def _sigmoid(x):
    return 1.0 / (1.0 + jnp.exp(-x))


def _mod_rows(mod_ref, chunks):
    row = pl.ds(pl.program_id(0), 1)
    return tuple(mod_ref[0, row, c * D_MODEL:(c + 1) * D_MODEL] for c in chunks)


def _layer_spec(arr, layer, **kw):
    tail = (0,) * (arr.ndim - 1)
    return pl.BlockSpec((1,) + arr.shape[1:], lambda i, j: (layer,) + tail, **kw)


def _whole_spec(arr, **kw):
    zeros = (0,) * arr.ndim
    return pl.BlockSpec(arr.shape, lambda i, j: zeros, **kw)


def _mod_kernel(c_ref, w_ref, b_ref, o_ref):
    c = c_ref[...]
    c_act = c * _sigmoid(c)
    o_ref[0] = jnp.dot(c_act.astype(BF16), w_ref[0].astype(BF16),
                       preferred_element_type=F32) + b_ref[0]


def _modulation(c, w_ada, b_ada):
    depth, d, n = w_ada.shape
    b = c.shape[0]
    rows = F32_SUBLANES
    assert b <= rows
    c_pad = jnp.pad(c, ((0, rows - b), (0, 0)))
    out = pl.pallas_call(
        _mod_kernel,
        grid=(depth, n // MOD_TN),
        in_specs=[
            pl.BlockSpec((rows, d), lambda l, j: (0, 0)),
            pl.BlockSpec((1, d, MOD_TN), lambda l, j: (l, 0, j)),
            pl.BlockSpec((1, 1, MOD_TN), lambda l, j: (l, 0, j)),
        ],
        out_specs=pl.BlockSpec((1, rows, MOD_TN), lambda l, j: (l, 0, j)),
        out_shape=jax.ShapeDtypeStruct((depth, rows, n), F32),
        compiler_params=pltpu.CompilerParams(vmem_limit_bytes=VMEM_LIMIT_BYTES),
        name="adaln_mod",
    )(c_pad, w_ada, b_ada.reshape(depth, 1, n))
    return out


def _qkv_kernel(x_ref, mod_ref, g_ref, w_in_ref, gq_ref, gkv_ref, wuqt_ref,
                wuk_ref, wuvt_ref, pos_ref, invf_ref,
                qt_ref, k_ref, vt_ref, *, scale, layer):
    shift, scl = _mod_rows(mod_ref, (0, 1))
    toks = [slice(r0, r0 + QKV_SUB) for r0 in range(0, x_ref.shape[1], QKV_SUB)]

    hs = [((_rms(x_ref[0, tok, :]) * g_ref[layer:layer + 1, :]) * (1.0 + scl) + shift).astype(BF16)
          for tok in toks]
    zs = [jnp.dot(h, w_in_ref[0], preferred_element_type=F32) for h in hs]

    latents = []
    for z in zs:
        c_q = z[:, 0:Q_LORA]
        c_kv = z[:, Q_LORA:Q_LORA + KV_LORA]
        latents.append(((_rms(c_q) * gq_ref[layer:layer + 1, :]).astype(BF16),
                        (_rms(c_kv) * gkv_ref[layer:layer + 1, :]).astype(BF16)))

    prods = [(_nt_dot(wuqt_ref[0], cqn),
              jnp.dot(ckvn, wuk_ref[0], preferred_element_type=F32), _nt_dot(wuvt_ref[0], ckvn))
             for cqn, ckvn in latents]

    half = QK_ROPE // 2
    z_lo = jnp.zeros((ROPE_LO, QKV_SUB), F32)
    z_hi = jnp.zeros((HEAD_PAD - ROPE_HI, QKV_SUB), F32)
    for tok, z, (qt, k, vt) in zip(toks, zs, prods):
        ang = invf_ref[...] * pos_ref[0, :, tok].astype(F32)
        cost = jnp.cos(ang)
        sint = jnp.sin(ang)
        for hd in range(N_HEADS):
            src = hd * ROPE_HI
            base = hd * HEAD_PAD
            qt_ref[0, base:base + ROPE_LO, tok] = (qt[src:src + ROPE_LO] * scale).astype(BF16)
            x1 = qt[src + ROPE_LO:src + ROPE_LO + half]
            x2 = qt[src + ROPE_LO + half:src + ROPE_HI]
            roped = jnp.concatenate([x1 * cost[:half] - x2 * sint[:half],
                                     x2 * cost[half:] + x1 * sint[half:]], axis=0)
            qt_ref[0, base + ROPE_LO:base + ROPE_HI, tok] = (roped * scale).astype(BF16)
            qt_ref[0, base + ROPE_HI:base + HEAD_PAD, tok] = jnp.zeros((HEAD_PAD - ROPE_HI, QKV_SUB), BF16)

        krx = z[:, Q_LORA + KV_LORA:]
        kr_rot = pltpu.roll(krx, HEAD_PAD - QK_ROPE, axis=1)
        cos_tok = jnp.concatenate([z_lo, cost, z_hi], axis=0).T
        sin_tok = jnp.concatenate([z_lo, sint, z_hi], axis=0).T
        kr_full = krx * cos_tok + kr_rot * sin_tok
        nope_lane = lax.broadcasted_iota(jnp.int32, (QKV_SUB, HEAD_PAD), 1) < QK_NOPE
        for pair in range(N_HEADS // 2):
            two = k[:, pair * HEAD_PAD:(pair + 1) * HEAD_PAD]
            for odd, src in ((0, two), (1, pltpu.roll(two, QK_NOPE, axis=1))):
                base = (2 * pair + odd) * HEAD_PAD
                k_ref[0, tok, base:base + HEAD_PAD] = jnp.where(nope_lane, src, kr_full).astype(BF16)

        vt_ref[0, :, tok] = vt.astype(BF16)


def _qkv(x, mod, positions, wts, layer):
    b, s, d = x.shape
    tm = TM_QKV
    inv_freq = ROPE_THETA ** (-jnp.arange(0, QK_ROPE, 2, dtype=F32) / QK_ROPE)
    invf_col = jnp.concatenate([inv_freq, inv_freq]).reshape(QK_ROPE, 1)
    scale = math.log2(math.e) / math.sqrt(QK_NOPE + QK_ROPE)
    stacked = [wts["ln1_g"], wts["w_in_a"], wts["q_norm_g"], wts["kv_norm_g"], wts["wuqt"],
               wts["wuk"], wts["wuvt"]]
    return pl.pallas_call(
        functools.partial(_qkv_kernel, scale=scale, layer=layer),
        grid=(b, s // tm),
        in_specs=[
            pl.BlockSpec((1, tm, d), lambda i, j: (i, j, 0)),
            _layer_spec(mod, layer),
            _whole_spec(wts["ln1_g"]),
            _layer_spec(wts["w_in_a"], layer),
            _whole_spec(wts["q_norm_g"]),
            _whole_spec(wts["kv_norm_g"]),
            _layer_spec(wts["wuqt"], layer),
            _layer_spec(wts["wuk"], layer),
            _layer_spec(wts["wuvt"], layer),
            pl.BlockSpec((1, 1, tm), lambda i, j: (i, 0, j)),
            _whole_spec(invf_col),
        ],
        out_specs=[
            pl.BlockSpec((1, N_HEADS * HEAD_PAD, tm), lambda i, j: (i, 0, j)),
            pl.BlockSpec((1, tm, N_HEADS * HEAD_PAD), lambda i, j: (i, j, 0)),
            pl.BlockSpec((1, ATTN_DIM, tm), lambda i, j: (i, 0, j)),
        ],
        out_shape=[
            jax.ShapeDtypeStruct((b, N_HEADS * HEAD_PAD, s), BF16),
            jax.ShapeDtypeStruct((b, s, N_HEADS * HEAD_PAD), BF16),
            jax.ShapeDtypeStruct((b, ATTN_DIM, s), BF16),
        ],
        compiler_params=pltpu.CompilerParams(
            dimension_semantics=("arbitrary", "arbitrary"),
            vmem_limit_bytes=VMEM_LIMIT_BYTES),
        name="qkv_proj",
    )(x, mod, *stacked, positions.reshape(b, 1, s), invf_col)


def _attn_kernel(qt_ref, k_ref, vt_ref, *refs, n_q, n_cast):
    cast_in, o_ref, cast_out = refs[:n_cast], refs[n_cast], refs[n_cast + 1:2 * n_cast + 1]
    s_ref, acc_ref, m_ref = refs[2 * n_cast + 1:]

    n_below = n_q * (n_q - 1) // 2
    hq = TQ // 2
    ones_rows = jnp.ones((ACC_ROWS - V_DIM, TK), BF16)
    half_mask = (lax.broadcasted_iota(jnp.int32, (hq, hq), 1)
                 >= lax.broadcasted_iota(jnp.int32, (hq, hq), 0))

    def keys(hd, off, n):
        return k_ref[0, pl.ds(off, n), hd * HEAD_PAD:(hd + 1) * HEAD_PAD]

    def queries(hd, off, n):
        return qt_ref[0, hd * HEAD_PAD:(hd + 1) * HEAD_PAD, pl.ds(off, n)]

    def values(hd, off, n):
        return jnp.concatenate(
            [vt_ref[0, hd * V_DIM:(hd + 1) * V_DIM, pl.ds(off, n)], ones_rows[:, :n]], axis=0)

    def col_max(s):
        return jnp.max(s, axis=0, keepdims=True)

    def produce_below(slot, hd, qi, j):
        s = jnp.dot(keys(hd, pl.multiple_of(j * TK, TK), TK), queries(hd, pl.multiple_of(qi * TQ, TQ), TQ),
                    preferred_element_type=F32)
        s_ref[slot, hd] = s
        return col_max(s)

    def consume_below(slot, hd, qi, j, tile_max):
        m = m_ref[qi, hd]
        m_new = jnp.maximum(m, tile_max)
        p = jnp.exp2(s_ref[slot, hd] - m_new).astype(BF16)
        acc_ref[qi, hd] = (jnp.exp2(m - m_new) * acc_ref[qi, hd]
                           + jnp.dot(values(hd, pl.multiple_of(j * TK, TK), TK), p, preferred_element_type=F32))
        m_ref[qi, hd] = m_new

    def produce_diag(slot, hd, d):
        off = d * TQ
        s_ref[slot, hd, 0:hq, :] = jnp.dot(keys(hd, off, hq), queries(hd, off, TQ),
                                           preferred_element_type=F32)
        s_ref[slot, hd, hq:TK, hq:TQ] = jnp.dot(keys(hd, off + hq, hq), queries(hd, off + hq, hq),
                                                preferred_element_type=F32)

    def consume_diag(slot, hd, d):
        off = d * TQ
        neg = jnp.finfo(F32).min
        s_tl = jnp.where(half_mask, s_ref[slot, hd, 0:hq, 0:hq], neg)
        s_tr = s_ref[slot, hd, 0:hq, hq:TQ]
        s_br = jnp.where(half_mask, s_ref[slot, hd, hq:TK, hq:TQ], neg)
        m = m_ref[d, hd]
        m_l = jnp.maximum(m[:, 0:hq], col_max(s_tl))
        m_r = jnp.maximum(m[:, hq:TQ], jnp.maximum(col_max(s_tr), col_max(s_br)))
        p_l = jnp.exp2(s_tl - m_l).astype(BF16)
        p_r = jnp.concatenate([jnp.exp2(s_tr - m_r), jnp.exp2(s_br - m_r)], axis=0).astype(BF16)
        acc = acc_ref[d, hd]
        acc_l = (jnp.exp2(m[:, 0:hq] - m_l) * acc[:, 0:hq]
                 + jnp.dot(values(hd, off, hq), p_l, preferred_element_type=F32))
        acc_r = (jnp.exp2(m[:, hq:TQ] - m_r) * acc[:, hq:TQ]
                 + jnp.dot(values(hd, off, TK), p_r, preferred_element_type=F32))
        rows = slice(hd * V_DIM, (hd + 1) * V_DIM)
        o_ref[0, rows, pl.ds(off, hq)] = (
            acc_l[0:V_DIM] * (1.0 / acc_l[V_DIM:V_DIM + 1])).astype(o_ref.dtype)
        o_ref[0, rows, pl.ds(off + hq, hq)] = (
            acc_r[0:V_DIM] * (1.0 / acc_r[V_DIM:V_DIM + 1])).astype(o_ref.dtype)

    heads = range(ATTN_HEADS)

    def next_below(qi, j):
        row_end = j + 1 == qi
        return jnp.where(row_end, qi + 1, qi), jnp.where(row_end, 0, j + 1)

    def below_step(slot, tile, maxes, to_diag=False):
        nxt = next_below(*tile)
        new_maxes = []
        for hd in heads:
            if to_diag:
                produce_diag(1 - slot, hd, 0)
            else:
                new_maxes.append(produce_below(1 - slot, hd, *nxt))
            consume_below(slot, hd, *tile, maxes[hd])
        return nxt, tuple(new_maxes)

    def diag_step(slot, d, produce_next=True):
        for hd in heads:
            if produce_next:
                produce_diag(1 - slot, hd, d + 1)
            consume_diag(slot, hd, d)
        return d + 1

    tile0 = (jnp.int32(1), jnp.int32(0))
    maxes0 = tuple(produce_below(0, hd, *tile0) for hd in heads)

    acc_ref[...] = jnp.zeros(acc_ref.shape, F32)
    m_ref[...] = jnp.full(m_ref.shape, jnp.finfo(F32).min, F32)
    for src, dst in zip(cast_in, cast_out):
        dst[...] = src[...].astype(BF16)

    def below_pair(carry):
        tile, maxes = carry
        tile, maxes = below_step(0, tile, maxes)
        return below_step(1, tile, maxes)

    def below_trip(_, carry):
        for _ in range(BELOW_PAIRS_PER_TRIP):
            carry = below_pair(carry)
        return carry

    n_pairs = n_below // 2 - 1
    carry = lax.fori_loop(0, n_pairs // BELOW_PAIRS_PER_TRIP, below_trip, (tile0, maxes0))
    for _ in range(n_pairs % BELOW_PAIRS_PER_TRIP):
        carry = below_pair(carry)
    tile, maxes = below_step(0, *carry)
    below_step(1, tile, maxes, to_diag=True)

    for d in range(n_q):
        diag_step(d % 2, d, produce_next=d + 1 < n_q)


def _attention(qt, k, vt, f32_weights, layer):
    b, _, s = qt.shape
    hb = ATTN_HEADS
    groups = N_HEADS // hb
    n_steps = b * groups
    n_q = s // TQ
    assert n_q % 2 == 0 and (n_q * (n_q - 1) // 2) % 2 == 0, "both tile streams are consumed in pairs of tiles"

    def rows_per_step(w):
        rows = w.shape[1] // n_steps
        assert rows * n_steps == w.shape[1] and rows % BF16_SUBLANES == 0, w.shape
        return rows

    cast_in = [pl.BlockSpec((1, rows_per_step(w), w.shape[2]), lambda i, h: (layer, i * groups + h, 0))
               for w in f32_weights]
    cast_out = [pl.BlockSpec((1, rows_per_step(w), w.shape[2]), lambda i, h: (0, i * groups + h, 0))
                for w in f32_weights]
    out = pl.pallas_call(
        functools.partial(_attn_kernel, n_q=n_q, n_cast=len(f32_weights)),
        grid=(b, groups),
        in_specs=[
            pl.BlockSpec((1, hb * HEAD_PAD, s), lambda i, h: (i, h, 0)),
            pl.BlockSpec((1, s, hb * HEAD_PAD), lambda i, h: (i, 0, h)),
            pl.BlockSpec((1, hb * V_DIM, s), lambda i, h: (i, h, 0)),
        ] + cast_in,
        out_specs=[pl.BlockSpec((1, hb * V_DIM, s), lambda i, h: (i, h, 0))] + cast_out,
        out_shape=[jax.ShapeDtypeStruct((b, ATTN_DIM, s), BF16)]
        + [jax.ShapeDtypeStruct((1,) + w.shape[1:], BF16) for w in f32_weights],
        scratch_shapes=[
            pltpu.VMEM((2, hb, TK, TQ), F32),
            pltpu.VMEM((n_q, hb, ACC_ROWS, TQ), F32),
            pltpu.VMEM((n_q, hb, 1, TQ), F32),
        ],
        compiler_params=pltpu.CompilerParams(
            dimension_semantics=("arbitrary", "arbitrary"),
            vmem_limit_bytes=VMEM_LIMIT_BYTES),
        name="mla_attention",
    )(qt, k, vt, *f32_weights)
    return out[0], out[1:]


def _mix_kernel(x_ref, ot_ref, mod_ref, g1_ref, g2_ref, gf_ref, w_in_ref, w_pool_ref, pscale_ref,
                p_pool_ref, p_attn_ref, w_out_ref, w_ff1_ref, w_ff2_ref,
                o_ref, uext_ref, *, final, layer):
    tm = x_ref.shape[1]
    si = pl.program_id(1)

    @pl.when(si == 0)
    def _():
        uext_ref[0:POOL_HALO, :] = jnp.zeros((POOL_HALO, POOL_DIM), F32)

    @pl.when(si > 0)
    def _():
        uext_ref[0:POOL_HALO, :] = uext_ref[tm:tm + POOL_HALO, :]

    shift1, scale1, gate1, shift2, scale2, gate2 = _mod_rows(mod_ref, range(N_MOD))

    for r0 in range(0, tm, MIX_SUB):
        tok = slice(r0, r0 + MIX_SUB)
        x = x_ref[0, tok, :]

        y_b = _tn_dot(ot_ref[0, :, tok], p_attn_ref[0])
        h = ((_rms(x) * g1_ref[layer:layer + 1, :]) * (1.0 + scale1) + shift1).astype(BF16)
        u = jnp.dot(h, w_in_ref[0, :, 0:POOL_DIM], preferred_element_type=F32)
        gz = jnp.dot(h, w_in_ref[0, :, POOL_DIM:], preferred_element_type=F32)
        gz_a = gz[:, 0:D_MODEL]
        gz_b = gz[:, D_MODEL:]

        uext_ref[POOL_HALO + r0:POOL_HALO + r0 + MIX_SUB, :] = u
        head_pos = si * tm + r0 + lax.broadcasted_iota(jnp.int32, (POOL_HALO, 1), 0)
        pooled = []
        for g, w in enumerate(POOL_WINDOWS):
            eg = uext_ref[r0:r0 + POOL_HALO + MIX_SUB,
                          g * POOL_GROUP_DIM:(g + 1) * POOL_GROUP_DIM]
            win = eg
            k = 1
            while k < w:
                win = win + pltpu.roll(win, k, axis=0)
                k *= 2
            win = win[POOL_HALO:]
            ug = eg[POOL_HALO:]
            inv_head = 1.0 / jnp.minimum(head_pos + 1, w).astype(F32)
            mean = jnp.concatenate([win[:POOL_HALO] * inv_head, win[POOL_HALO:] * (1.0 / w)], axis=0)
            yg = jnp.dot((mean - ug).astype(BF16), w_pool_ref[0, g], preferred_element_type=F32)
            pooled.append(yg)
        y_pool = jnp.concatenate(pooled, axis=-1) * pscale_ref[layer:layer + 1, :]
        y_a = jnp.dot(y_pool.astype(BF16), p_pool_ref[0], preferred_element_type=F32)

        merged = _sigmoid(gz_a) * y_a + _sigmoid(gz_b) * y_b
        x1 = x + gate1 * jnp.dot(merged.astype(BF16), w_out_ref[0], preferred_element_type=F32)

        h2 = ((_rms(x1) * g2_ref[layer:layer + 1, :]) * (1.0 + scale2) + shift2).astype(BF16)
        hidden = [jnp.square(jnp.maximum(
            jnp.dot(h2, w_ff1_ref[0, :, c0:c0 + FF_CHUNK], preferred_element_type=F32), 0.0)).astype(BF16)
            for c0 in range(0, D_FF, FF_CHUNK)]
        ff = jnp.dot(jnp.concatenate(hidden, axis=-1), w_ff2_ref[0], preferred_element_type=F32)
        x2 = x1 + gate2 * ff
        if final:
            x2 = _rms(x2) * gf_ref[...]
        o_ref[0, tok, :] = x2


def _mix(x, ot, mod, wts, layer_wts, layer, *, final):
    b, s, d = x.shape
    tm = TM_MIX
    once = dict(pipeline_mode=pl.Buffered(1))
    names = ["ln1_g", "ln2_g", "final_g", "w_in_b", "w_pool", "pool_scale", "p_pool", "p_attn", "w_out",
             "w_ff1", "w_ff2"]
    whole = {"ln1_g", "ln2_g", "final_g", "pool_scale"}
    wts = {**wts, **layer_wts}
    specs = [_whole_spec(wts[n], **once) if n in whole
             else _layer_spec(wts[n], 0 if n in layer_wts else layer, **once)
             for n in names]
    return pl.pallas_call(
        functools.partial(_mix_kernel, final=final, layer=layer),
        grid=(b, s // tm),
        in_specs=[
            pl.BlockSpec((1, tm, d), lambda i, j: (i, j, 0)),
            pl.BlockSpec((1, ATTN_DIM, tm), lambda i, j: (i, 0, j)),
            _layer_spec(mod, layer, **once),
        ] + specs,
        out_specs=pl.BlockSpec((1, tm, d), lambda i, j: (i, j, 0)),
        out_shape=jax.ShapeDtypeStruct((b, s, d), F32),
        scratch_shapes=[pltpu.VMEM((tm + POOL_HALO, POOL_DIM), F32)],
        compiler_params=pltpu.CompilerParams(
            dimension_semantics=("arbitrary", "arbitrary"),
            vmem_limit_bytes=VMEM_LIMIT_BYTES),
        name="mix_mlp",
    )(x, ot, mod, *[wts[n] for n in names])


W_IN_PREP_ROWS = 256


def _w_in_prep_kernel(w_ref, a_ref, b_ref):
    wt = w_ref[0]
    rows = wt.shape[1]
    c0 = POOL_DIM
    c2 = c0 + Q_LORA + KV_LORA
    c3 = c2 + QK_ROPE
    half = QK_ROPE // 2
    kr = wt[c2:c3]
    zl = jnp.zeros((ROPE_LO, rows), F32)
    a_t = jnp.concatenate([wt[c0:c2], zl, kr, -kr[half:], kr[:half]], axis=0)
    a_ref[0] = a_t.T.astype(BF16)
    b_ref[0] = jnp.concatenate([wt[:c0], wt[c3:]], axis=0).T.astype(BF16)


def _w_in_prep(w_in):
    depth, d, n = w_in.shape
    na = Q_LORA + KV_LORA + HEAD_PAD
    nb = n - (Q_LORA + KV_LORA + QK_ROPE)
    rows = W_IN_PREP_ROWS
    w_in = jnp.swapaxes(w_in, 1, 2)
    return pl.pallas_call(
        _w_in_prep_kernel,
        grid=(depth, d // rows),
        in_specs=[pl.BlockSpec((1, n, rows), lambda l, r: (l, 0, r))],
        out_specs=[pl.BlockSpec((1, rows, na), lambda l, r: (l, r, 0)),
                   pl.BlockSpec((1, rows, nb), lambda l, r: (l, r, 0))],
        out_shape=[jax.ShapeDtypeStruct((depth, d, na), BF16), jax.ShapeDtypeStruct((depth, d, nb), BF16)],
        compiler_params=pltpu.CompilerParams(vmem_limit_bytes=VMEM_LIMIT_BYTES),
        name="w_in_prep",
    )(w_in)


def _prep_weights(ln1_g, ln2_g, w_in, q_norm_g, w_uq, kv_norm_g, w_uk, w_uv, w_pool, pool_scale, final_g):
    depth = w_in.shape[0]
    w_in_a, w_in_b = _w_in_prep(w_in)

    wuqt = w_uq.reshape(depth, Q_LORA, N_HEADS * ROPE_HI).transpose(0, 2, 1).astype(BF16)
    wuk = w_uk.reshape(depth, KV_LORA, N_HEADS * QK_NOPE).astype(BF16)
    wuvt = w_uv.reshape(depth, KV_LORA, ATTN_DIM).transpose(0, 2, 1).astype(BF16)
    return dict(
        ln1_g=ln1_g, ln2_g=ln2_g, final_g=final_g.reshape(1, D_MODEL), q_norm_g=q_norm_g,
        kv_norm_g=kv_norm_g, pool_scale=pool_scale, w_in_a=w_in_a, w_in_b=w_in_b, wuqt=wuqt,
        wuk=wuk, wuvt=wuvt, w_pool=w_pool.astype(BF16))


def kernel(x, c, positions, ln1_g, ln2_g, w_ada, b_ada, w_in, q_norm_g, w_uq, kv_norm_g, w_uk,
           w_uv, w_pool, pool_scale, p_pool, p_attn, w_out, w_ff1, w_ff2, final_g):
    depth = w_in.shape[0]
    mod = _modulation(c, w_ada, b_ada)
    wts = _prep_weights(ln1_g, ln2_g, w_in, q_norm_g, w_uq, kv_norm_g, w_uk, w_uv, w_pool, pool_scale,
                        final_g)
    cast_names = ("p_pool", "p_attn", "w_out", "w_ff1", "w_ff2")
    f32_weights = (p_pool, p_attn, w_out, w_ff1, w_ff2)
    for layer in range(depth):
        qt, k, vt = _qkv(x, mod, positions, wts, layer)
        ot, cast = _attention(qt, k, vt, f32_weights, layer)
        x = _mix(x, ot, mod, wts, dict(zip(cast_names, cast)), layer, final=(layer == depth - 1))
    return x
```

```python
import functools
import math

import jax
import jax.numpy as jnp
from jax import lax
from jax.experimental import pallas as pl
from jax.experimental.pallas import tpu as pltpu

D_MODEL = 1024
N_HEADS = 8
QK_NOPE = 64
QK_ROPE = 32
V_DIM = 64
Q_LORA = 384
KV_LORA = 256
POOL_WINDOWS = (2, 4, 8, 16)
POOL_GROUP_DIM = 128
POOL_DIM = len(POOL_WINDOWS) * POOL_GROUP_DIM
ATTN_DIM = N_HEADS * V_DIM
D_FF = 4 * D_MODEL
N_MOD = 6
EPS = 1e-6
ROPE_THETA = 10000.0

HEAD_PAD = 128
ROPE_LO = QK_NOPE
ROPE_HI = QK_NOPE + QK_ROPE
POOL_HALO = 16

VMEM_LIMIT_BYTES = 56 * 1024 * 1024

F32 = jnp.float32
BF16 = jnp.bfloat16

TM_QKV = 1024
QKV_SUB = 512
TM_MIX = 1024
MIX_SUB = 512
TQ = 512
TK = 512
ATTN_HEADS = 4
BELOW_PAIRS_PER_TRIP = 4
F32_SUBLANES = 8
BF16_SUBLANES = 16
ACC_ROWS = V_DIM + BF16_SUBLANES
FF_CHUNK = 1024
MOD_TN = 1536


def _nt_dot(a, b):
    return lax.dot_general(a, b, (((1,), (1,)), ((), ())), preferred_element_type=F32)


def _tn_dot(a, b):
    return lax.dot_general(a, b, (((0,), (0,)), ((), ())), preferred_element_type=F32)


def _rms(x):
    return x * lax.rsqrt(jnp.mean(x * x, axis=-1, keepdims=True) + EPS)


def _sigmoid(x):
    return 1.0 / (1.0 + jnp.exp(-x))


def _mod_rows(mod_ref, chunks):
    row = pl.ds(pl.program_id(0), 1)
    return tuple(mod_ref[0, row, c * D_MODEL:(c + 1) * D_MODEL] for c in chunks)


def _layer_spec(arr, layer, **kw):
    tail = (0,) * (arr.ndim - 1)
    return pl.BlockSpec((1,) + arr.shape[1:], lambda i, j: (layer,) + tail, **kw)


def _whole_spec(arr, **kw):
    zeros = (0,) * arr.ndim
    return pl.BlockSpec(arr.shape, lambda i, j: zeros, **kw)


def _mod_kernel(c_ref, w_ref, b_ref, o_ref):
    c = c_ref[...]
    c_act = c * _sigmoid(c)
    o_ref[0] = jnp.dot(c_act.astype(BF16), w_ref[0].astype(BF16),
                       preferred_element_type=F32) + b_ref[0]


def _modulation(c, w_ada, b_ada):
    depth, d, n = w_ada.shape
    b = c.shape[0]
    rows = F32_SUBLANES
    assert b <= rows
    c_pad = jnp.pad(c, ((0, rows - b), (0, 0)))
    out = pl.pallas_call(
        _mod_kernel,
        grid=(depth, n // MOD_TN),
        in_specs=[
            pl.BlockSpec((rows, d), lambda l, j: (0, 0)),
            pl.BlockSpec((1, d, MOD_TN), lambda l, j: (l, 0, j)),
            pl.BlockSpec((1, 1, MOD_TN), lambda l, j: (l, 0, j)),
        ],
        out_specs=pl.BlockSpec((1, rows, MOD_TN), lambda l, j: (l, 0, j)),
        out_shape=jax.ShapeDtypeStruct((depth, rows, n), F32),
        compiler_params=pltpu.CompilerParams(vmem_limit_bytes=VMEM_LIMIT_BYTES),
        name="adaln_mod",
    )(c_pad, w_ada, b_ada.reshape(depth, 1, n))
    return out


def _qkv_kernel(x_ref, mod_ref, g_ref, w_in_ref, gq_ref, gkv_ref, wuqt_ref,
                wuk_ref, wuvt_ref, pos_ref, invf_ref,
                qt_ref, k_ref, vt_ref, *, scale, layer):
    shift, scl = _mod_rows(mod_ref, (0, 1))
    toks = [slice(r0, r0 + QKV_SUB) for r0 in range(0, x_ref.shape[1], QKV_SUB)]

    hs = [((_rms(x_ref[0, tok, :]) * g_ref[layer:layer + 1, :]) * (1.0 + scl) + shift).astype(BF16)
          for tok in toks]
    zs = [jnp.dot(h, w_in_ref[0], preferred_element_type=F32) for h in hs]

    latents = []
    for z in zs:
        c_q = z[:, 0:Q_LORA]
        c_kv = z[:, Q_LORA:Q_LORA + KV_LORA]
        latents.append(((_rms(c_q) * gq_ref[layer:layer + 1, :]).astype(BF16),
                        (_rms(c_kv) * gkv_ref[layer:layer + 1, :]).astype(BF16)))

    prods = [(_nt_dot(wuqt_ref[0], cqn),
              jnp.dot(ckvn, wuk_ref[0], preferred_element_type=F32), _nt_dot(wuvt_ref[0], ckvn))
             for cqn, ckvn in latents]

    half = QK_ROPE // 2
    z_lo = jnp.zeros((ROPE_LO, QKV_SUB), F32)
    z_hi = jnp.zeros((HEAD_PAD - ROPE_HI, QKV_SUB), F32)
    for tok, z, (qt, k, vt) in zip(toks, zs, prods):
        ang = invf_ref[...] * pos_ref[0, :, tok].astype(F32)
        cost = jnp.cos(ang)
        sint = jnp.sin(ang)
        for hd in range(N_HEADS):
            src = hd * ROPE_HI
            base = hd * HEAD_PAD
            qt_ref[0, base:base + ROPE_LO, tok] = (qt[src:src + ROPE_LO] * scale).astype(BF16)
            x1 = qt[src + ROPE_LO:src + ROPE_LO + half]
            x2 = qt[src + ROPE_LO + half:src + ROPE_HI]
            roped = jnp.concatenate([x1 * cost[:half] - x2 * sint[:half],
                                     x2 * cost[half:] + x1 * sint[half:]], axis=0)
            qt_ref[0, base + ROPE_LO:base + ROPE_HI, tok] = (roped * scale).astype(BF16)
            qt_ref[0, base + ROPE_HI:base + HEAD_PAD, tok] = jnp.zeros((HEAD_PAD - ROPE_HI, QKV_SUB), BF16)

        krx = z[:, Q_LORA + KV_LORA:]
        kr_rot = pltpu.roll(krx, HEAD_PAD - QK_ROPE, axis=1)
        cos_tok = jnp.concatenate([z_lo, cost, z_hi], axis=0).T
        sin_tok = jnp.concatenate([z_lo, sint, z_hi], axis=0).T
        kr_full = krx * cos_tok + kr_rot * sin_tok
        nope_lane = lax.broadcasted_iota(jnp.int32, (QKV_SUB, HEAD_PAD), 1) < QK_NOPE
        for pair in range(N_HEADS // 2):
            two = k[:, pair * HEAD_PAD:(pair + 1) * HEAD_PAD]
            for odd, src in ((0, two), (1, pltpu.roll(two, QK_NOPE, axis=1))):
                base = (2 * pair + odd) * HEAD_PAD
                k_ref[0, tok, base:base + HEAD_PAD] = jnp.where(nope_lane, src, kr_full).astype(BF16)

        vt_ref[0, :, tok] = vt.astype(BF16)


def _qkv(x, mod, positions, wts, layer):
    b, s, d = x.shape
    tm = TM_QKV
    inv_freq = ROPE_THETA ** (-jnp.arange(0, QK_ROPE, 2, dtype=F32) / QK_ROPE)
    invf_col = jnp.concatenate([inv_freq, inv_freq]).reshape(QK_ROPE, 1)
    scale = math.log2(math.e) / math.sqrt(QK_NOPE + QK_ROPE)
    stacked = [wts["ln1_g"], wts["w_in_a"], wts["q_norm_g"], wts["kv_norm_g"], wts["wuqt"],
               wts["wuk"], wts["wuvt"]]
    return pl.pallas_call(
        functools.partial(_qkv_kernel, scale=scale, layer=layer),
        grid=(b, s // tm),
        in_specs=[
            pl.BlockSpec((1, tm, d), lambda i, j: (i, j, 0)),
            _layer_spec(mod, layer),
            _whole_spec(wts["ln1_g"]),
            _layer_spec(wts["w_in_a"], layer),
            _whole_spec(wts["q_norm_g"]),
            _whole_spec(wts["kv_norm_g"]),
            _layer_spec(wts["wuqt"], layer),
            _layer_spec(wts["wuk"], layer),
            _layer_spec(wts["wuvt"], layer),
            pl.BlockSpec((1, 1, tm), lambda i, j: (i, 0, j)),
            _whole_spec(invf_col),
        ],
        out_specs=[
            pl.BlockSpec((1, N_HEADS * HEAD_PAD, tm), lambda i, j: (i, 0, j)),
            pl.BlockSpec((1, tm, N_HEADS * HEAD_PAD), lambda i, j: (i, j, 0)),
            pl.BlockSpec((1, ATTN_DIM, tm), lambda i, j: (i, 0, j)),
        ],
        out_shape=[
            jax.ShapeDtypeStruct((b, N_HEADS * HEAD_PAD, s), BF16),
            jax.ShapeDtypeStruct((b, s, N_HEADS * HEAD_PAD), BF16),
            jax.ShapeDtypeStruct((b, ATTN_DIM, s), BF16),
        ],
        compiler_params=pltpu.CompilerParams(
            dimension_semantics=("arbitrary", "arbitrary"),
            vmem_limit_bytes=VMEM_LIMIT_BYTES),
        name="qkv_proj",
    )(x, mod, *stacked, positions.reshape(b, 1, s), invf_col)


def _attn_kernel(qt_ref, k_ref, vt_ref, *refs, n_q, n_cast):
    cast_in, o_ref, cast_out = refs[:n_cast], refs[n_cast], refs[n_cast + 1:2 * n_cast + 1]
    s_ref, acc_ref, m_ref = refs[2 * n_cast + 1:]

    n_below = n_q * (n_q - 1) // 2
    hq = TQ // 2
    ones_rows = jnp.ones((ACC_ROWS - V_DIM, TK), BF16)
    half_mask = (lax.broadcasted_iota(jnp.int32, (hq, hq), 1)
                 >= lax.broadcasted_iota(jnp.int32, (hq, hq), 0))

    def keys(hd, off, n):
        return k_ref[0, pl.ds(off, n), hd * HEAD_PAD:(hd + 1) * HEAD_PAD]

    def queries(hd, off, n):
        return qt_ref[0, hd * HEAD_PAD:(hd + 1) * HEAD_PAD, pl.ds(off, n)]

    def values(hd, off, n):
        return jnp.concatenate(
            [vt_ref[0, hd * V_DIM:(hd + 1) * V_DIM, pl.ds(off, n)], ones_rows[:, :n]], axis=0)

    def col_max(s):
        return jnp.max(s, axis=0, keepdims=True)

    def produce_below(slot, hd, qi, j):
        s = jnp.dot(keys(hd, pl.multiple_of(j * TK, TK), TK), queries(hd, pl.multiple_of(qi * TQ, TQ), TQ),
                    preferred_element_type=F32)
        s_ref[slot, hd] = s
        return col_max(s)

    def consume_below(slot, hd, qi, j, tile_max):
        m = m_ref[qi, hd]
        m_new = jnp.maximum(m, tile_max)
        p = jnp.exp2(s_ref[slot, hd] - m_new).astype(BF16)
        acc_ref[qi, hd] = (jnp.exp2(m - m_new) * acc_ref[qi, hd]
                           + jnp.dot(values(hd, pl.multiple_of(j * TK, TK), TK), p, preferred_element_type=F32))
        m_ref[qi, hd] = m_new

    def produce_diag(slot, hd, d):
        off = d * TQ
        s_ref[slot, hd, 0:hq, :] = jnp.dot(keys(hd, off, hq), queries(hd, off, TQ),
                                           preferred_element_type=F32)
        s_ref[slot, hd, hq:TK, hq:TQ] = jnp.dot(keys(hd, off + hq, hq), queries(hd, off + hq, hq),
                                                preferred_element_type=F32)

    def consume_diag(slot, hd, d):
        off = d * TQ
        neg = jnp.finfo(F32).min
        s_tl = jnp.where(half_mask, s_ref[slot, hd, 0:hq, 0:hq], neg)
        s_tr = s_ref[slot, hd, 0:hq, hq:TQ]
        s_br = jnp.where(half_mask, s_ref[slot, hd, hq:TK, hq:TQ], neg)
        m = m_ref[d, hd]
        m_l = jnp.maximum(m[:, 0:hq], col_max(s_tl))
        m_r = jnp.maximum(m[:, hq:TQ], jnp.maximum(col_max(s_tr), col_max(s_br)))
        p_l = jnp.exp2(s_tl - m_l).astype(BF16)
        p_r = jnp.concatenate([jnp.exp2(s_tr - m_r), jnp.exp2(s_br - m_r)], axis=0).astype(BF16)
        acc = acc_ref[d, hd]
        acc_l = (jnp.exp2(m[:, 0:hq] - m_l) * acc[:, 0:hq]
                 + jnp.dot(values(hd, off, hq), p_l, preferred_element_type=F32))
        acc_r = (jnp.exp2(m[:, hq:TQ] - m_r) * acc[:, hq:TQ]
                 + jnp.dot(values(hd, off, TK), p_r, preferred_element_type=F32))
        rows = slice(hd * V_DIM, (hd + 1) * V_DIM)
        o_ref[0, rows, pl.ds(off, hq)] = (
            acc_l[0:V_DIM] * (1.0 / acc_l[V_DIM:V_DIM + 1])).astype(o_ref.dtype)
        o_ref[0, rows, pl.ds(off + hq, hq)] = (
            acc_r[0:V_DIM] * (1.0 / acc_r[V_DIM:V_DIM + 1])).astype(o_ref.dtype)

    heads = range(ATTN_HEADS)

    def next_below(qi, j):
        row_end = j + 1 == qi
        return jnp.where(row_end, qi + 1, qi), jnp.where(row_end, 0, j + 1)

    def below_step(slot, tile, maxes, to_diag=False):
        nxt = next_below(*tile)
        new_maxes = []
        for hd in heads:
            if to_diag:
                produce_diag(1 - slot, hd, 0)
            else:
                new_maxes.append(produce_below(1 - slot, hd, *nxt))
            consume_below(slot, hd, *tile, maxes[hd])
        return nxt, tuple(new_maxes)

    def diag_step(slot, d, produce_next=True):
        for hd in heads:
            if produce_next:
                produce_diag(1 - slot, hd, d + 1)
            consume_diag(slot, hd, d)
        return d + 1

    tile0 = (jnp.int32(1), jnp.int32(0))
    maxes0 = tuple(produce_below(0, hd, *tile0) for hd in heads)

    acc_ref[...] = jnp.zeros(acc_ref.shape, F32)
    m_ref[...] = jnp.full(m_ref.shape, jnp.finfo(F32).min, F32)
    for src, dst in zip(cast_in, cast_out):
        dst[...] = src[...].astype(BF16)

    def below_pair(carry):
        tile, maxes = carry
        tile, maxes = below_step(0, tile, maxes)
        return below_step(1, tile, maxes)

    def below_trip(_, carry):
        for _ in range(BELOW_PAIRS_PER_TRIP):
            carry = below_pair(carry)
        return carry

    n_pairs = n_below // 2 - 1
    carry = lax.fori_loop(0, n_pairs // BELOW_PAIRS_PER_TRIP, below_trip, (tile0, maxes0))
    for _ in range(n_pairs % BELOW_PAIRS_PER_TRIP):
        carry = below_pair(carry)
    tile, maxes = below_step(0, *carry)
    below_step(1, tile, maxes, to_diag=True)

    for d in range(n_q):
        diag_step(d % 2, d, produce_next=d + 1 < n_q)


def _attention(qt, k, vt, f32_weights, layer):
    b, _, s = qt.shape
    hb = ATTN_HEADS
    groups = N_HEADS // hb
    n_steps = b * groups
    n_q = s // TQ
    assert n_q % 2 == 0 and (n_q * (n_q - 1) // 2) % 2 == 0, "both tile streams are consumed in pairs of tiles"

    def rows_per_step(w):
        rows = w.shape[1] // n_steps
        assert rows * n_steps == w.shape[1] and rows % BF16_SUBLANES == 0, w.shape
        return rows

    cast_in = [pl.BlockSpec((1, rows_per_step(w), w.shape[2]), lambda i, h: (layer, i * groups + h, 0))
               for w in f32_weights]
    cast_out = [pl.BlockSpec((1, rows_per_step(w), w.shape[2]), lambda i, h: (0, i * groups + h, 0))
                for w in f32_weights]
    out = pl.pallas_call(
        functools.partial(_attn_kernel, n_q=n_q, n_cast=len(f32_weights)),
        grid=(b, groups),
        in_specs=[
            pl.BlockSpec((1, hb * HEAD_PAD, s), lambda i, h: (i, h, 0)),
            pl.BlockSpec((1, s, hb * HEAD_PAD), lambda i, h: (i, 0, h)),
            pl.BlockSpec((1, hb * V_DIM, s), lambda i, h: (i, h, 0)),
        ] + cast_in,
        out_specs=[pl.BlockSpec((1, hb * V_DIM, s), lambda i, h: (i, h, 0))] + cast_out,
        out_shape=[jax.ShapeDtypeStruct((b, ATTN_DIM, s), BF16)]
        + [jax.ShapeDtypeStruct((1,) + w.shape[1:], BF16) for w in f32_weights],
        scratch_shapes=[
            pltpu.VMEM((2, hb, TK, TQ), F32),
            pltpu.VMEM((n_q, hb, ACC_ROWS, TQ), F32),
            pltpu.VMEM((n_q, hb, 1, TQ), F32),
        ],
        compiler_params=pltpu.CompilerParams(
            dimension_semantics=("arbitrary", "arbitrary"),
            vmem_limit_bytes=VMEM_LIMIT_BYTES),
        name="mla_attention",
    )(qt, k, vt, *f32_weights)
    return out[0], out[1:]


def _mix_kernel(x_ref, ot_ref, mod_ref, g1_ref, g2_ref, gf_ref, w_in_ref, w_pool_ref, pscale_ref,
                p_pool_ref, p_attn_ref, w_out_ref, w_ff1_ref, w_ff2_ref,
                o_ref, uext_ref, *, final, layer):
    tm = x_ref.shape[1]
    si = pl.program_id(1)

    @pl.when(si == 0)
    def _():
        uext_ref[0:POOL_HALO, :] = jnp.zeros((POOL_HALO, POOL_DIM), F32)

    @pl.when(si > 0)
    def _():
        uext_ref[0:POOL_HALO, :] = uext_ref[tm:tm + POOL_HALO, :]

    shift1, scale1, gate1, shift2, scale2, gate2 = _mod_rows(mod_ref, range(N_MOD))

    for r0 in range(0, tm, MIX_SUB):
        tok = slice(r0, r0 + MIX_SUB)
        x = x_ref[0, tok, :]

        y_b = _tn_dot(ot_ref[0, :, tok], p_attn_ref[0])
        h = ((_rms(x) * g1_ref[layer:layer + 1, :]) * (1.0 + scale1) + shift1).astype(BF16)
        u = jnp.dot(h, w_in_ref[0, :, 0:POOL_DIM], preferred_element_type=F32)
        gz = jnp.dot(h, w_in_ref[0, :, POOL_DIM:], preferred_element_type=F32)
        gz_a = gz[:, 0:D_MODEL]
        gz_b = gz[:, D_MODEL:]

        uext_ref[POOL_HALO + r0:POOL_HALO + r0 + MIX_SUB, :] = u
        head_pos = si * tm + r0 + lax.broadcasted_iota(jnp.int32, (POOL_HALO, 1), 0)
        pooled = []
        for g, w in enumerate(POOL_WINDOWS):
            eg = uext_ref[r0:r0 + POOL_HALO + MIX_SUB,
                          g * POOL_GROUP_DIM:(g + 1) * POOL_GROUP_DIM]
            win = eg
            k = 1
            while k < w:
                win = win + pltpu.roll(win, k, axis=0)
                k *= 2
            win = win[POOL_HALO:]
            ug = eg[POOL_HALO:]
            inv_head = 1.0 / jnp.minimum(head_pos + 1, w).astype(F32)
            mean = jnp.concatenate([win[:POOL_HALO] * inv_head, win[POOL_HALO:] * (1.0 / w)], axis=0)
            yg = jnp.dot((mean - ug).astype(BF16), w_pool_ref[0, g], preferred_element_type=F32)
            pooled.append(yg)
        y_pool = jnp.concatenate(pooled, axis=-1) * pscale_ref[layer:layer + 1, :]
        y_a = jnp.dot(y_pool.astype(BF16), p_pool_ref[0], preferred_element_type=F32)

        merged = _sigmoid(gz_a) * y_a + _sigmoid(gz_b) * y_b
        x1 = x + gate1 * jnp.dot(merged.astype(BF16), w_out_ref[0], preferred_element_type=F32)

        h2 = ((_rms(x1) * g2_ref[layer:layer + 1, :]) * (1.0 + scale2) + shift2).astype(BF16)
        hidden = [jnp.square(jnp.maximum(
            jnp.dot(h2, w_ff1_ref[0, :, c0:c0 + FF_CHUNK], preferred_element_type=F32), 0.0)).astype(BF16)
            for c0 in range(0, D_FF, FF_CHUNK)]
        ff = jnp.dot(jnp.concatenate(hidden, axis=-1), w_ff2_ref[0], preferred_element_type=F32)
        x2 = x1 + gate2 * ff
        if final:
            x2 = _rms(x2) * gf_ref[...]
        o_ref[0, tok, :] = x2


def _mix(x, ot, mod, wts, layer_wts, layer, *, final):
    b, s, d = x.shape
    tm = TM_MIX
    once = dict(pipeline_mode=pl.Buffered(1))
    names = ["ln1_g", "ln2_g", "final_g", "w_in_b", "w_pool", "pool_scale", "p_pool", "p_attn", "w_out",
             "w_ff1", "w_ff2"]
    whole = {"ln1_g", "ln2_g", "final_g", "pool_scale"}
    wts = {**wts, **layer_wts}
    specs = [_whole_spec(wts[n], **once) if n in whole
             else _layer_spec(wts[n], 0 if n in layer_wts else layer, **once)
             for n in names]
    return pl.pallas_call(
        functools.partial(_mix_kernel, final=final, layer=layer),
        grid=(b, s // tm),
        in_specs=[
            pl.BlockSpec((1, tm, d), lambda i, j: (i, j, 0)),
            pl.BlockSpec((1, ATTN_DIM, tm), lambda i, j: (i, 0, j)),
            _layer_spec(mod, layer, **once),
        ] + specs,
        out_specs=pl.BlockSpec((1, tm, d), lambda i, j: (i, j, 0)),
        out_shape=jax.ShapeDtypeStruct((b, s, d), F32),
        scratch_shapes=[pltpu.VMEM((tm + POOL_HALO, POOL_DIM), F32)],
        compiler_params=pltpu.CompilerParams(
            dimension_semantics=("arbitrary", "arbitrary"),
            vmem_limit_bytes=VMEM_LIMIT_BYTES),
        name="mix_mlp",
    )(x, ot, mod, *[wts[n] for n in names])


W_IN_PREP_ROWS = 256


def _w_in_prep_kernel(w_ref, a_ref, b_ref):
    wt = w_ref[0]
    rows = wt.shape[1]
    c0 = POOL_DIM
    c2 = c0 + Q_LORA + KV_LORA
    c3 = c2 + QK_ROPE
    half = QK_ROPE // 2
    kr = wt[c2:c3]
    zl = jnp.zeros((ROPE_LO, rows), F32)
    a_t = jnp.concatenate([wt[c0:c2], zl, kr, -kr[half:], kr[:half]], axis=0)
    a_ref[0] = a_t.T.astype(BF16)
    b_ref[0] = jnp.concatenate([wt[:c0], wt[c3:]], axis=0).T.astype(BF16)


def _w_in_prep(w_in):
    depth, d, n = w_in.shape
    na = Q_LORA + KV_LORA + HEAD_PAD
    nb = n - (Q_LORA + KV_LORA + QK_ROPE)
    rows = W_IN_PREP_ROWS
    w_in = jnp.swapaxes(w_in, 1, 2)
    return pl.pallas_call(
        _w_in_prep_kernel,
        grid=(depth, d // rows),
        in_specs=[pl.BlockSpec((1, n, rows), lambda l, r: (l, 0, r))],
        out_specs=[pl.BlockSpec((1, rows, na), lambda l, r: (l, r, 0)),
                   pl.BlockSpec((1, rows, nb), lambda l, r: (l, r, 0))],
        out_shape=[jax.ShapeDtypeStruct((depth, d, na), BF16), jax.ShapeDtypeStruct((depth, d, nb), BF16)],
        compiler_params=pltpu.CompilerParams(vmem_limit_bytes=VMEM_LIMIT_BYTES),
        name="w_in_prep",
    )(w_in)


def _prep_weights(ln1_g, ln2_g, w_in, q_norm_g, w_uq, kv_norm_g, w_uk, w_uv, w_pool, pool_scale, final_g):
    depth = w_in.shape[0]
    w_in_a, w_in_b = _w_in_prep(w_in)

    wuqt = w_uq.reshape(depth, Q_LORA, N_HEADS * ROPE_HI).transpose(0, 2, 1).astype(BF16)
    wuk = w_uk.reshape(depth, KV_LORA, N_HEADS * QK_NOPE).astype(BF16)
    wuvt = w_uv.reshape(depth, KV_LORA, ATTN_DIM).transpose(0, 2, 1).astype(BF16)
    return dict(
        ln1_g=ln1_g, ln2_g=ln2_g, final_g=final_g.reshape(1, D_MODEL), q_norm_g=q_norm_g,
        kv_norm_g=kv_norm_g, pool_scale=pool_scale, w_in_a=w_in_a, w_in_b=w_in_b, wuqt=wuqt,
        wuk=wuk, wuvt=wuvt, w_pool=w_pool.astype(BF16))


def kernel(x, c, positions, ln1_g, ln2_g, w_ada, b_ada, w_in, q_norm_g, w_uq, kv_norm_g, w_uk,
           w_uv, w_pool, pool_scale, p_pool, p_attn, w_out, w_ff1, w_ff2, final_g):
    depth = w_in.shape[0]
    mod = _modulation(c, w_ada, b_ada)
    wts = _prep_weights(ln1_g, ln2_g, w_in, q_norm_g, w_uq, kv_norm_g, w_uk, w_uv, w_pool, pool_scale,
                        final_g)
    cast_names = ("p_pool", "p_attn", "w_out", "w_ff1", "w_ff2")
    f32_weights = (p_pool, p_attn, w_out, w_ff1, w_ff2)
    for layer in range(depth):
        qt, k, vt = _qkv(x, mod, positions, wts, layer)
        ot, cast = _attention(qt, k, vt, f32_weights, layer)
        x = _mix(x, ot, mod, wts, dict(zip(cast_names, cast)), layer, final=(layer == depth - 1))
    return x
```

```python
import functools
import math

import jax
import jax.numpy as jnp
from jax import lax
from jax.experimental import pallas as pl
from jax.experimental.pallas import tpu as pltpu

D_MODEL = 1024
N_HEADS = 8
QK_NOPE = 64
QK_ROPE = 32
V_DIM = 64
Q_LORA = 384
KV_LORA = 256
POOL_WINDOWS = (2, 4, 8, 16)
POOL_GROUP_DIM = 128
POOL_DIM = len(POOL_WINDOWS) * POOL_GROUP_DIM
ATTN_DIM = N_HEADS * V_DIM
D_FF = 4 * D_MODEL
N_MOD = 6
EPS = 1e-6
ROPE_THETA = 10000.0

HEAD_PAD = 128
ROPE_LO = QK_NOPE
ROPE_HI = QK_NOPE + QK_ROPE
POOL_HALO = 16

VMEM_LIMIT_BYTES = 56 * 1024 * 1024

F32 = jnp.float32
BF16 = jnp.bfloat16

TM_QKV = 1024
QKV_SUB = 256
TM_MIX = 1024
MIX_SUB = 512
TQ = 512
TK = 512
ATTN_HEADS = 4
BELOW_PAIRS_PER_TRIP = 4
F32_SUBLANES = 8
BF16_SUBLANES = 16
ACC_ROWS = V_DIM + BF16_SUBLANES
FF_CHUNK = 1024
MOD_TN = 1536


def _nt_dot(a, b):
    return lax.dot_general(a, b, (((1,), (1,)), ((), ())), preferred_element_type=F32)


def _tn_dot(a, b):
    return lax.dot_general(a, b, (((0,), (0,)), ((), ())), preferred_element_type=F32)


def _rms(x):
    return x * lax.rsqrt(jnp.mean(x * x, axis=-1, keepdims=True) + EPS)


def _sigmoid(x):
    return 1.0 / (1.0 + jnp.exp(-x))


def _mod_rows(mod_ref, chunks):
    row = pl.ds(pl.program_id(0), 1)
    return tuple(mod_ref[0, row, c * D_MODEL:(c + 1) * D_MODEL] for c in chunks)


def _layer_spec(arr, layer, **kw):
    tail = (0,) * (arr.ndim - 1)
    return pl.BlockSpec((1,) + arr.shape[1:], lambda i, j: (layer,) + tail, **kw)


def _whole_spec(arr, **kw):
    zeros = (0,) * arr.ndim
    return pl.BlockSpec(arr.shape, lambda i, j: zeros, **kw)


def _mod_kernel(c_ref, w_ref, b_ref, o_ref):
    c = c_ref[...]
    c_act = c * _sigmoid(c)
    o_ref[0] = jnp.dot(c_act.astype(BF16), w_ref[0].astype(BF16),
                       preferred_element_type=F32) + b_ref[0]


def _modulation(c, w_ada, b_ada):
    depth, d, n = w_ada.shape
    b = c.shape[0]
    rows = F32_SUBLANES
    assert b <= rows
    c_pad = jnp.pad(c, ((0, rows - b), (0, 0)))
    out = pl.pallas_call(
        _mod_kernel,
        grid=(depth, n // MOD_TN),
        in_specs=[
            pl.BlockSpec((rows, d), lambda l, j: (0, 0)),
            pl.BlockSpec((1, d, MOD_TN), lambda l, j: (l, 0, j)),
            pl.BlockSpec((1, 1, MOD_TN), lambda l, j: (l, 0, j)),
        ],
        out_specs=pl.BlockSpec((1, rows, MOD_TN), lambda l, j: (l, 0, j)),
        out_shape=jax.ShapeDtypeStruct((depth, rows, n), F32),
        compiler_params=pltpu.CompilerParams(vmem_limit_bytes=VMEM_LIMIT_BYTES),
        name="adaln_mod",
    )(c_pad, w_ada, b_ada.reshape(depth, 1, n))
    return out


def _qkv_kernel(x_ref, mod_ref, g_ref, w_in_ref, gq_ref, gkv_ref, wuqt_ref,
                wuk_ref, wuvt_ref, pos_ref, invf_ref,
                qt_ref, k_ref, vt_ref, *, scale, layer):
    shift, scl = _mod_rows(mod_ref, (0, 1))
    toks = [slice(r0, r0 + QKV_SUB) for r0 in range(0, x_ref.shape[1], QKV_SUB)]

    hs = [((_rms(x_ref[0, tok, :]) * g_ref[layer:layer + 1, :]) * (1.0 + scl) + shift).astype(BF16)
          for tok in toks]
    zs = [jnp.dot(h, w_in_ref[0], preferred_element_type=F32) for h in hs]

    latents = []
    for z in zs:
        c_q = z[:, 0:Q_LORA]
        c_kv = z[:, Q_LORA:Q_LORA + KV_LORA]
        latents.append(((_rms(c_q) * gq_ref[layer:layer + 1, :]).astype(BF16),
                        (_rms(c_kv) * gkv_ref[layer:layer + 1, :]).astype(BF16)))

    prods = [(_nt_dot(wuqt_ref[0], cqn),
              jnp.dot(ckvn, wuk_ref[0], preferred_element_type=F32), _nt_dot(wuvt_ref[0], ckvn))
             for cqn, ckvn in latents]

    half = QK_ROPE // 2
    z_lo = jnp.zeros((ROPE_LO, QKV_SUB), F32)
    z_hi = jnp.zeros((HEAD_PAD - ROPE_HI, QKV_SUB), F32)
    for tok, z, (qt, k, vt) in zip(toks, zs, prods):
        ang = invf_ref[...] * pos_ref[0, :, tok].astype(F32)
        cost = jnp.cos(ang)
        sint = jnp.sin(ang)
        for hd in range(N_HEADS):
            src = hd * ROPE_HI
            base = hd * HEAD_PAD
            qt_ref[0, base:base + ROPE_LO, tok] = (qt[src:src + ROPE_LO] * scale).astype(BF16)
            x1 = qt[src + ROPE_LO:src + ROPE_LO + half]
            x2 = qt[src + ROPE_LO + half:src + ROPE_HI]
            roped = jnp.concatenate([x1 * cost[:half] - x2 * sint[:half],
                                     x2 * cost[half:] + x1 * sint[half:]], axis=0)
            qt_ref[0, base + ROPE_LO:base + ROPE_HI, tok] = (roped * scale).astype(BF16)
            qt_ref[0, base + ROPE_HI:base + HEAD_PAD, tok] = jnp.zeros((HEAD_PAD - ROPE_HI, QKV_SUB), BF16)

        krx = z[:, Q_LORA + KV_LORA:]
        kr_rot = pltpu.roll(krx, HEAD_PAD - QK_ROPE, axis=1)
        cos_tok = jnp.concatenate([z_lo, cost, z_hi], axis=0).T
        sin_tok = jnp.concatenate([z_lo, sint, z_hi], axis=0).T
        kr_full = krx * cos_tok + kr_rot * sin_tok
        nope_lane = lax.broadcasted_iota(jnp.int32, (QKV_SUB, HEAD_PAD), 1) < QK_NOPE
        for pair in range(N_HEADS // 2):
            two = k[:, pair * HEAD_PAD:(pair + 1) * HEAD_PAD]
            for odd, src in ((0, two), (1, pltpu.roll(two, QK_NOPE, axis=1))):
                base = (2 * pair + odd) * HEAD_PAD
                k_ref[0, tok, base:base + HEAD_PAD] = jnp.where(nope_lane, src, kr_full).astype(BF16)

        vt_ref[0, :, tok] = vt.astype(BF16)


def _qkv(x, mod, positions, wts, layer):
    b, s, d = x.shape
    tm = TM_QKV
    inv_freq = ROPE_THETA ** (-jnp.arange(0, QK_ROPE, 2, dtype=F32) / QK_ROPE)
    invf_col = jnp.concatenate([inv_freq, inv_freq]).reshape(QK_ROPE, 1)
    scale = math.log2(math.e) / math.sqrt(QK_NOPE + QK_ROPE)
    stacked = [wts["ln1_g"], wts["w_in_a"], wts["q_norm_g"], wts["kv_norm_g"], wts["wuqt"],
               wts["wuk"], wts["wuvt"]]
    return pl.pallas_call(
        functools.partial(_qkv_kernel, scale=scale, layer=layer),
        grid=(b, s // tm),
        in_specs=[
            pl.BlockSpec((1, tm, d), lambda i, j: (i, j, 0)),
            _layer_spec(mod, layer),
            _whole_spec(wts["ln1_g"]),
            _layer_spec(wts["w_in_a"], layer),
            _whole_spec(wts["q_norm_g"]),
            _whole_spec(wts["kv_norm_g"]),
            _layer_spec(wts["wuqt"], layer),
            _layer_spec(wts["wuk"], layer),
            _layer_spec(wts["wuvt"], layer),
            pl.BlockSpec((1, 1, tm), lambda i, j: (i, 0, j)),
            _whole_spec(invf_col),
        ],
        out_specs=[
            pl.BlockSpec((1, N_HEADS * HEAD_PAD, tm), lambda i, j: (i, 0, j)),
            pl.BlockSpec((1, tm, N_HEADS * HEAD_PAD), lambda i, j: (i, j, 0)),
            pl.BlockSpec((1, ATTN_DIM, tm), lambda i, j: (i, 0, j)),
        ],
        out_shape=[
            jax.ShapeDtypeStruct((b, N_HEADS * HEAD_PAD, s), BF16),
            jax.ShapeDtypeStruct((b, s, N_HEADS * HEAD_PAD), BF16),
            jax.ShapeDtypeStruct((b, ATTN_DIM, s), BF16),
        ],
        compiler_params=pltpu.CompilerParams(
            dimension_semantics=("arbitrary", "arbitrary"),
            vmem_limit_bytes=VMEM_LIMIT_BYTES),
        name="qkv_proj",
    )(x, mod, *stacked, positions.reshape(b, 1, s), invf_col)


def _attn_kernel(qt_ref, k_ref, vt_ref, *refs, n_q, n_cast):
    cast_in, o_ref, cast_out = refs[:n_cast], refs[n_cast], refs[n_cast + 1:2 * n_cast + 1]
    s_ref, acc_ref, m_ref = refs[2 * n_cast + 1:]

    n_below = n_q * (n_q - 1) // 2
    hq = TQ // 2
    ones_rows = jnp.ones((ACC_ROWS - V_DIM, TK), BF16)
    half_mask = (lax.broadcasted_iota(jnp.int32, (hq, hq), 1)
                 >= lax.broadcasted_iota(jnp.int32, (hq, hq), 0))

    def keys(hd, off, n):
        return k_ref[0, pl.ds(off, n), hd * HEAD_PAD:(hd + 1) * HEAD_PAD]

    def queries(hd, off, n):
        return qt_ref[0, hd * HEAD_PAD:(hd + 1) * HEAD_PAD, pl.ds(off, n)]

    def values(hd, off, n):
        return jnp.concatenate(
            [vt_ref[0, hd * V_DIM:(hd + 1) * V_DIM, pl.ds(off, n)], ones_rows[:, :n]], axis=0)

    def col_max(s):
        return jnp.max(s, axis=0, keepdims=True)

    def produce_below(slot, hd, qi, j):
        s = jnp.dot(keys(hd, pl.multiple_of(j * TK, TK), TK), queries(hd, pl.multiple_of(qi * TQ, TQ), TQ),
                    preferred_element_type=F32)
        s_ref[slot, hd] = s
        return col_max(s)

    def consume_below(slot, hd, qi, j, tile_max):
        m = m_ref[qi, hd]
        m_new = jnp.maximum(m, tile_max)
        p = jnp.exp2(s_ref[slot, hd] - m_new).astype(BF16)
        acc_ref[qi, hd] = (jnp.exp2(m - m_new) * acc_ref[qi, hd]
                           + jnp.dot(values(hd, pl.multiple_of(j * TK, TK), TK), p, preferred_element_type=F32))
        m_ref[qi, hd] = m_new

    def produce_diag(slot, hd, d):
        off = d * TQ
        s_ref[slot, hd, 0:hq, :] = jnp.dot(keys(hd, off, hq), queries(hd, off, TQ),
                                           preferred_element_type=F32)
        s_ref[slot, hd, hq:TK, hq:TQ] = jnp.dot(keys(hd, off + hq, hq), queries(hd, off + hq, hq),
                                                preferred_element_type=F32)

    def consume_diag(slot, hd, d):
        off = d * TQ
        neg = jnp.finfo(F32).min
        s_tl = jnp.where(half_mask, s_ref[slot, hd, 0:hq, 0:hq], neg)
        s_tr = s_ref[slot, hd, 0:hq, hq:TQ]
        s_br = jnp.where(half_mask, s_ref[slot, hd, hq:TK, hq:TQ], neg)
        m = m_ref[d, hd]
        m_l = jnp.maximum(m[:, 0:hq], col_max(s_tl))
        m_r = jnp.maximum(m[:, hq:TQ], jnp.maximum(col_max(s_tr), col_max(s_br)))
        p_l = jnp.exp2(s_tl - m_l).astype(BF16)
        p_r = jnp.concatenate([jnp.exp2(s_tr - m_r), jnp.exp2(s_br - m_r)], axis=0).astype(BF16)
        acc = acc_ref[d, hd]
        acc_l = (jnp.exp2(m[:, 0:hq] - m_l) * acc[:, 0:hq]
                 + jnp.dot(values(hd, off, hq), p_l, preferred_element_type=F32))
        acc_r = (jnp.exp2(m[:, hq:TQ] - m_r) * acc[:, hq:TQ]
                 + jnp.dot(values(hd, off, TK), p_r, preferred_element_type=F32))
        rows = slice(hd * V_DIM, (hd + 1) * V_DIM)
        o_ref[0, rows, pl.ds(off, hq)] = (
            acc_l[0:V_DIM] * (1.0 / acc_l[V_DIM:V_DIM + 1])).astype(o_ref.dtype)
        o_ref[0, rows, pl.ds(off + hq, hq)] = (
            acc_r[0:V_DIM] * (1.0 / acc_r[V_DIM:V_DIM + 1])).astype(o_ref.dtype)

    heads = range(ATTN_HEADS)

    def next_below(qi, j):
        row_end = j + 1 == qi
        return jnp.where(row_end, qi + 1, qi), jnp.where(row_end, 0, j + 1)

    def below_step(slot, tile, maxes, to_diag=False):
        nxt = next_below(*tile)
        new_maxes = []
        for hd in heads:
            if to_diag:
                produce_diag(1 - slot, hd, 0)
            else:
                new_maxes.append(produce_below(1 - slot, hd, *nxt))
            consume_below(slot, hd, *tile, maxes[hd])
        return nxt, tuple(new_maxes)

    def diag_step(slot, d, produce_next=True):
        for hd in heads:
            if produce_next:
                produce_diag(1 - slot, hd, d + 1)
            consume_diag(slot, hd, d)
        return d + 1

    tile0 = (jnp.int32(1), jnp.int32(0))
    maxes0 = tuple(produce_below(0, hd, *tile0) for hd in heads)

    acc_ref[...] = jnp.zeros(acc_ref.shape, F32)
    m_ref[...] = jnp.full(m_ref.shape, jnp.finfo(F32).min, F32)
    for src, dst in zip(cast_in, cast_out):
        dst[...] = src[...].astype(BF16)

    def below_pair(carry):
        tile, maxes = carry
        tile, maxes = below_step(0, tile, maxes)
        return below_step(1, tile, maxes)

    def below_trip(_, carry):
        for _ in range(BELOW_PAIRS_PER_TRIP):
            carry = below_pair(carry)
        return carry

    n_pairs = n_below // 2 - 1
    carry = lax.fori_loop(0, n_pairs // BELOW_PAIRS_PER_TRIP, below_trip, (tile0, maxes0))
    for _ in range(n_pairs % BELOW_PAIRS_PER_TRIP):
        carry = below_pair(carry)
    tile, maxes = below_step(0, *carry)
    below_step(1, tile, maxes, to_diag=True)

    for d in range(n_q):
        diag_step(d % 2, d, produce_next=d + 1 < n_q)


def _attention(qt, k, vt, f32_weights, layer):
    b, _, s = qt.shape
    hb = ATTN_HEADS
    groups = N_HEADS // hb
    n_steps = b * groups
    n_q = s // TQ
    assert n_q % 2 == 0 and (n_q * (n_q - 1) // 2) % 2 == 0, "both tile streams run two tiles per loop trip"

    def rows_per_step(w):
        rows = w.shape[1] // n_steps
        assert rows * n_steps == w.shape[1] and rows % BF16_SUBLANES == 0, w.shape
        return rows

    cast_in = [pl.BlockSpec((1, rows_per_step(w), w.shape[2]), lambda i, h: (layer, i * groups + h, 0))
               for w in f32_weights]
    cast_out = [pl.BlockSpec((1, rows_per_step(w), w.shape[2]), lambda i, h: (0, i * groups + h, 0))
                for w in f32_weights]
    out = pl.pallas_call(
        functools.partial(_attn_kernel, n_q=n_q, n_cast=len(f32_weights)),
        grid=(b, groups),
        in_specs=[
            pl.BlockSpec((1, hb * HEAD_PAD, s), lambda i, h: (i, h, 0)),
            pl.BlockSpec((1, s, hb * HEAD_PAD), lambda i, h: (i, 0, h)),
            pl.BlockSpec((1, hb * V_DIM, s), lambda i, h: (i, h, 0)),
        ] + cast_in,
        out_specs=[pl.BlockSpec((1, hb * V_DIM, s), lambda i, h: (i, h, 0))] + cast_out,
        out_shape=[jax.ShapeDtypeStruct((b, ATTN_DIM, s), BF16)]
        + [jax.ShapeDtypeStruct((1,) + w.shape[1:], BF16) for w in f32_weights],
        scratch_shapes=[
            pltpu.VMEM((2, hb, TK, TQ), F32),
            pltpu.VMEM((n_q, hb, ACC_ROWS, TQ), F32),
            pltpu.VMEM((n_q, hb, 1, TQ), F32),
        ],
        compiler_params=pltpu.CompilerParams(
            dimension_semantics=("arbitrary", "arbitrary"),
            vmem_limit_bytes=VMEM_LIMIT_BYTES),
        name="mla_attention",
    )(qt, k, vt, *f32_weights)
    return out[0], out[1:]


def _mix_kernel(x_ref, ot_ref, mod_ref, g1_ref, g2_ref, gf_ref, w_in_ref, w_pool_ref, pscale_ref,
                p_pool_ref, p_attn_ref, w_out_ref, w_ff1_ref, w_ff2_ref,
                o_ref, uext_ref, *, final, layer):
    tm = x_ref.shape[1]
    si = pl.program_id(1)

    @pl.when(si == 0)
    def _():
        uext_ref[0:POOL_HALO, :] = jnp.zeros((POOL_HALO, POOL_DIM), F32)

    @pl.when(si > 0)
    def _():
        uext_ref[0:POOL_HALO, :] = uext_ref[tm:tm + POOL_HALO, :]

    shift1, scale1, gate1, shift2, scale2, gate2 = _mod_rows(mod_ref, range(N_MOD))

    for r0 in range(0, tm, MIX_SUB):
        tok = slice(r0, r0 + MIX_SUB)
        x = x_ref[0, tok, :]

        y_b = _tn_dot(ot_ref[0, :, tok], p_attn_ref[0])
        h = ((_rms(x) * g1_ref[layer:layer + 1, :]) * (1.0 + scale1) + shift1).astype(BF16)
        u = jnp.dot(h, w_in_ref[0, :, 0:POOL_DIM], preferred_element_type=F32)
        gz = jnp.dot(h, w_in_ref[0, :, POOL_DIM:], preferred_element_type=F32)
        gz_a = gz[:, 0:D_MODEL]
        gz_b = gz[:, D_MODEL:]

        uext_ref[POOL_HALO + r0:POOL_HALO + r0 + MIX_SUB, :] = u
        head_pos = si * tm + r0 + lax.broadcasted_iota(jnp.int32, (POOL_HALO, 1), 0)
        pooled = []
        for g, w in enumerate(POOL_WINDOWS):
            eg = uext_ref[r0:r0 + POOL_HALO + MIX_SUB,
                          g * POOL_GROUP_DIM:(g + 1) * POOL_GROUP_DIM]
            win = eg
            k = 1
            while k < w:
                win = win + pltpu.roll(win, k, axis=0)
                k *= 2
            win = win[POOL_HALO:]
            ug = eg[POOL_HALO:]
            inv_head = 1.0 / jnp.minimum(head_pos + 1, w).astype(F32)
            mean = jnp.concatenate([win[:POOL_HALO] * inv_head, win[POOL_HALO:] * (1.0 / w)], axis=0)
            yg = jnp.dot((mean - ug).astype(BF16), w_pool_ref[0, g], preferred_element_type=F32)
            pooled.append(yg)
        y_pool = jnp.concatenate(pooled, axis=-1) * pscale_ref[layer:layer + 1, :]
        y_a = jnp.dot(y_pool.astype(BF16), p_pool_ref[0], preferred_element_type=F32)

        merged = _sigmoid(gz_a) * y_a + _sigmoid(gz_b) * y_b
        x1 = x + gate1 * jnp.dot(merged.astype(BF16), w_out_ref[0], preferred_element_type=F32)

        h2 = ((_rms(x1) * g2_ref[layer:layer + 1, :]) * (1.0 + scale2) + shift2).astype(BF16)
        hidden = [jnp.square(jnp.maximum(
            jnp.dot(h2, w_ff1_ref[0, :, c0:c0 + FF_CHUNK], preferred_element_type=F32), 0.0)).astype(BF16)
            for c0 in range(0, D_FF, FF_CHUNK)]
        ff = jnp.dot(jnp.concatenate(hidden, axis=-1), w_ff2_ref[0], preferred_element_type=F32)
        x2 = x1 + gate2 * ff
        if final:
            x2 = _rms(x2) * gf_ref[...]
        o_ref[0, tok, :] = x2


def _mix(x, ot, mod, wts, layer_wts, layer, *, final):
    b, s, d = x.shape
    tm = TM_MIX
    once = dict(pipeline_mode=pl.Buffered(1))
    names = ["ln1_g", "ln2_g", "final_g", "w_in_b", "w_pool", "pool_scale", "p_pool", "p_attn", "w_out",
             "w_ff1", "w_ff2"]
    whole = {"ln1_g", "ln2_g", "final_g", "pool_scale"}
    wts = {**wts, **layer_wts}
    specs = [_whole_spec(wts[n], **once) if n in whole
             else _layer_spec(wts[n], 0 if n in layer_wts else layer, **once)
             for n in names]
    return pl.pallas_call(
        functools.partial(_mix_kernel, final=final, layer=layer),
        grid=(b, s // tm),
        in_specs=[
            pl.BlockSpec((1, tm, d), lambda i, j: (i, j, 0)),
            pl.BlockSpec((1, ATTN_DIM, tm), lambda i, j: (i, 0, j)),
            _layer_spec(mod, layer, **once),
        ] + specs,
        out_specs=pl.BlockSpec((1, tm, d), lambda i, j: (i, j, 0)),
        out_shape=jax.ShapeDtypeStruct((b, s, d), F32),
        scratch_shapes=[pltpu.VMEM((tm + POOL_HALO, POOL_DIM), F32)],
        compiler_params=pltpu.CompilerParams(
            dimension_semantics=("arbitrary", "arbitrary"),
            vmem_limit_bytes=VMEM_LIMIT_BYTES),
        name="mix_mlp",
    )(x, ot, mod, *[wts[n] for n in names])


W_IN_PREP_ROWS = 256


def _w_in_prep_kernel(w_ref, a_ref, b_ref):
    wt = w_ref[0]
    rows = wt.shape[1]
    c0 = POOL_DIM
    c2 = c0 + Q_LORA + KV_LORA
    c3 = c2 + QK_ROPE
    half = QK_ROPE // 2
    kr = wt[c2:c3]
    zl = jnp.zeros((ROPE_LO, rows), F32)
    a_t = jnp.concatenate([wt[c0:c2], zl, kr, -kr[half:], kr[:half]], axis=0)
    a_ref[0] = a_t.T.astype(BF16)
    b_ref[0] = jnp.concatenate([wt[:c0], wt[c3:]], axis=0).T.astype(BF16)


def _w_in_prep(w_in):
    depth, d, n = w_in.shape
    na = Q_LORA + KV_LORA + HEAD_PAD
    nb = n - (Q_LORA + KV_LORA + QK_ROPE)
    rows = W_IN_PREP_ROWS
    w_in = jnp.swapaxes(w_in, 1, 2)
    return pl.pallas_call(
        _w_in_prep_kernel,
        grid=(depth, d // rows),
        in_specs=[pl.BlockSpec((1, n, rows), lambda l, r: (l, 0, r))],
        out_specs=[pl.BlockSpec((1, rows, na), lambda l, r: (l, r, 0)),
                   pl.BlockSpec((1, rows, nb), lambda l, r: (l, r, 0))],
        out_shape=[jax.ShapeDtypeStruct((depth, d, na), BF16), jax.ShapeDtypeStruct((depth, d, nb), BF16)],
        compiler_params=pltpu.CompilerParams(vmem_limit_bytes=VMEM_LIMIT_BYTES),
        name="w_in_prep",
    )(w_in)


def _prep_weights(ln1_g, ln2_g, w_in, q_norm_g, w_uq, kv_norm_g, w_uk, w_uv, w_pool, pool_scale, final_g):
    depth = w_in.shape[0]
    w_in_a, w_in_b = _w_in_prep(w_in)

    wuqt = w_uq.reshape(depth, Q_LORA, N_HEADS * ROPE_HI).transpose(0, 2, 1).astype(BF16)
    wuk = w_uk.reshape(depth, KV_LORA, N_HEADS * QK_NOPE).astype(BF16)
    wuvt = w_uv.reshape(depth, KV_LORA, ATTN_DIM).transpose(0, 2, 1).astype(BF16)
    return dict(
        ln1_g=ln1_g, ln2_g=ln2_g, final_g=final_g.reshape(1, D_MODEL), q_norm_g=q_norm_g,
        kv_norm_g=kv_norm_g, pool_scale=pool_scale, w_in_a=w_in_a, w_in_b=w_in_b, wuqt=wuqt,
        wuk=wuk, wuvt=wuvt, w_pool=w_pool.astype(BF16))


def kernel(x, c, positions, ln1_g, ln2_g, w_ada, b_ada, w_in, q_norm_g, w_uq, kv_norm_g, w_uk,
           w_uv, w_pool, pool_scale, p_pool, p_attn, w_out, w_ff1, w_ff2, final_g):
    depth = w_in.shape[0]
    mod = _modulation(c, w_ada, b_ada)
    wts = _prep_weights(ln1_g, ln2_g, w_in, q_norm_g, w_uq, kv_norm_g, w_uk, w_uv, w_pool, pool_scale,
                        final_g)
    cast_names = ("p_pool", "p_attn", "w_out", "w_ff1", "w_ff2")
    f32_weights = (p_pool, p_attn, w_out, w_ff1, w_ff2)
    for layer in range(depth):
        qt, k, vt = _qkv(x, mod, positions, wts, layer)
        ot, cast = _attention(qt, k, vt, f32_weights, layer)
        x = _mix(x, ot, mod, wts, dict(zip(cast_names, cast)), layer, final=(layer == depth - 1))
    return x
```

```python
import functools
import math

import jax
import jax.numpy as jnp
from jax import lax
from jax.experimental import pallas as pl
from jax.experimental.pallas import tpu as pltpu

D_MODEL = 1024
N_HEADS = 8
QK_NOPE = 64
QK_ROPE = 32
V_DIM = 64
Q_LORA = 384
KV_LORA = 256
POOL_WINDOWS = (2, 4, 8, 16)
POOL_GROUP_DIM = 128
POOL_DIM = len(POOL_WINDOWS) * POOL_GROUP_DIM
ATTN_DIM = N_HEADS * V_DIM
D_FF = 4 * D_MODEL
N_MOD = 6
EPS = 1e-6
ROPE_THETA = 10000.0

HEAD_PAD = 128
ROPE_LO = QK_NOPE
ROPE_HI = QK_NOPE + QK_ROPE
POOL_HALO = 16

VMEM_LIMIT_BYTES = 56 * 1024 * 1024

F32 = jnp.float32
BF16 = jnp.bfloat16

TM_QKV = 1024
QKV_SUB = 256
TM_MIX = 1024
MIX_SUB = 512
TQ = 512
TK = 512
ATTN_HEADS = 4
BELOW_PAIRS_PER_TRIP = 4
F32_SUBLANES = 8
BF16_SUBLANES = 16
ACC_ROWS = V_DIM + BF16_SUBLANES
FF_CHUNK = 1024
MOD_TN = 1536


def _nt_dot(a, b):
    return lax.dot_general(a, b, (((1,), (1,)), ((), ())), preferred_element_type=F32)


def _tn_dot(a, b):
    return lax.dot_general(a, b, (((0,), (0,)), ((), ())), preferred_element_type=F32)


def _rms(x):
    return x * lax.rsqrt(jnp.mean(x * x, axis=-1, keepdims=True) + EPS)


def _sigmoid(x):
    return 1.0 / (1.0 + jnp.exp(-x))


def _mod_rows(mod_ref, chunks):
    row = pl.ds(pl.program_id(0), 1)
    return tuple(mod_ref[0, row, c * D_MODEL:(c + 1) * D_MODEL] for c in chunks)


def _layer_spec(arr, layer, **kw):
    tail = (0,) * (arr.ndim - 1)
    return pl.BlockSpec((1,) + arr.shape[1:], lambda i, j: (layer,) + tail, **kw)


def _whole_spec(arr, **kw):
    zeros = (0,) * arr.ndim
    return pl.BlockSpec(arr.shape, lambda i, j: zeros, **kw)


def _mod_kernel(c_ref, w_ref, b_ref, o_ref):
    c = c_ref[...]
    c_act = c * _sigmoid(c)
    o_ref[0] = jnp.dot(c_act.astype(BF16), w_ref[0].astype(BF16),
                       preferred_element_type=F32) + b_ref[0]


def _modulation(c, w_ada, b_ada):
    depth, d, n = w_ada.shape
    b = c.shape[0]
    rows = F32_SUBLANES
    assert b <= rows
    c_pad = jnp.pad(c, ((0, rows - b), (0, 0)))
    out = pl.pallas_call(
        _mod_kernel,
        grid=(depth, n // MOD_TN),
        in_specs=[
            pl.BlockSpec((rows, d), lambda l, j: (0, 0)),
            pl.BlockSpec((1, d, MOD_TN), lambda l, j: (l, 0, j)),
            pl.BlockSpec((1, 1, MOD_TN), lambda l, j: (l, 0, j)),
        ],
        out_specs=pl.BlockSpec((1, rows, MOD_TN), lambda l, j: (l, 0, j)),
        out_shape=jax.ShapeDtypeStruct((depth, rows, n), F32),
        compiler_params=pltpu.CompilerParams(vmem_limit_bytes=VMEM_LIMIT_BYTES),
        name="adaln_mod",
    )(c_pad, w_ada, b_ada.reshape(depth, 1, n))
    return out


def _qkv_kernel(x_ref, mod_ref, g_ref, w_in_ref, gq_ref, gkv_ref, wuqt_ref,
                wuk_ref, wuvt_ref, pos_ref, invf_ref,
                qt_ref, k_ref, vt_ref, *, scale, layer):
    shift, scl = _mod_rows(mod_ref, (0, 1))
    toks = [slice(r0, r0 + QKV_SUB) for r0 in range(0, x_ref.shape[1], QKV_SUB)]

    hs = [((_rms(x_ref[0, tok, :]) * g_ref[layer:layer + 1, :]) * (1.0 + scl) + shift).astype(BF16)
          for tok in toks]
    zs = [jnp.dot(h, w_in_ref[0], preferred_element_type=F32) for h in hs]

    latents = []
    for z in zs:
        c_q = z[:, 0:Q_LORA]
        c_kv = z[:, Q_LORA:Q_LORA + KV_LORA]
        latents.append(((_rms(c_q) * gq_ref[layer:layer + 1, :]).astype(BF16),
                        (_rms(c_kv) * gkv_ref[layer:layer + 1, :]).astype(BF16)))

    prods = [(_nt_dot(wuqt_ref[0], cqn),
              jnp.dot(ckvn, wuk_ref[0], preferred_element_type=F32), _nt_dot(wuvt_ref[0], ckvn))
             for cqn, ckvn in latents]

    half = QK_ROPE // 2
    z_lo = jnp.zeros((ROPE_LO, QKV_SUB), F32)
    z_hi = jnp.zeros((HEAD_PAD - ROPE_HI, QKV_SUB), F32)
    for tok, z, (qt, k, vt) in zip(toks, zs, prods):
        ang = invf_ref[...] * pos_ref[0, :, tok].astype(F32)
        cost = jnp.cos(ang)
        sint = jnp.sin(ang)
        for hd in range(N_HEADS):
            src = hd * ROPE_HI
            base = hd * HEAD_PAD
            qt_ref[0, base:base + ROPE_LO, tok] = (qt[src:src + ROPE_LO] * scale).astype(BF16)
            x1 = qt[src + ROPE_LO:src + ROPE_LO + half]
            x2 = qt[src + ROPE_LO + half:src + ROPE_HI]
            roped = jnp.concatenate([x1 * cost[:half] - x2 * sint[:half],
                                     x2 * cost[half:] + x1 * sint[half:]], axis=0)
            qt_ref[0, base + ROPE_LO:base + ROPE_HI, tok] = (roped * scale).astype(BF16)
            qt_ref[0, base + ROPE_HI:base + HEAD_PAD, tok] = jnp.zeros((HEAD_PAD - ROPE_HI, QKV_SUB), BF16)

        krx = z[:, Q_LORA + KV_LORA:]
        kr_rot = pltpu.roll(krx, HEAD_PAD - QK_ROPE, axis=1)
        cos_tok = jnp.concatenate([z_lo, cost, z_hi], axis=0).T
        sin_tok = jnp.concatenate([z_lo, sint, z_hi], axis=0).T
        kr_full = krx * cos_tok + kr_rot * sin_tok
        nope_lane = lax.broadcasted_iota(jnp.int32, (QKV_SUB, HEAD_PAD), 1) < QK_NOPE
        for pair in range(N_HEADS // 2):
            two = k[:, pair * HEAD_PAD:(pair + 1) * HEAD_PAD]
            for odd, src in ((0, two), (1, pltpu.roll(two, QK_NOPE, axis=1))):
                base = (2 * pair + odd) * HEAD_PAD
                k_ref[0, tok, base:base + HEAD_PAD] = jnp.where(nope_lane, src, kr_full).astype(BF16)

        vt_ref[0, :, tok] = vt.astype(BF16)


def _qkv(x, mod, positions, wts, layer):
    b, s, d = x.shape
    tm = TM_QKV
    inv_freq = ROPE_THETA ** (-jnp.arange(0, QK_ROPE, 2, dtype=F32) / QK_ROPE)
    invf_col = jnp.concatenate([inv_freq, inv_freq]).reshape(QK_ROPE, 1)
    scale = math.log2(math.e) / math.sqrt(QK_NOPE + QK_ROPE)
    stacked = [wts["ln1_g"], wts["w_in_a"], wts["q_norm_g"], wts["kv_norm_g"], wts["wuqt"],
               wts["wuk"], wts["wuvt"]]
    return pl.pallas_call(
        functools.partial(_qkv_kernel, scale=scale, layer=layer),
        grid=(b, s // tm),
        in_specs=[
            pl.BlockSpec((1, tm, d), lambda i, j: (i, j, 0)),
            _layer_spec(mod, layer),
            _whole_spec(wts["ln1_g"]),
            _layer_spec(wts["w_in_a"], layer),
            _whole_spec(wts["q_norm_g"]),
            _whole_spec(wts["kv_norm_g"]),
            _layer_spec(wts["wuqt"], layer),
            _layer_spec(wts["wuk"], layer),
            _layer_spec(wts["wuvt"], layer),
            pl.BlockSpec((1, 1, tm), lambda i, j: (i, 0, j)),
            _whole_spec(invf_col),
        ],
        out_specs=[
            pl.BlockSpec((1, N_HEADS * HEAD_PAD, tm), lambda i, j: (i, 0, j)),
            pl.BlockSpec((1, tm, N_HEADS * HEAD_PAD), lambda i, j: (i, j, 0)),
            pl.BlockSpec((1, ATTN_DIM, tm), lambda i, j: (i, 0, j)),
        ],
        out_shape=[
            jax.ShapeDtypeStruct((b, N_HEADS * HEAD_PAD, s), BF16),
            jax.ShapeDtypeStruct((b, s, N_HEADS * HEAD_PAD), BF16),
            jax.ShapeDtypeStruct((b, ATTN_DIM, s), BF16),
        ],
        compiler_params=pltpu.CompilerParams(
            dimension_semantics=("arbitrary", "arbitrary"),
            vmem_limit_bytes=VMEM_LIMIT_BYTES),
        name="qkv_proj",
    )(x, mod, *stacked, positions.reshape(b, 1, s), invf_col)


def _attn_kernel(qt_ref, k_ref, vt_ref, *refs, n_q, n_cast):
    cast_in, o_ref, cast_out = refs[:n_cast], refs[n_cast], refs[n_cast + 1:2 * n_cast + 1]
    s_ref, acc_ref, m_ref = refs[2 * n_cast + 1:]

    n_below = n_q * (n_q - 1) // 2
    hq = TQ // 2
    ones_rows = jnp.ones((ACC_ROWS - V_DIM, TK), BF16)
    half_mask = (lax.broadcasted_iota(jnp.int32, (hq, hq), 1)
                 >= lax.broadcasted_iota(jnp.int32, (hq, hq), 0))

    def keys(hd, off, n):
        return k_ref[0, pl.ds(off, n), hd * HEAD_PAD:(hd + 1) * HEAD_PAD]

    def queries(hd, off, n):
        return qt_ref[0, hd * HEAD_PAD:(hd + 1) * HEAD_PAD, pl.ds(off, n)]

    def values(hd, off, n):
        return jnp.concatenate(
            [vt_ref[0, hd * V_DIM:(hd + 1) * V_DIM, pl.ds(off, n)], ones_rows[:, :n]], axis=0)

    def col_max(s):
        return jnp.max(s, axis=0, keepdims=True)

    def produce_below(slot, hd, qi, j):
        s = jnp.dot(keys(hd, pl.multiple_of(j * TK, TK), TK), queries(hd, pl.multiple_of(qi * TQ, TQ), TQ),
                    preferred_element_type=F32)
        s_ref[slot, hd] = s
        return col_max(s)

    def consume_below(slot, hd, qi, j, tile_max):
        m = m_ref[qi, hd]
        m_new = jnp.maximum(m, tile_max)
        p = jnp.exp2(s_ref[slot, hd] - m_new).astype(BF16)
        acc_ref[qi, hd] = (jnp.exp2(m - m_new) * acc_ref[qi, hd]
                           + jnp.dot(values(hd, pl.multiple_of(j * TK, TK), TK), p, preferred_element_type=F32))
        m_ref[qi, hd] = m_new

    def produce_diag(slot, hd, d):
        off = d * TQ
        s_ref[slot, hd, 0:hq, :] = jnp.dot(keys(hd, off, hq), queries(hd, off, TQ),
                                           preferred_element_type=F32)
        s_ref[slot, hd, hq:TK, hq:TQ] = jnp.dot(keys(hd, off + hq, hq), queries(hd, off + hq, hq),
                                                preferred_element_type=F32)

    def consume_diag(slot, hd, d):
        off = d * TQ
        neg = jnp.finfo(F32).min
        s_tl = jnp.where(half_mask, s_ref[slot, hd, 0:hq, 0:hq], neg)
        s_tr = s_ref[slot, hd, 0:hq, hq:TQ]
        s_br = jnp.where(half_mask, s_ref[slot, hd, hq:TK, hq:TQ], neg)
        m = m_ref[d, hd]
        m_l = jnp.maximum(m[:, 0:hq], col_max(s_tl))
        m_r = jnp.maximum(m[:, hq:TQ], jnp.maximum(col_max(s_tr), col_max(s_br)))
        p_l = jnp.exp2(s_tl - m_l).astype(BF16)
        p_r = jnp.concatenate([jnp.exp2(s_tr - m_r), jnp.exp2(s_br - m_r)], axis=0).astype(BF16)
        acc = acc_ref[d, hd]
        acc_l = (jnp.exp2(m[:, 0:hq] - m_l) * acc[:, 0:hq]
                 + jnp.dot(values(hd, off, hq), p_l, preferred_element_type=F32))
        acc_r = (jnp.exp2(m[:, hq:TQ] - m_r) * acc[:, hq:TQ]
                 + jnp.dot(values(hd, off, TK), p_r, preferred_element_type=F32))
        rows = slice(hd * V_DIM, (hd + 1) * V_DIM)
        o_ref[0, rows, pl.ds(off, hq)] = (
            acc_l[0:V_DIM] * (1.0 / acc_l[V_DIM:V_DIM + 1])).astype(o_ref.dtype)
        o_ref[0, rows, pl.ds(off + hq, hq)] = (
            acc_r[0:V_DIM] * (1.0 / acc_r[V_DIM:V_DIM + 1])).astype(o_ref.dtype)

    heads = range(ATTN_HEADS)

    def next_below(qi, j):
        row_end = j + 1 == qi
        return jnp.where(row_end, qi + 1, qi), jnp.where(row_end, 0, j + 1)

    def below_step(slot, tile, maxes, to_diag=False):
        nxt = next_below(*tile)
        new_maxes = []
        for hd in heads:
            if to_diag:
                produce_diag(1 - slot, hd, 0)
            else:
                new_maxes.append(produce_below(1 - slot, hd, *nxt))
            consume_below(slot, hd, *tile, maxes[hd])
        return nxt, tuple(new_maxes)

    def diag_step(slot, d, produce_next=True):
        for hd in heads:
            if produce_next:
                produce_diag(1 - slot, hd, d + 1)
            consume_diag(slot, hd, d)
        return d + 1

    tile0 = (jnp.int32(1), jnp.int32(0))
    maxes0 = tuple(produce_below(0, hd, *tile0) for hd in heads)

    acc_ref[...] = jnp.zeros(acc_ref.shape, F32)
    m_ref[...] = jnp.full(m_ref.shape, jnp.finfo(F32).min, F32)
    for src, dst in zip(cast_in, cast_out):
        dst[...] = src[...].astype(BF16)

    def below_pair(carry):
        tile, maxes = carry
        tile, maxes = below_step(0, tile, maxes)
        return below_step(1, tile, maxes)

    def below_trip(_, carry):
        for _ in range(BELOW_PAIRS_PER_TRIP):
            carry = below_pair(carry)
        return carry

    n_pairs = n_below // 2 - 1
    carry = lax.fori_loop(0, n_pairs // BELOW_PAIRS_PER_TRIP, below_trip, (tile0, maxes0))
    for _ in range(n_pairs % BELOW_PAIRS_PER_TRIP):
        carry = below_pair(carry)
    tile, maxes = below_step(0, *carry)
    below_step(1, tile, maxes, to_diag=True)

    for d in range(n_q):
        diag_step(d % 2, d, produce_next=d + 1 < n_q)


def _attention(qt, k, vt, f32_weights, layer):
    b, _, s = qt.shape
    hb = ATTN_HEADS
    groups = N_HEADS // hb
    n_steps = b * groups
    n_q = s // TQ
    assert n_q % 2 == 0 and (n_q * (n_q - 1) // 2) % 2 == 0, "both tile streams are consumed in pairs of tiles"

    def rows_per_step(w):
        rows = w.shape[1] // n_steps
        assert rows * n_steps == w.shape[1] and rows % BF16_SUBLANES == 0, w.shape
        return rows

    cast_in = [pl.BlockSpec((1, rows_per_step(w), w.shape[2]), lambda i, h: (layer, i * groups + h, 0))
               for w in f32_weights]
    cast_out = [pl.BlockSpec((1, rows_per_step(w), w.shape[2]), lambda i, h: (0, i * groups + h, 0))
                for w in f32_weights]
    out = pl.pallas_call(
        functools.partial(_attn_kernel, n_q=n_q, n_cast=len(f32_weights)),
        grid=(b, groups),
        in_specs=[
            pl.BlockSpec((1, hb * HEAD_PAD, s), lambda i, h: (i, h, 0)),
            pl.BlockSpec((1, s, hb * HEAD_PAD), lambda i, h: (i, 0, h)),
            pl.BlockSpec((1, hb * V_DIM, s), lambda i, h: (i, h, 0)),
        ] + cast_in,
        out_specs=[pl.BlockSpec((1, hb * V_DIM, s), lambda i, h: (i, h, 0))] + cast_out,
        out_shape=[jax.ShapeDtypeStruct((b, ATTN_DIM, s), BF16)]
        + [jax.ShapeDtypeStruct((1,) + w.shape[1:], BF16) for w in f32_weights],
        scratch_shapes=[
            pltpu.VMEM((2, hb, TK, TQ), F32),
            pltpu.VMEM((n_q, hb, ACC_ROWS, TQ), F32),
            pltpu.VMEM((n_q, hb, 1, TQ), F32),
        ],
        compiler_params=pltpu.CompilerParams(
            dimension_semantics=("arbitrary", "arbitrary"),
            vmem_limit_bytes=VMEM_LIMIT_BYTES),
        name="mla_attention",
    )(qt, k, vt, *f32_weights)
    return out[0], out[1:]


def _mix_kernel(x_ref, ot_ref, mod_ref, g1_ref, g2_ref, gf_ref, w_in_ref, w_pool_ref, pscale_ref,
                p_pool_ref, p_attn_ref, w_out_ref, w_ff1_ref, w_ff2_ref,
                o_ref, uext_ref, *, final, layer):
    tm = x_ref.shape[1]
    si = pl.program_id(1)

    @pl.when(si == 0)
    def _():
        uext_ref[0:POOL_HALO, :] = jnp.zeros((POOL_HALO, POOL_DIM), F32)

    @pl.when(si > 0)
    def _():
        uext_ref[0:POOL_HALO, :] = uext_ref[tm:tm + POOL_HALO, :]

    shift1, scale1, gate1, shift2, scale2, gate2 = _mod_rows(mod_ref, range(N_MOD))

    for r0 in range(0, tm, MIX_SUB):
        tok = slice(r0, r0 + MIX_SUB)
        x = x_ref[0, tok, :]

        y_b = _tn_dot(ot_ref[0, :, tok], p_attn_ref[0])
        h = ((_rms(x) * g1_ref[layer:layer + 1, :]) * (1.0 + scale1) + shift1).astype(BF16)
        u = jnp.dot(h, w_in_ref[0, :, 0:POOL_DIM], preferred_element_type=F32)
        gz = jnp.dot(h, w_in_ref[0, :, POOL_DIM:], preferred_element_type=F32)
        gz_a = gz[:, 0:D_MODEL]
        gz_b = gz[:, D_MODEL:]

        uext_ref[POOL_HALO + r0:POOL_HALO + r0 + MIX_SUB, :] = u
        head_pos = si * tm + r0 + lax.broadcasted_iota(jnp.int32, (POOL_HALO, 1), 0)
        pooled = []
        for g, w in enumerate(POOL_WINDOWS):
            eg = uext_ref[r0:r0 + POOL_HALO + MIX_SUB,
                          g * POOL_GROUP_DIM:(g + 1) * POOL_GROUP_DIM]
            win = eg
            k = 1
            while k < w:
                win = win + pltpu.roll(win, k, axis=0)
                k *= 2
            win = win[POOL_HALO:]
            ug = eg[POOL_HALO:]
            inv_head = 1.0 / jnp.minimum(head_pos + 1, w).astype(F32)
            mean = jnp.concatenate([win[:POOL_HALO] * inv_head, win[POOL_HALO:] * (1.0 / w)], axis=0)
            yg = jnp.dot((mean - ug).astype(BF16), w_pool_ref[0, g], preferred_element_type=F32)
            pooled.append(yg)
        y_pool = jnp.concatenate(pooled, axis=-1) * pscale_ref[layer:layer + 1, :]
        y_a = jnp.dot(y_pool.astype(BF16), p_pool_ref[0], preferred_element_type=F32)

        merged = _sigmoid(gz_a) * y_a + _sigmoid(gz_b) * y_b
        x1 = x + gate1 * jnp.dot(merged.astype(BF16), w_out_ref[0], preferred_element_type=F32)

        h2 = ((_rms(x1) * g2_ref[layer:layer + 1, :]) * (1.0 + scale2) + shift2).astype(BF16)
        hidden = [jnp.square(jnp.maximum(
            jnp.dot(h2, w_ff1_ref[0, :, c0:c0 + FF_CHUNK], preferred_element_type=F32), 0.0)).astype(BF16)
            for c0 in range(0, D_FF, FF_CHUNK)]
        ff = jnp.dot(jnp.concatenate(hidden, axis=-1), w_ff2_ref[0], preferred_element_type=F32)
        x2 = x1 + gate2 * ff
        if final:
            x2 = _rms(x2) * gf_ref[...]
        o_ref[0, tok, :] = x2


def _mix(x, ot, mod, wts, layer_wts, layer, *, final):
    b, s, d = x.shape
    tm = TM_MIX
    once = dict(pipeline_mode=pl.Buffered(1))
    names = ["ln1_g", "ln2_g", "final_g", "w_in_b", "w_pool", "pool_scale", "p_pool", "p_attn", "w_out",
             "w_ff1", "w_ff2"]
    whole = {"ln1_g", "ln2_g", "final_g", "pool_scale"}
    wts = {**wts, **layer_wts}
    specs = [_whole_spec(wts[n], **once) if n in whole
             else _layer_spec(wts[n], 0 if n in layer_wts else layer, **once)
             for n in names]
    return pl.pallas_call(
        functools.partial(_mix_kernel, final=final, layer=layer),
        grid=(b, s // tm),
        in_specs=[
            pl.BlockSpec((1, tm, d), lambda i, j: (i, j, 0)),
            pl.BlockSpec((1, ATTN_DIM, tm), lambda i, j: (i, 0, j)),
            _layer_spec(mod, layer, **once),
        ] + specs,
        out_specs=pl.BlockSpec((1, tm, d), lambda i, j: (i, j, 0)),
        out_shape=jax.ShapeDtypeStruct((b, s, d), F32),
        scratch_shapes=[pltpu.VMEM((tm + POOL_HALO, POOL_DIM), F32)],
        compiler_params=pltpu.CompilerParams(
            dimension_semantics=("arbitrary", "arbitrary"),
            vmem_limit_bytes=VMEM_LIMIT_BYTES),
        name="mix_mlp",
    )(x, ot, mod, *[wts[n] for n in names])


W_IN_PREP_ROWS = 256


def _w_in_prep_kernel(w_ref, a_ref, b_ref):
    wt = w_ref[0]
    rows = wt.shape[1]
    c0 = POOL_DIM
    c2 = c0 + Q_LORA + KV_LORA
    c3 = c2 + QK_ROPE
    half = QK_ROPE // 2
    kr = wt[c2:c3]
    zl = jnp.zeros((ROPE_LO, rows), F32)
    a_t = jnp.concatenate([wt[c0:c2], zl, kr, -kr[half:], kr[:half]], axis=0)
    a_ref[0] = a_t.T.astype(BF16)
    b_ref[0] = jnp.concatenate([wt[:c0], wt[c3:]], axis=0).T.astype(BF16)


def _w_in_prep(w_in):
    depth, d, n = w_in.shape
    na = Q_LORA + KV_LORA + HEAD_PAD
    nb = n - (Q_LORA + KV_LORA + QK_ROPE)
    rows = W_IN_PREP_ROWS
    w_in = jnp.swapaxes(w_in, 1, 2)
    return pl.pallas_call(
        _w_in_prep_kernel,
        grid=(depth, d // rows),
        in_specs=[pl.BlockSpec((1, n, rows), lambda l, r: (l, 0, r))],
        out_specs=[pl.BlockSpec((1, rows, na), lambda l, r: (l, r, 0)),
                   pl.BlockSpec((1, rows, nb), lambda l, r: (l, r, 0))],
        out_shape=[jax.ShapeDtypeStruct((depth, d, na), BF16), jax.ShapeDtypeStruct((depth, d, nb), BF16)],
        compiler_params=pltpu.CompilerParams(vmem_limit_bytes=VMEM_LIMIT_BYTES),
        name="w_in_prep",
    )(w_in)


def _prep_weights(ln1_g, ln2_g, w_in, q_norm_g, w_uq, kv_norm_g, w_uk, w_uv, w_pool, pool_scale, final_g):
    depth = w_in.shape[0]
    w_in_a, w_in_b = _w_in_prep(w_in)

    wuqt = w_uq.reshape(depth, Q_LORA, N_HEADS * ROPE_HI).transpose(0, 2, 1).astype(BF16)
    wuk = w_uk.reshape(depth, KV_LORA, N_HEADS * QK_NOPE).astype(BF16)
    wuvt = w_uv.reshape(depth, KV_LORA, ATTN_DIM).transpose(0, 2, 1).astype(BF16)
    return dict(
        ln1_g=ln1_g, ln2_g=ln2_g, final_g=final_g.reshape(1, D_MODEL), q_norm_g=q_norm_g,
        kv_norm_g=kv_norm_g, pool_scale=pool_scale, w_in_a=w_in_a, w_in_b=w_in_b, wuqt=wuqt,
        wuk=wuk, wuvt=wuvt, w_pool=w_pool.astype(BF16))


def kernel(x, c, positions, ln1_g, ln2_g, w_ada, b_ada, w_in, q_norm_g, w_uq, kv_norm_g, w_uk,
           w_uv, w_pool, pool_scale, p_pool, p_attn, w_out, w_ff1, w_ff2, final_g):
    depth = w_in.shape[0]
    mod = _modulation(c, w_ada, b_ada)
    wts = _prep_weights(ln1_g, ln2_g, w_in, q_norm_g, w_uq, kv_norm_g, w_uk, w_uv, w_pool, pool_scale,
                        final_g)
    cast_names = ("p_pool", "p_attn", "w_out", "w_ff1", "w_ff2")
    f32_weights = (p_pool, p_attn, w_out, w_ff1, w_ff2)
    for layer in range(depth):
        qt, k, vt = _qkv(x, mod, positions, wts, layer)
        ot, cast = _attention(qt, k, vt, f32_weights, layer)
        x = _mix(x, ot, mod, wts, dict(zip(cast_names, cast)), layer, final=(layer == depth - 1))
    return x
```

```python
import functools
import math

import jax
import jax.numpy as jnp
from jax import lax
from jax.experimental import pallas as pl
from jax.experimental.pallas import tpu as pltpu

D_MODEL = 1024
N_HEADS = 8
QK_NOPE = 64
QK_ROPE = 32
V_DIM = 64
Q_LORA = 384
KV_LORA = 256
POOL_WINDOWS = (2, 4, 8, 16)
POOL_GROUP_DIM = 128
POOL_DIM = len(POOL_WINDOWS) * POOL_GROUP_DIM
ATTN_DIM = N_HEADS * V_DIM
D_FF = 4 * D_MODEL
N_MOD = 6
EPS = 1e-6
ROPE_THETA = 10000.0

HEAD_PAD = 128
ROPE_LO = QK_NOPE
ROPE_HI = QK_NOPE + QK_ROPE
POOL_HALO = 16

VMEM_LIMIT_BYTES = 56 * 1024 * 1024

F32 = jnp.float32
BF16 = jnp.bfloat16

TM_QKV = 1024
QKV_SUB = 256
TM_MIX = 1024
MIX_SUB = 512
TQ = 512
TK = 512
ATTN_HEADS = 4
BELOW_PAIRS_PER_TRIP = 4
F32_SUBLANES = 8
BF16_SUBLANES = 16
ACC_ROWS = V_DIM + BF16_SUBLANES
FF_CHUNK = 1024
MOD_TN = 1536


def _nt_dot(a, b):
    return lax.dot_general(a, b, (((1,), (1,)), ((), ())), preferred_element_type=F32)


def _tn_dot(a, b):
    return lax.dot_general(a, b, (((0,), (0,)), ((), ())), preferred_element_type=F32)


def _rms(x):
    return x * lax.rsqrt(jnp.mean(x * x, axis=-1, keepdims=True) + EPS)


def _sigmoid(x):
    return 1.0 / (1.0 + jnp.exp(-x))


def _mod_rows(mod_ref, chunks):
    row = pl.ds(pl.program_id(0), 1)
    return tuple(mod_ref[0, row, c * D_MODEL:(c + 1) * D_MODEL] for c in chunks)


def _layer_spec(arr, layer, **kw):
    tail = (0,) * (arr.ndim - 1)
    return pl.BlockSpec((1,) + arr.shape[1:], lambda i, j: (layer,) + tail, **kw)


def _whole_spec(arr, **kw):
    zeros = (0,) * arr.ndim
    return pl.BlockSpec(arr.shape, lambda i, j: zeros, **kw)


def _mod_kernel(c_ref, w_ref, b_ref, o_ref):
    c = c_ref[...]
    c_act = c * _sigmoid(c)
    o_ref[0] = jnp.dot(c_act.astype(BF16), w_ref[0].astype(BF16),
                       preferred_element_type=F32) + b_ref[0]


def _modulation(c, w_ada, b_ada):
    depth, d, n = w_ada.shape
    b = c.shape[0]
    rows = F32_SUBLANES
    assert b <= rows
    c_pad = jnp.pad(c, ((0, rows - b), (0, 0)))
    out = pl.pallas_call(
        _mod_kernel,
        grid=(depth, n // MOD_TN),
        in_specs=[
            pl.BlockSpec((rows, d), lambda l, j: (0, 0)),
            pl.BlockSpec((1, d, MOD_TN), lambda l, j: (l, 0, j)),
            pl.BlockSpec((1, 1, MOD_TN), lambda l, j: (l, 0, j)),
        ],
        out_specs=pl.BlockSpec((1, rows, MOD_TN), lambda l, j: (l, 0, j)),
        out_shape=jax.ShapeDtypeStruct((depth, rows, n), F32),
        compiler_params=pltpu.CompilerParams(vmem_limit_bytes=VMEM_LIMIT_BYTES),
        name="adaln_mod",
    )(c_pad, w_ada, b_ada.reshape(depth, 1, n))
    return out


def _qkv_kernel(x_ref, mod_ref, g_ref, w_in_ref, gq_ref, gkv_ref, wuqt_ref,
                wuk_ref, wuvt_ref, ta_ref, tb_ref,
                qt_ref, k_ref, vt_ref, *table_out, scale, layer):
    shift, scl = _mod_rows(mod_ref, (0, 1))
    toks = [slice(r0, r0 + QKV_SUB) for r0 in range(0, x_ref.shape[1], QKV_SUB)]

    hs = [((_rms(x_ref[0, tok, :]) * g_ref[layer:layer + 1, :]) * (1.0 + scl) + shift).astype(BF16)
          for tok in toks]
    zs = [jnp.dot(h, w_in_ref[0], preferred_element_type=F32) for h in hs]

    latents = []
    for z in zs:
        c_q = z[:, 0:Q_LORA]
        c_kv = z[:, Q_LORA:Q_LORA + KV_LORA]
        latents.append(((_rms(c_q) * gq_ref[layer:layer + 1, :]).astype(BF16),
                        (_rms(c_kv) * gkv_ref[layer:layer + 1, :]).astype(BF16)))

    prods = [(_nt_dot(wuqt_ref[0], cqn),
              jnp.dot(ckvn, wuk_ref[0], preferred_element_type=F32), _nt_dot(wuvt_ref[0], ckvn))
             for cqn, ckvn in latents]

    half = QK_ROPE // 2
    z_lo = jnp.zeros((ROPE_LO, QKV_SUB), F32)
    z_hi = jnp.zeros((HEAD_PAD - ROPE_HI, QKV_SUB), F32)
    for tok, z, (qt, k, vt) in zip(toks, zs, prods):
        if table_out:
            ang = tb_ref[...] * ta_ref[0, :, tok].astype(F32)
            cost = jnp.cos(ang)
            sint = jnp.sin(ang)
            table_out[0][0, :, tok] = cost
            table_out[1][0, :, tok] = sint
        else:
            cost = ta_ref[0, :, tok]
            sint = tb_ref[0, :, tok]
        for hd in range(N_HEADS):
            src = hd * ROPE_HI
            base = hd * HEAD_PAD
            qt_ref[0, base:base + ROPE_LO, tok] = (qt[src:src + ROPE_LO] * scale).astype(BF16)
            x1 = qt[src + ROPE_LO:src + ROPE_LO + half]
            x2 = qt[src + ROPE_LO + half:src + ROPE_HI]
            roped = jnp.concatenate([x1 * cost[:half] - x2 * sint[:half],
                                     x2 * cost[half:] + x1 * sint[half:]], axis=0)
            qt_ref[0, base + ROPE_LO:base + ROPE_HI, tok] = (roped * scale).astype(BF16)
            qt_ref[0, base + ROPE_HI:base + HEAD_PAD, tok] = jnp.zeros((HEAD_PAD - ROPE_HI, QKV_SUB), BF16)

        krx = z[:, Q_LORA + KV_LORA:]
        kr_rot = pltpu.roll(krx, HEAD_PAD - QK_ROPE, axis=1)
        cos_tok = jnp.concatenate([z_lo, cost, z_hi], axis=0).T
        sin_tok = jnp.concatenate([z_lo, sint, z_hi], axis=0).T
        kr_full = krx * cos_tok + kr_rot * sin_tok
        nope_lane = lax.broadcasted_iota(jnp.int32, (QKV_SUB, HEAD_PAD), 1) < QK_NOPE
        for pair in range(N_HEADS // 2):
            two = k[:, pair * HEAD_PAD:(pair + 1) * HEAD_PAD]
            for odd, src in ((0, two), (1, pltpu.roll(two, QK_NOPE, axis=1))):
                base = (2 * pair + odd) * HEAD_PAD
                k_ref[0, tok, base:base + HEAD_PAD] = jnp.where(nope_lane, src, kr_full).astype(BF16)

        vt_ref[0, :, tok] = vt.astype(BF16)


def _qkv(x, mod, positions, wts, layer, tables=None):
    b, s, d = x.shape
    tm = TM_QKV
    table_spec = pl.BlockSpec((1, QK_ROPE, tm), lambda i, j: (i, 0, j))
    if tables is None:
        inv_freq = ROPE_THETA ** (-jnp.arange(0, QK_ROPE, 2, dtype=F32) / QK_ROPE)
        invf_col = jnp.concatenate([inv_freq, inv_freq]).reshape(QK_ROPE, 1)
        table_src = (positions.reshape(b, 1, s), invf_col)
        table_src_specs = [pl.BlockSpec((1, 1, tm), lambda i, j: (i, 0, j)), _whole_spec(invf_col)]
        extra_specs = [table_spec, table_spec]
        extra_shapes = [jax.ShapeDtypeStruct((b, QK_ROPE, s), F32)] * 2
    else:
        table_src, table_src_specs, extra_specs, extra_shapes = tuple(tables), [table_spec, table_spec], [], []
    scale = math.log2(math.e) / math.sqrt(QK_NOPE + QK_ROPE)
    stacked = [wts["ln1_g"], wts["w_in_a"], wts["q_norm_g"], wts["kv_norm_g"], wts["wuqt"],
               wts["wuk"], wts["wuvt"]]
    out = pl.pallas_call(
        functools.partial(_qkv_kernel, scale=scale, layer=layer),
        grid=(b, s // tm),
        in_specs=[
            pl.BlockSpec((1, tm, d), lambda i, j: (i, j, 0)),
            _layer_spec(mod, layer),
            _whole_spec(wts["ln1_g"]),
            _layer_spec(wts["w_in_a"], layer),
            _whole_spec(wts["q_norm_g"]),
            _whole_spec(wts["kv_norm_g"]),
            _layer_spec(wts["wuqt"], layer),
            _layer_spec(wts["wuk"], layer),
            _layer_spec(wts["wuvt"], layer),
        ] + table_src_specs,
        out_specs=[
            pl.BlockSpec((1, N_HEADS * HEAD_PAD, tm), lambda i, j: (i, 0, j)),
            pl.BlockSpec((1, tm, N_HEADS * HEAD_PAD), lambda i, j: (i, j, 0)),
            pl.BlockSpec((1, ATTN_DIM, tm), lambda i, j: (i, 0, j)),
        ] + extra_specs,
        out_shape=[
            jax.ShapeDtypeStruct((b, N_HEADS * HEAD_PAD, s), BF16),
            jax.ShapeDtypeStruct((b, s, N_HEADS * HEAD_PAD), BF16),
            jax.ShapeDtypeStruct((b, ATTN_DIM, s), BF16),
        ] + extra_shapes,
        compiler_params=pltpu.CompilerParams(
            dimension_semantics=("arbitrary", "arbitrary"),
            vmem_limit_bytes=VMEM_LIMIT_BYTES),
        name="qkv_proj",
    )(x, mod, *stacked, *table_src)
    return out[0], out[1], out[2], (tuple(out[3:]) if tables is None else tuple(tables))


def _attn_kernel(qt_ref, k_ref, vt_ref, *refs, n_q, n_cast):
    cast_in, o_ref, cast_out = refs[:n_cast], refs[n_cast], refs[n_cast + 1:2 * n_cast + 1]
    s_ref, acc_ref, m_ref = refs[2 * n_cast + 1:]

    n_below = n_q * (n_q - 1) // 2
    hq = TQ // 2
    ones_rows = jnp.ones((ACC_ROWS - V_DIM, TK), BF16)
    half_mask = (lax.broadcasted_iota(jnp.int32, (hq, hq), 1)
                 >= lax.broadcasted_iota(jnp.int32, (hq, hq), 0))

    def keys(hd, off, n):
        return k_ref[0, pl.ds(off, n), hd * HEAD_PAD:(hd + 1) * HEAD_PAD]

    def queries(hd, off, n):
        return qt_ref[0, hd * HEAD_PAD:(hd + 1) * HEAD_PAD, pl.ds(off, n)]

    def values(hd, off, n):
        return jnp.concatenate(
            [vt_ref[0, hd * V_DIM:(hd + 1) * V_DIM, pl.ds(off, n)], ones_rows[:, :n]], axis=0)

    def col_max(s):
        return jnp.max(s, axis=0, keepdims=True)

    def produce_below(slot, hd, qi, j):
        s = jnp.dot(keys(hd, pl.multiple_of(j * TK, TK), TK), queries(hd, pl.multiple_of(qi * TQ, TQ), TQ),
                    preferred_element_type=F32)
        s_ref[slot, hd] = s
        return col_max(s)

    def consume_below(slot, hd, qi, j, tile_max):
        m = m_ref[qi, hd]
        m_new = jnp.maximum(m, tile_max)
        p = jnp.exp2(s_ref[slot, hd] - m_new).astype(BF16)
        acc_ref[qi, hd] = (jnp.exp2(m - m_new) * acc_ref[qi, hd]
                           + jnp.dot(values(hd, pl.multiple_of(j * TK, TK), TK), p, preferred_element_type=F32))
        m_ref[qi, hd] = m_new

    def produce_diag(slot, hd, d):
        off = d * TQ
        s_ref[slot, hd, 0:hq, :] = jnp.dot(keys(hd, off, hq), queries(hd, off, TQ),
                                           preferred_element_type=F32)
        s_ref[slot, hd, hq:TK, hq:TQ] = jnp.dot(keys(hd, off + hq, hq), queries(hd, off + hq, hq),
                                                preferred_element_type=F32)

    def consume_diag(slot, hd, d):
        off = d * TQ
        neg = jnp.finfo(F32).min
        s_tl = jnp.where(half_mask, s_ref[slot, hd, 0:hq, 0:hq], neg)
        s_tr = s_ref[slot, hd, 0:hq, hq:TQ]
        s_br = jnp.where(half_mask, s_ref[slot, hd, hq:TK, hq:TQ], neg)
        m = m_ref[d, hd]
        m_l = jnp.maximum(m[:, 0:hq], col_max(s_tl))
        m_r = jnp.maximum(m[:, hq:TQ], jnp.maximum(col_max(s_tr), col_max(s_br)))
        p_l = jnp.exp2(s_tl - m_l).astype(BF16)
        p_r = jnp.concatenate([jnp.exp2(s_tr - m_r), jnp.exp2(s_br - m_r)], axis=0).astype(BF16)
        acc = acc_ref[d, hd]
        acc_l = (jnp.exp2(m[:, 0:hq] - m_l) * acc[:, 0:hq]
                 + jnp.dot(values(hd, off, hq), p_l, preferred_element_type=F32))
        acc_r = (jnp.exp2(m[:, hq:TQ] - m_r) * acc[:, hq:TQ]
                 + jnp.dot(values(hd, off, TK), p_r, preferred_element_type=F32))
        rows = slice(hd * V_DIM, (hd + 1) * V_DIM)
        o_ref[0, rows, pl.ds(off, hq)] = (
            acc_l[0:V_DIM] * (1.0 / acc_l[V_DIM:V_DIM + 1])).astype(o_ref.dtype)
        o_ref[0, rows, pl.ds(off + hq, hq)] = (
            acc_r[0:V_DIM] * (1.0 / acc_r[V_DIM:V_DIM + 1])).astype(o_ref.dtype)

    heads = range(ATTN_HEADS)

    def next_below(qi, j):
        row_end = j + 1 == qi
        return jnp.where(row_end, qi + 1, qi), jnp.where(row_end, 0, j + 1)

    def below_step(slot, tile, maxes, to_diag=False):
        nxt = next_below(*tile)
        new_maxes = []
        for hd in heads:
            if to_diag:
                produce_diag(1 - slot, hd, 0)
            else:
                new_maxes.append(produce_below(1 - slot, hd, *nxt))
            consume_below(slot, hd, *tile, maxes[hd])
        return nxt, tuple(new_maxes)

    def diag_step(slot, d, produce_next=True):
        for hd in heads:
            if produce_next:
                produce_diag(1 - slot, hd, d + 1)
            consume_diag(slot, hd, d)
        return d + 1

    tile0 = (jnp.int32(1), jnp.int32(0))
    maxes0 = tuple(produce_below(0, hd, *tile0) for hd in heads)

    acc_ref[...] = jnp.zeros(acc_ref.shape, F32)
    m_ref[...] = jnp.full(m_ref.shape, jnp.finfo(F32).min, F32)
    for src, dst in zip(cast_in, cast_out):
        dst[...] = src[...].astype(BF16)

    def below_pair(carry):
        tile, maxes = carry
        tile, maxes = below_step(0, tile, maxes)
        return below_step(1, tile, maxes)

    def below_trip(_, carry):
        for _ in range(BELOW_PAIRS_PER_TRIP):
            carry = below_pair(carry)
        return carry

    n_pairs = n_below // 2 - 1
    carry = lax.fori_loop(0, n_pairs // BELOW_PAIRS_PER_TRIP, below_trip, (tile0, maxes0))
    for _ in range(n_pairs % BELOW_PAIRS_PER_TRIP):
        carry = below_pair(carry)
    tile, maxes = below_step(0, *carry)
    below_step(1, tile, maxes, to_diag=True)

    for d in range(n_q):
        diag_step(d % 2, d, produce_next=d + 1 < n_q)


def _attention(qt, k, vt, f32_weights, layer):
    b, _, s = qt.shape
    hb = ATTN_HEADS
    groups = N_HEADS // hb
    n_steps = b * groups
    n_q = s // TQ
    assert n_q % 2 == 0 and (n_q * (n_q - 1) // 2) % 2 == 0, "both tile streams run two tiles per loop trip"

    def rows_per_step(w):
        rows = w.shape[1] // n_steps
        assert rows * n_steps == w.shape[1] and rows % BF16_SUBLANES == 0, w.shape
        return rows

    cast_in = [pl.BlockSpec((1, rows_per_step(w), w.shape[2]), lambda i, h: (layer, i * groups + h, 0))
               for w in f32_weights]
    cast_out = [pl.BlockSpec((1, rows_per_step(w), w.shape[2]), lambda i, h: (0, i * groups + h, 0))
                for w in f32_weights]
    out = pl.pallas_call(
        functools.partial(_attn_kernel, n_q=n_q, n_cast=len(f32_weights)),
        grid=(b, groups),
        in_specs=[
            pl.BlockSpec((1, hb * HEAD_PAD, s), lambda i, h: (i, h, 0)),
            pl.BlockSpec((1, s, hb * HEAD_PAD), lambda i, h: (i, 0, h)),
            pl.BlockSpec((1, hb * V_DIM, s), lambda i, h: (i, h, 0)),
        ] + cast_in,
        out_specs=[pl.BlockSpec((1, hb * V_DIM, s), lambda i, h: (i, h, 0))] + cast_out,
        out_shape=[jax.ShapeDtypeStruct((b, ATTN_DIM, s), BF16)]
        + [jax.ShapeDtypeStruct((1,) + w.shape[1:], BF16) for w in f32_weights],
        scratch_shapes=[
            pltpu.VMEM((2, hb, TK, TQ), F32),
            pltpu.VMEM((n_q, hb, ACC_ROWS, TQ), F32),
            pltpu.VMEM((n_q, hb, 1, TQ), F32),
        ],
        compiler_params=pltpu.CompilerParams(
            dimension_semantics=("arbitrary", "arbitrary"),
            vmem_limit_bytes=VMEM_LIMIT_BYTES),
        name="mla_attention",
    )(qt, k, vt, *f32_weights)
    return out[0], out[1:]


def _mix_kernel(x_ref, ot_ref, mod_ref, g1_ref, g2_ref, gf_ref, w_in_ref, w_pool_ref, pscale_ref,
                p_pool_ref, p_attn_ref, w_out_ref, w_ff1_ref, w_ff2_ref,
                o_ref, uext_ref, *, final, layer):
    tm = x_ref.shape[1]
    si = pl.program_id(1)

    @pl.when(si == 0)
    def _():
        uext_ref[0:POOL_HALO, :] = jnp.zeros((POOL_HALO, POOL_DIM), F32)

    @pl.when(si > 0)
    def _():
        uext_ref[0:POOL_HALO, :] = uext_ref[tm:tm + POOL_HALO, :]

    shift1, scale1, gate1, shift2, scale2, gate2 = _mod_rows(mod_ref, range(N_MOD))

    for r0 in range(0, tm, MIX_SUB):
        tok = slice(r0, r0 + MIX_SUB)
        x = x_ref[0, tok, :]

        y_b = _tn_dot(ot_ref[0, :, tok], p_attn_ref[0])
        h = ((_rms(x) * g1_ref[layer:layer + 1, :]) * (1.0 + scale1) + shift1).astype(BF16)
        u = jnp.dot(h, w_in_ref[0, :, 0:POOL_DIM], preferred_element_type=F32)
        gz = jnp.dot(h, w_in_ref[0, :, POOL_DIM:], preferred_element_type=F32)
        gz_a = gz[:, 0:D_MODEL]
        gz_b = gz[:, D_MODEL:]

        uext_ref[POOL_HALO + r0:POOL_HALO + r0 + MIX_SUB, :] = u
        head_pos = si * tm + r0 + lax.broadcasted_iota(jnp.int32, (POOL_HALO, 1), 0)
        pooled = []
        for g, w in enumerate(POOL_WINDOWS):
            eg = uext_ref[r0:r0 + POOL_HALO + MIX_SUB,
                          g * POOL_GROUP_DIM:(g + 1) * POOL_GROUP_DIM]
            win = eg
            k = 1
            while k < w:
                win = win + pltpu.roll(win, k, axis=0)
                k *= 2
            win = win[POOL_HALO:]
            ug = eg[POOL_HALO:]
            inv_head = 1.0 / jnp.minimum(head_pos + 1, w).astype(F32)
            mean = jnp.concatenate([win[:POOL_HALO] * inv_head, win[POOL_HALO:] * (1.0 / w)], axis=0)
            yg = jnp.dot((mean - ug).astype(BF16), w_pool_ref[0, g], preferred_element_type=F32)
            pooled.append(yg)
        y_pool = jnp.concatenate(pooled, axis=-1) * pscale_ref[layer:layer + 1, :]
        y_a = jnp.dot(y_pool.astype(BF16), p_pool_ref[0], preferred_element_type=F32)

        merged = _sigmoid(gz_a) * y_a + _sigmoid(gz_b) * y_b
        x1 = x + gate1 * jnp.dot(merged.astype(BF16), w_out_ref[0], preferred_element_type=F32)

        h2 = ((_rms(x1) * g2_ref[layer:layer + 1, :]) * (1.0 + scale2) + shift2).astype(BF16)
        hidden = [jnp.square(jnp.maximum(
            jnp.dot(h2, w_ff1_ref[0, :, c0:c0 + FF_CHUNK], preferred_element_type=F32), 0.0)).astype(BF16)
            for c0 in range(0, D_FF, FF_CHUNK)]
        ff = jnp.dot(jnp.concatenate(hidden, axis=-1), w_ff2_ref[0], preferred_element_type=F32)
        x2 = x1 + gate2 * ff
        if final:
            x2 = _rms(x2) * gf_ref[...]
        o_ref[0, tok, :] = x2


def _mix(x, ot, mod, wts, layer_wts, layer, *, final):
    b, s, d = x.shape
    tm = TM_MIX
    once = dict(pipeline_mode=pl.Buffered(1))
    names = ["ln1_g", "ln2_g", "final_g", "w_in_b", "w_pool", "pool_scale", "p_pool", "p_attn", "w_out",
             "w_ff1", "w_ff2"]
    whole = {"ln1_g", "ln2_g", "final_g", "pool_scale"}
    wts = {**wts, **layer_wts}
    specs = [_whole_spec(wts[n], **once) if n in whole
             else _layer_spec(wts[n], 0 if n in layer_wts else layer, **once)
             for n in names]
    return pl.pallas_call(
        functools.partial(_mix_kernel, final=final, layer=layer),
        grid=(b, s // tm),
        in_specs=[
            pl.BlockSpec((1, tm, d), lambda i, j: (i, j, 0)),
            pl.BlockSpec((1, ATTN_DIM, tm), lambda i, j: (i, 0, j)),
            _layer_spec(mod, layer, **once),
        ] + specs,
        out_specs=pl.BlockSpec((1, tm, d), lambda i, j: (i, j, 0)),
        out_shape=jax.ShapeDtypeStruct((b, s, d), F32),
        scratch_shapes=[pltpu.VMEM((tm + POOL_HALO, POOL_DIM), F32)],
        compiler_params=pltpu.CompilerParams(
            dimension_semantics=("arbitrary", "arbitrary"),
            vmem_limit_bytes=VMEM_LIMIT_BYTES),
        name="mix_mlp",
    )(x, ot, mod, *[wts[n] for n in names])


W_IN_PREP_ROWS = 256


def _w_in_prep_kernel(w_ref, a_ref, b_ref):
    wt = w_ref[0]
    rows = wt.shape[1]
    c0 = POOL_DIM
    c2 = c0 + Q_LORA + KV_LORA
    c3 = c2 + QK_ROPE
    half = QK_ROPE // 2
    kr = wt[c2:c3]
    zl = jnp.zeros((ROPE_LO, rows), F32)
    a_t = jnp.concatenate([wt[c0:c2], zl, kr, -kr[half:], kr[:half]], axis=0)
    a_ref[0] = a_t.T.astype(BF16)
    b_ref[0] = jnp.concatenate([wt[:c0], wt[c3:]], axis=0).T.astype(BF16)


def _w_in_prep(w_in):
    depth, d, n = w_in.shape
    na = Q_LORA + KV_LORA + HEAD_PAD
    nb = n - (Q_LORA + KV_LORA + QK_ROPE)
    rows = W_IN_PREP_ROWS
    w_in = jnp.swapaxes(w_in, 1, 2)
    return pl.pallas_call(
        _w_in_prep_kernel,
        grid=(depth, d // rows),
        in_specs=[pl.BlockSpec((1, n, rows), lambda l, r: (l, 0, r))],
        out_specs=[pl.BlockSpec((1, rows, na), lambda l, r: (l, r, 0)),
                   pl.BlockSpec((1, rows, nb), lambda l, r: (l, r, 0))],
        out_shape=[jax.ShapeDtypeStruct((depth, d, na), BF16), jax.ShapeDtypeStruct((depth, d, nb), BF16)],
        compiler_params=pltpu.CompilerParams(vmem_limit_bytes=VMEM_LIMIT_BYTES),
        name="w_in_prep",
    )(w_in)


def _prep_weights(ln1_g, ln2_g, w_in, q_norm_g, w_uq, kv_norm_g, w_uk, w_uv, w_pool, pool_scale, final_g):
    depth = w_in.shape[0]
    w_in_a, w_in_b = _w_in_prep(w_in)

    wuqt = w_uq.reshape(depth, Q_LORA, N_HEADS * ROPE_HI).transpose(0, 2, 1).astype(BF16)
    wuk = w_uk.reshape(depth, KV_LORA, N_HEADS * QK_NOPE).astype(BF16)
    wuvt = w_uv.reshape(depth, KV_LORA, ATTN_DIM).transpose(0, 2, 1).astype(BF16)
    return dict(
        ln1_g=ln1_g, ln2_g=ln2_g, final_g=final_g.reshape(1, D_MODEL), q_norm_g=q_norm_g,
        kv_norm_g=kv_norm_g, pool_scale=pool_scale, w_in_a=w_in_a, w_in_b=w_in_b, wuqt=wuqt,
        wuk=wuk, wuvt=wuvt, w_pool=w_pool.astype(BF16))


def kernel(x, c, positions, ln1_g, ln2_g, w_ada, b_ada, w_in, q_norm_g, w_uq, kv_norm_g, w_uk,
           w_uv, w_pool, pool_scale, p_pool, p_attn, w_out, w_ff1, w_ff2, final_g):
    depth = w_in.shape[0]
    mod = _modulation(c, w_ada, b_ada)
    wts = _prep_weights(ln1_g, ln2_g, w_in, q_norm_g, w_uq, kv_norm_g, w_uk, w_uv, w_pool, pool_scale,
                        final_g)
    cast_names = ("p_pool", "p_attn", "w_out", "w_ff1", "w_ff2")
    f32_weights = (p_pool, p_attn, w_out, w_ff1, w_ff2)
    tables = None
    for layer in range(depth):
        qt, k, vt, tables = _qkv(x, mod, positions, wts, layer, tables)
        ot, cast = _attention(qt, k, vt, f32_weights, layer)
        x = _mix(x, ot, mod, wts, dict(zip(cast_names, cast)), layer, final=(layer == depth - 1))
    return x
```
